```python
import math
import jax, jax.numpy as jnp
from jax import lax
import numpy as np

D_MODEL = 2048
BATCH = 8
SEQ = 8192
DEPTH = 4

CHUNK = 64
Q_BLOCK = 128
NORM_EPS = 1e-6
N_BRANCH = 3

RET_HEAD_DIM = 128
RET_WIDTH = D_MODEL // 2
RET_HEADS = RET_WIDTH // RET_HEAD_DIM
ROPE_BASE = 10000.0

SB_HEAD_DIM = 128
SB_WIDTH = D_MODEL // 2
SB_HEADS = SB_WIDTH // SB_HEAD_DIM

SSD_HEAD_DIM = 64
SSD_WIDTH = D_MODEL // 2
SSD_HEADS = SSD_WIDTH // SSD_HEAD_DIM
SSD_GROUPS = 4
SSD_HEADS_PER_GROUP = SSD_HEADS // SSD_GROUPS
SSD_STATE = 128
SSD_CONV = 4
SSD_CONV_DIM = SSD_WIDTH + 2 * SSD_GROUPS * SSD_STATE

FFN_HIDDEN = ((8 * D_MODEL // 3 + 255) // 256) * 256

IN_SIZES = (RET_WIDTH, RET_WIDTH, RET_WIDTH, RET_WIDTH,
            SB_WIDTH, SB_WIDTH, SB_WIDTH,
            SSD_WIDTH, SSD_CONV_DIM, SSD_HEADS,
            N_BRANCH * D_MODEL)
IN_COLS = sum(IN_SIZES)

kernel_name = 'hybrid_retention_stickbreaking_ssd_trunk'


def rms_norm(x, w):
    xf = x.astype(jnp.float32)
    y = xf * lax.rsqrt(jnp.mean(xf * xf, axis=-1, keepdims=True) + NORM_EPS)
    return (y * w.astype(jnp.float32)).astype(x.dtype)


def apply_rotary(t, positions):
    half = t.shape[-1] // 2
    inv_freq = ROPE_BASE ** (-2.0 * jnp.arange(half, dtype=jnp.float32) / t.shape[-1])
    ang = positions.astype(jnp.float32)[:, :, None] * inv_freq
    cos = jnp.cos(ang)[:, :, None, :]
    sin = jnp.sin(ang)[:, :, None, :]
    t1, t2 = t[..., :half], t[..., half:]
    return jnp.concatenate([t1 * cos - t2 * sin, t1 * sin + t2 * cos], axis=-1)


def retention_mixer(q, k, v, g, positions, gn_w):
    out_dtype = q.dtype
    b, s, _ = q.shape
    nc = s // CHUNK
    H, Dh = RET_HEADS, RET_HEAD_DIM
    q = apply_rotary(q.astype(jnp.float32).reshape(b, s, H, Dh), positions)
    k = apply_rotary(k.astype(jnp.float32).reshape(b, s, H, Dh), positions) * (Dh ** -0.5)
    v = v.astype(jnp.float32).reshape(b, s, H, Dh)
    log_gamma = jnp.log1p(-jnp.exp2(-5.0 - jnp.arange(H, dtype=jnp.float32)))
    idx = jnp.arange(CHUNK, dtype=jnp.float32)
    intra_decay = jnp.exp(log_gamma[:, None, None] * jnp.abs(idx[:, None] - idx[None, :]))
    q = q.reshape(b, nc, CHUNK, H, Dh)
    k = k.reshape(b, nc, CHUNK, H, Dh)
    v = v.reshape(b, nc, CHUNK, H, Dh)
    scores = jnp.einsum('bcihd,bcjhd->bchij', q, k) * intra_decay
    o_intra = jnp.einsum('bchij,bcjhd->bcihd', scores, v)
    q_decay = jnp.exp(log_gamma[None, :] * (idx[:, None] + 1.0))
    k_decay = jnp.exp(log_gamma[None, :] * (CHUNK - 1.0 - idx[:, None]))
    chunk_decay = jnp.exp(log_gamma * CHUNK)

    def step(state, inp):
        qc, kc, vc = inp
        cross = jnp.einsum('bihd,bhde->bihe', qc * q_decay[None, :, :, None], state)
        state = state * chunk_decay[None, :, None, None] + jnp.einsum(
            'bjhd,bjhe->bhde', kc * k_decay[None, :, :, None], vc)
        return state, cross

    state0 = jnp.zeros((b, H, Dh, Dh), jnp.float32)
    _, o_cross = lax.scan(step, state0, (jnp.moveaxis(q, 1, 0), jnp.moveaxis(k, 1, 0), jnp.moveaxis(v, 1, 0)))
    o = (o_intra + jnp.moveaxis(o_cross, 0, 1)).reshape(b, s, H, Dh)
    mu = jnp.mean(o, axis=-1, keepdims=True)
    var = jnp.mean(jnp.square(o - mu), axis=-1, keepdims=True)
    o = (o - mu) * lax.rsqrt(var + NORM_EPS) * gn_w.astype(jnp.float32).reshape(H, Dh)
    o = o.reshape(b, s, RET_WIDTH) * jax.nn.silu(g.astype(jnp.float32))
    return o.astype(out_dtype)


def stick_breaking_mixer(q, k, v):
    b, s, _ = q.shape
    H, Dh = SB_HEADS, SB_HEAD_DIM
    q = q.reshape(b, s, H, Dh).transpose(0, 2, 1, 3)
    k = k.reshape(b, s, H, Dh).transpose(0, 2, 1, 3)
    v = v.reshape(b, s, H, Dh).transpose(0, 2, 1, 3)
    key_pos = jnp.arange(s)
    scale = Dh ** -0.5

    def block(i):
        qb = lax.dynamic_slice_in_dim(q, i * Q_BLOCK, Q_BLOCK, axis=2)
        z = jnp.einsum('bhqd,bhkd->bhqk', qb, k).astype(jnp.float32) * scale
        t = i * Q_BLOCK + jnp.arange(Q_BLOCK)
        mask = key_pos[None, :] < t[:, None]
        log_keep = jnp.where(mask, jax.nn.log_sigmoid(-z), 0.0)
        later = lax.cumsum(log_keep, axis=3, reverse=True) - log_keep
        w = jnp.where(mask, jnp.exp(jax.nn.log_sigmoid(z) + later), 0.0)
        return jnp.einsum('bhqk,bhkd->bhqd', w.astype(v.dtype), v)

    out = lax.map(block, jnp.arange(s // Q_BLOCK))
    return out.transpose(1, 0, 3, 2, 4).reshape(b, s, SB_WIDTH)


def causal_depthwise_conv(u, w, bias):
    out = lax.conv_general_dilated(
        u, w[:, None, :], window_strides=(1,), padding=[(SSD_CONV - 1, 0)],
        dimension_numbers=('NWC', 'WIO', 'NWC'), feature_group_count=u.shape[-1])
    return out + bias


def ssd_mixer(z, xbc, dt_raw, conv_w, conv_b, dt_bias, a_log, d_skip, norm_w):
    out_dtype = z.dtype
    b, s, _ = z.shape
    nc = s // CHUNK
    G, E, P, N = SSD_GROUPS, SSD_HEADS_PER_GROUP, SSD_HEAD_DIM, SSD_STATE
    f32 = jnp.float32
    xbc = jax.nn.silu(causal_depthwise_conv(xbc.astype(f32), conv_w.astype(f32), conv_b.astype(f32)))
    x = xbc[..., :SSD_WIDTH].reshape(b, nc, CHUNK, G, E, P)
    Bm = xbc[..., SSD_WIDTH:SSD_WIDTH + G * N].reshape(b, nc, CHUNK, G, N)
    Cm = xbc[..., SSD_WIDTH + G * N:].reshape(b, nc, CHUNK, G, N)
    dt = jax.nn.softplus(dt_raw.astype(f32) + dt_bias.astype(f32)).reshape(b, nc, CHUNK, G, E)
    A = -jnp.exp(a_log.astype(f32)).reshape(G, E)
    a = dt * A
    acum = jnp.cumsum(a, axis=2)
    xdt = x * dt[..., None]
    acum_t = jnp.moveaxis(acum, 2, -1)
    seg = acum_t[..., :, None] - acum_t[..., None, :]
    causal = jnp.tril(jnp.ones((CHUNK, CHUNK), dtype=bool))
    decay = jnp.exp(jnp.where(causal, seg, -jnp.inf))
    cb = jnp.einsum('bclgn,bcsgn->bcgls', Cm, Bm)
    y_diag = jnp.einsum('bcgls,bcgels,bcsgep->bclgep', cb, decay, xdt)
    decay_states = jnp.exp(acum[:, :, -1:] - acum)
    states = jnp.einsum('bclgn,bclge,bclgep->bcgepn', Bm, decay_states, xdt)
    chunk_decay = jnp.exp(acum[:, :, -1])

    def step(state, inp):
        c_c, acum_c, st_c, dec_c = inp
        y_off = jnp.einsum('blgn,bgepn,blge->blgep', c_c, state, jnp.exp(acum_c))
        state = state * dec_c[..., None, None] + st_c
        return state, y_off

    state0 = jnp.zeros((b, G, E, P, N), f32)
    _, y_off = lax.scan(step, state0, (jnp.moveaxis(Cm, 1, 0), jnp.moveaxis(acum, 1, 0),
                                       jnp.moveaxis(states, 1, 0), jnp.moveaxis(chunk_decay, 1, 0)))
    y = y_diag + jnp.moveaxis(y_off, 0, 1) + x * d_skip.astype(f32).reshape(G, E)[..., None]
    y = y.reshape(b, s, SSD_WIDTH)
    y = rms_norm(y * jax.nn.silu(z.astype(f32)), norm_w)
    return y.astype(out_dtype)


def hybrid_layer(x, positions, n_mix_pre, n_mix_post, n_ffn_pre, n_ffn_post, w_in, b_gate, ret_gn_w,
                 conv_w, conv_b, dt_bias, a_log, d_skip, ssd_norm_w, w_br_ret, w_br_sb, w_br_ssd,
                 w_out, w_gate, w_up, w_down):
    b, s, d = x.shape
    h = rms_norm(x, n_mix_pre)
    proj = h @ w_in
    split_points = np.cumsum(IN_SIZES)[:-1].tolist()
    rq, rk, rv, rg, sq, sk, sv, sz, sxbc, sdt, gate_logits = jnp.split(proj, split_points, axis=-1)
    y_ret = retention_mixer(rq, rk, rv, rg, positions, ret_gn_w)
    y_sb = stick_breaking_mixer(sq, sk, sv)
    y_ssd = ssd_mixer(sz, sxbc, sdt, conv_w, conv_b, dt_bias, a_log, d_skip, ssd_norm_w)
    gates = jax.nn.sigmoid(gate_logits + b_gate).reshape(b, s, N_BRANCH, d)
    merged = (gates[:, :, 0] * (y_ret @ w_br_ret)
              + gates[:, :, 1] * (y_sb @ w_br_sb)
              + gates[:, :, 2] * (y_ssd @ w_br_ssd))
    x = x + rms_norm(merged @ w_out, n_mix_post)
    h = rms_norm(x, n_ffn_pre)
    f = (jax.nn.silu(h @ w_gate) * (h @ w_up)) @ w_down
    return x + rms_norm(f, n_ffn_post)


def _fwd_setup_inputs(seed: int = 0) -> dict:
    key = jax.random.key(seed)
    ks = jax.random.split(key, 24)
    f32 = jnp.float32

    def normal(k, shape, scale):
        return jax.random.normal(k, shape, f32) * scale

    def gain(k, shape):
        return 1.0 + 0.02 * jax.random.normal(k, shape, f32)

    dt_init = jnp.exp(jax.random.uniform(ks[10], (DEPTH, SSD_HEADS), f32)
                      * (math.log(0.1) - math.log(0.001)) + math.log(0.001))
    return {
        'x': jax.random.normal(ks[0], (BATCH, SEQ, D_MODEL), f32),
        'positions': jnp.broadcast_to(jnp.arange(SEQ, dtype=jnp.int32), (BATCH, SEQ)),
        'norm_mix_pre': gain(ks[1], (DEPTH, D_MODEL)),
        'norm_mix_post': gain(ks[2], (DEPTH, D_MODEL)),
        'norm_ffn_pre': gain(ks[3], (DEPTH, D_MODEL)),
        'norm_ffn_post': gain(ks[4], (DEPTH, D_MODEL)),
        'w_in': normal(ks[5], (DEPTH, D_MODEL, IN_COLS), D_MODEL ** -0.5),
        'b_gate': normal(ks[6], (DEPTH, N_BRANCH * D_MODEL), 0.01),
        'ret_gn_w': gain(ks[7], (DEPTH, RET_WIDTH)),
        'ssd_conv_w': normal(ks[8], (DEPTH, SSD_CONV, SSD_CONV_DIM), SSD_CONV ** -0.5),
        'ssd_conv_b': normal(ks[9], (DEPTH, SSD_CONV_DIM), 0.01),
        'ssd_dt_bias': dt_init + jnp.log(-jnp.expm1(-dt_init)),
        'ssd_a_log': jnp.log(jax.random.uniform(ks[11], (DEPTH, SSD_HEADS), f32, 1.0, 16.0)),
        'ssd_d': gain(ks[12], (DEPTH, SSD_HEADS)),
        'ssd_norm_w': gain(ks[13], (DEPTH, SSD_WIDTH)),
        'w_branch_ret': normal(ks[14], (DEPTH, RET_WIDTH, D_MODEL), RET_WIDTH ** -0.5),
        'w_branch_sb': normal(ks[15], (DEPTH, SB_WIDTH, D_MODEL), SB_WIDTH ** -0.5),
        'w_branch_ssd': normal(ks[16], (DEPTH, SSD_WIDTH, D_MODEL), SSD_WIDTH ** -0.5),
        'w_out': normal(ks[17], (DEPTH, D_MODEL, D_MODEL), D_MODEL ** -0.5),
        'ffn_w_gate': normal(ks[18], (DEPTH, D_MODEL, FFN_HIDDEN), D_MODEL ** -0.5),
        'ffn_w_up': normal(ks[19], (DEPTH, D_MODEL, FFN_HIDDEN), D_MODEL ** -0.5),
        'ffn_w_down': normal(ks[20], (DEPTH, FFN_HIDDEN, D_MODEL), FFN_HIDDEN ** -0.5),
    }


def _fwd_reference(x, positions, norm_mix_pre, norm_mix_post, norm_ffn_pre, norm_ffn_post, w_in, b_gate,
              ret_gn_w, ssd_conv_w, ssd_conv_b, ssd_dt_bias, ssd_a_log, ssd_d, ssd_norm_w,
              w_branch_ret, w_branch_sb, w_branch_ssd, w_out, ffn_w_gate, ffn_w_up, ffn_w_down):
    for l in range(DEPTH):
        x = hybrid_layer(x, positions, norm_mix_pre[l], norm_mix_post[l], norm_ffn_pre[l], norm_ffn_post[l],
                         w_in[l], b_gate[l], ret_gn_w[l], ssd_conv_w[l], ssd_conv_b[l], ssd_dt_bias[l],
                         ssd_a_log[l], ssd_d[l], ssd_norm_w[l], w_branch_ret[l], w_branch_sb[l],
                         w_branch_ssd[l], w_out[l], ffn_w_gate[l], ffn_w_up[l], ffn_w_down[l])
    return x


import jax as _jax
import jax.numpy as _jnp

TWIN_FORMAT = 'train_step'
FWD_PARAMS = ['x', 'positions', 'norm_mix_pre', 'norm_mix_post', 'norm_ffn_pre', 'norm_ffn_post', 'w_in', 'b_gate', 'ret_gn_w', 'ssd_conv_w', 'ssd_conv_b', 'ssd_dt_bias', 'ssd_a_log', 'ssd_d', 'ssd_norm_w', 'w_branch_ret', 'w_branch_sb', 'w_branch_ssd', 'w_out', 'ffn_w_gate', 'ffn_w_up', 'ffn_w_down']
TWIN_WEIGHTS = ['norm_mix_pre', 'norm_mix_post', 'norm_ffn_pre', 'norm_ffn_post', 'w_in', 'b_gate', 'ret_gn_w', 'ssd_conv_w', 'ssd_conv_b', 'ssd_dt_bias', 'ssd_a_log', 'ssd_d', 'ssd_norm_w', 'w_branch_ret', 'w_branch_sb', 'w_branch_ssd', 'w_out', 'ffn_w_gate', 'ffn_w_up', 'ffn_w_down']
TWIN_DIFF_INPUT = 'x'
TWIN_INPUTS = ['x', 'positions', 'norm_mix_pre', 'norm_mix_post', 'norm_ffn_pre', 'norm_ffn_post', 'w_in', 'b_gate', 'ret_gn_w', 'ssd_conv_w', 'ssd_conv_b', 'ssd_dt_bias', 'ssd_a_log', 'ssd_d', 'ssd_norm_w', 'w_branch_ret', 'w_branch_sb', 'w_branch_ssd', 'w_out', 'ffn_w_gate', 'ffn_w_up', 'ffn_w_down', 'loss_target', 'm_norm_mix_pre', 'm_norm_mix_post', 'm_norm_ffn_pre', 'm_norm_ffn_post', 'm_w_in', 'm_b_gate', 'm_ret_gn_w', 'm_ssd_conv_w', 'm_ssd_conv_b', 'm_ssd_dt_bias', 'm_ssd_a_log', 'm_ssd_d', 'm_ssd_norm_w', 'm_w_branch_ret', 'm_w_branch_sb', 'm_w_branch_ssd', 'm_w_out', 'm_ffn_w_gate', 'm_ffn_w_up', 'm_ffn_w_down', 'v_norm_mix_pre', 'v_norm_mix_post', 'v_norm_ffn_pre', 'v_norm_ffn_post', 'v_w_in', 'v_b_gate', 'v_ret_gn_w', 'v_ssd_conv_w', 'v_ssd_conv_b', 'v_ssd_dt_bias', 'v_ssd_a_log', 'v_ssd_d', 'v_ssd_norm_w', 'v_w_branch_ret', 'v_w_branch_sb', 'v_w_branch_ssd', 'v_w_out', 'v_ffn_w_gate', 'v_ffn_w_up', 'v_ffn_w_down']
TWIN_OUTPUTS = ['loss', 'grad_x', 'grad_norm_mix_pre', 'grad_norm_mix_post', 'grad_norm_ffn_pre', 'grad_norm_ffn_post', 'grad_w_in', 'grad_b_gate', 'grad_ret_gn_w', 'grad_ssd_conv_w', 'grad_ssd_conv_b', 'grad_ssd_dt_bias', 'grad_ssd_a_log', 'grad_ssd_d', 'grad_ssd_norm_w', 'grad_w_branch_ret', 'grad_w_branch_sb', 'grad_w_branch_ssd', 'grad_w_out', 'grad_ffn_w_gate', 'grad_ffn_w_up', 'grad_ffn_w_down', 'delta_norm_mix_pre', 'delta_norm_mix_post', 'delta_norm_ffn_pre', 'delta_norm_ffn_post', 'delta_w_in', 'delta_b_gate', 'delta_ret_gn_w', 'delta_ssd_conv_w', 'delta_ssd_conv_b', 'delta_ssd_dt_bias', 'delta_ssd_a_log', 'delta_ssd_d', 'delta_ssd_norm_w', 'delta_w_branch_ret', 'delta_w_branch_sb', 'delta_w_branch_ssd', 'delta_w_out', 'delta_ffn_w_gate', 'delta_ffn_w_up', 'delta_ffn_w_down', 'new_m_norm_mix_pre', 'new_m_norm_mix_post', 'new_m_norm_ffn_pre', 'new_m_norm_ffn_post', 'new_m_w_in', 'new_m_b_gate', 'new_m_ret_gn_w', 'new_m_ssd_conv_w', 'new_m_ssd_conv_b', 'new_m_ssd_dt_bias', 'new_m_ssd_a_log', 'new_m_ssd_d', 'new_m_ssd_norm_w', 'new_m_w_branch_ret', 'new_m_w_branch_sb', 'new_m_w_branch_ssd', 'new_m_w_out', 'new_m_ffn_w_gate', 'new_m_ffn_w_up', 'new_m_ffn_w_down', 'new_v_norm_mix_pre', 'new_v_norm_mix_post', 'new_v_norm_ffn_pre', 'new_v_norm_ffn_post', 'new_v_w_in', 'new_v_b_gate', 'new_v_ret_gn_w', 'new_v_ssd_conv_w', 'new_v_ssd_conv_b', 'new_v_ssd_dt_bias', 'new_v_ssd_a_log', 'new_v_ssd_d', 'new_v_ssd_norm_w', 'new_v_w_branch_ret', 'new_v_w_branch_sb', 'new_v_w_branch_ssd', 'new_v_w_out', 'new_v_ffn_w_gate', 'new_v_ffn_w_up', 'new_v_ffn_w_down']
TWIN_LEAF_KINDS = {'loss': 'loss', 'grad_x': 'grad_x', 'grad_norm_mix_pre': 'grad_w', 'grad_norm_mix_post': 'grad_w', 'grad_norm_ffn_pre': 'grad_w', 'grad_norm_ffn_post': 'grad_w', 'grad_w_in': 'grad_w', 'grad_b_gate': 'grad_w', 'grad_ret_gn_w': 'grad_w', 'grad_ssd_conv_w': 'grad_w', 'grad_ssd_conv_b': 'grad_w', 'grad_ssd_dt_bias': 'grad_w', 'grad_ssd_a_log': 'grad_w', 'grad_ssd_d': 'grad_w', 'grad_ssd_norm_w': 'grad_w', 'grad_w_branch_ret': 'grad_w', 'grad_w_branch_sb': 'grad_w', 'grad_w_branch_ssd': 'grad_w', 'grad_w_out': 'grad_w', 'grad_ffn_w_gate': 'grad_w', 'grad_ffn_w_up': 'grad_w', 'grad_ffn_w_down': 'grad_w', 'delta_norm_mix_pre': 'delta_w', 'delta_norm_mix_post': 'delta_w', 'delta_norm_ffn_pre': 'delta_w', 'delta_norm_ffn_post': 'delta_w', 'delta_w_in': 'delta_w', 'delta_b_gate': 'delta_w', 'delta_ret_gn_w': 'delta_w', 'delta_ssd_conv_w': 'delta_w', 'delta_ssd_conv_b': 'delta_w', 'delta_ssd_dt_bias': 'delta_w', 'delta_ssd_a_log': 'delta_w', 'delta_ssd_d': 'delta_w', 'delta_ssd_norm_w': 'delta_w', 'delta_w_branch_ret': 'delta_w', 'delta_w_branch_sb': 'delta_w', 'delta_w_branch_ssd': 'delta_w', 'delta_w_out': 'delta_w', 'delta_ffn_w_gate': 'delta_w', 'delta_ffn_w_up': 'delta_w', 'delta_ffn_w_down': 'delta_w', 'new_m_norm_mix_pre': 'new_m', 'new_m_norm_mix_post': 'new_m', 'new_m_norm_ffn_pre': 'new_m', 'new_m_norm_ffn_post': 'new_m', 'new_m_w_in': 'new_m', 'new_m_b_gate': 'new_m', 'new_m_ret_gn_w': 'new_m', 'new_m_ssd_conv_w': 'new_m', 'new_m_ssd_conv_b': 'new_m', 'new_m_ssd_dt_bias': 'new_m', 'new_m_ssd_a_log': 'new_m', 'new_m_ssd_d': 'new_m', 'new_m_ssd_norm_w': 'new_m', 'new_m_w_branch_ret': 'new_m', 'new_m_w_branch_sb': 'new_m', 'new_m_w_branch_ssd': 'new_m', 'new_m_w_out': 'new_m', 'new_m_ffn_w_gate': 'new_m', 'new_m_ffn_w_up': 'new_m', 'new_m_ffn_w_down': 'new_m', 'new_v_norm_mix_pre': 'new_v', 'new_v_norm_mix_post': 'new_v', 'new_v_norm_ffn_pre': 'new_v', 'new_v_norm_ffn_post': 'new_v', 'new_v_w_in': 'new_v', 'new_v_b_gate': 'new_v', 'new_v_ret_gn_w': 'new_v', 'new_v_ssd_conv_w': 'new_v', 'new_v_ssd_conv_b': 'new_v', 'new_v_ssd_dt_bias': 'new_v', 'new_v_ssd_a_log': 'new_v', 'new_v_ssd_d': 'new_v', 'new_v_ssd_norm_w': 'new_v', 'new_v_w_branch_ret': 'new_v', 'new_v_w_branch_sb': 'new_v', 'new_v_w_branch_ssd': 'new_v', 'new_v_w_out': 'new_v', 'new_v_ffn_w_gate': 'new_v', 'new_v_ffn_w_up': 'new_v', 'new_v_ffn_w_down': 'new_v'}


def _forward(args):
    return _fwd_reference(*[args[k] for k in FWD_PARAMS])


def _output_shape():
    def fwd():
        inp = _fwd_setup_inputs(0)
        return _fwd_reference(*[inp[k] for k in FWD_PARAMS])
    out = _jax.eval_shape(fwd)
    return out.shape, out.dtype

N_MICROBATCH = 1
ADAM_LR = 0.001
ADAM_B1 = 0.9
ADAM_B2 = 0.999
ADAM_EPS = 1e-08
ADAM_WD = 0.01
ADAM_STEP = 10
PER_EXAMPLE_BATCH_AXIS = {'x': 0, 'positions': 0, 'loss_target': 0}
SHARED_INPUTS = []
_WEIGHT_DTYPES = {'norm_mix_pre': _jnp.float32, 'norm_mix_post': _jnp.float32, 'norm_ffn_pre': _jnp.float32, 'norm_ffn_post': _jnp.float32, 'w_in': _jnp.float32, 'b_gate': _jnp.float32, 'ret_gn_w': _jnp.float32, 'ssd_conv_w': _jnp.float32, 'ssd_conv_b': _jnp.float32, 'ssd_dt_bias': _jnp.float32, 'ssd_a_log': _jnp.float32, 'ssd_d': _jnp.float32, 'ssd_norm_w': _jnp.float32, 'w_branch_ret': _jnp.float32, 'w_branch_sb': _jnp.float32, 'w_branch_ssd': _jnp.float32, 'w_out': _jnp.float32, 'ffn_w_gate': _jnp.float32, 'ffn_w_up': _jnp.float32, 'ffn_w_down': _jnp.float32}
MOMENT_SCALE = {'norm_mix_pre': 1.753873e+00, 'norm_mix_post': 3.139860e+01, 'norm_ffn_pre': 1.172922e+00, 'norm_ffn_post': 3.182597e+01, 'w_in': 5.873613e-01, 'b_gate': 3.119468e-01, 'ret_gn_w': 6.666362e-01, 'ssd_conv_w': 9.917615e-01, 'ssd_conv_b': 2.427844e+00, 'ssd_dt_bias': 3.781409e+00, 'ssd_a_log': 1.003781e+01, 'ssd_d': 7.707529e+00, 'ssd_norm_w': 1.541803e+00, 'w_branch_ret': 4.519525e-01, 'w_branch_sb': 5.674130e-01, 'w_branch_ssd': 1.139402e+00, 'w_out': 1.348524e+00, 'ffn_w_gate': 4.741980e-01, 'ffn_w_up': 5.225198e-01, 'ffn_w_down': 8.698188e-01}


def _to_microbatches(a, axis):
    t = _jnp.moveaxis(a, axis, 0)
    t = t.reshape((N_MICROBATCH, t.shape[0] // N_MICROBATCH) + t.shape[1:])
    return _jnp.moveaxis(t, 1, axis + 1)


def setup_inputs(seed: int = 0) -> dict:
    inp = _fwd_setup_inputs(seed)
    key = _jax.random.fold_in(_jax.random.key(seed), 7919)
    shape, _ = _output_shape()
    out = dict(inp)
    out["loss_target"] = _jax.random.normal(_jax.random.fold_in(key, 0), shape, _jnp.float32)
    for i, name in enumerate(TWIN_WEIGHTS):
        w = inp[name].astype(_jnp.float32)
        if MOMENT_SCALE is None:
            s = _jnp.sqrt(_jnp.mean(_jnp.square(w)) + 1e-30)
        else:
            s = MOMENT_SCALE[name]
        km, kv = _jax.random.split(_jax.random.fold_in(key, i + 1))
        out[name] = w
        out["m_" + name] = s * _jax.random.normal(km, w.shape, _jnp.float32)
        out["v_" + name] = (s * s) * _jax.random.uniform(kv, w.shape, _jnp.float32, 0.5, 1.5)
    if N_MICROBATCH > 1:
        for name, axis in PER_EXAMPLE_BATCH_AXIS.items():
            out[name] = _to_microbatches(out[name], axis)
    return {'x': out['x'], 'positions': out['positions'], 'norm_mix_pre': out['norm_mix_pre'], 'norm_mix_post': out['norm_mix_post'], 'norm_ffn_pre': out['norm_ffn_pre'], 'norm_ffn_post': out['norm_ffn_post'], 'w_in': out['w_in'], 'b_gate': out['b_gate'], 'ret_gn_w': out['ret_gn_w'], 'ssd_conv_w': out['ssd_conv_w'], 'ssd_conv_b': out['ssd_conv_b'], 'ssd_dt_bias': out['ssd_dt_bias'], 'ssd_a_log': out['ssd_a_log'], 'ssd_d': out['ssd_d'], 'ssd_norm_w': out['ssd_norm_w'], 'w_branch_ret': out['w_branch_ret'], 'w_branch_sb': out['w_branch_sb'], 'w_branch_ssd': out['w_branch_ssd'], 'w_out': out['w_out'], 'ffn_w_gate': out['ffn_w_gate'], 'ffn_w_up': out['ffn_w_up'], 'ffn_w_down': out['ffn_w_down'], 'loss_target': out['loss_target'], 'm_norm_mix_pre': out['m_norm_mix_pre'], 'm_norm_mix_post': out['m_norm_mix_post'], 'm_norm_ffn_pre': out['m_norm_ffn_pre'], 'm_norm_ffn_post': out['m_norm_ffn_post'], 'm_w_in': out['m_w_in'], 'm_b_gate': out['m_b_gate'], 'm_ret_gn_w': out['m_ret_gn_w'], 'm_ssd_conv_w': out['m_ssd_conv_w'], 'm_ssd_conv_b': out['m_ssd_conv_b'], 'm_ssd_dt_bias': out['m_ssd_dt_bias'], 'm_ssd_a_log': out['m_ssd_a_log'], 'm_ssd_d': out['m_ssd_d'], 'm_ssd_norm_w': out['m_ssd_norm_w'], 'm_w_branch_ret': out['m_w_branch_ret'], 'm_w_branch_sb': out['m_w_branch_sb'], 'm_w_branch_ssd': out['m_w_branch_ssd'], 'm_w_out': out['m_w_out'], 'm_ffn_w_gate': out['m_ffn_w_gate'], 'm_ffn_w_up': out['m_ffn_w_up'], 'm_ffn_w_down': out['m_ffn_w_down'], 'v_norm_mix_pre': out['v_norm_mix_pre'], 'v_norm_mix_post': out['v_norm_mix_post'], 'v_norm_ffn_pre': out['v_norm_ffn_pre'], 'v_norm_ffn_post': out['v_norm_ffn_post'], 'v_w_in': out['v_w_in'], 'v_b_gate': out['v_b_gate'], 'v_ret_gn_w': out['v_ret_gn_w'], 'v_ssd_conv_w': out['v_ssd_conv_w'], 'v_ssd_conv_b': out['v_ssd_conv_b'], 'v_ssd_dt_bias': out['v_ssd_dt_bias'], 'v_ssd_a_log': out['v_ssd_a_log'], 'v_ssd_d': out['v_ssd_d'], 'v_ssd_norm_w': out['v_ssd_norm_w'], 'v_w_branch_ret': out['v_w_branch_ret'], 'v_w_branch_sb': out['v_w_branch_sb'], 'v_w_branch_ssd': out['v_w_branch_ssd'], 'v_w_out': out['v_w_out'], 'v_ffn_w_gate': out['v_ffn_w_gate'], 'v_ffn_w_up': out['v_ffn_w_up'], 'v_ffn_w_down': out['v_ffn_w_down']}


def _loss(weights, diff, rest, loss_target):
    with _jax.named_scope("forward"):
        args = {**rest, TWIN_DIFF_INPUT: diff, **{k: w.astype(_WEIGHT_DTYPES[k]) for k, w in weights.items()}}
        y = _forward(args)
    with _jax.named_scope("loss_head"):
        err = _jnp.square(y.astype(_jnp.float32) - loss_target)
        return 0.5 * _jnp.sum(_jnp.mean(err, axis=-1)) if err.ndim else 0.5 * err


def _adamw(w, g, m, v):
    m = ADAM_B1 * m + (1.0 - ADAM_B1) * g
    v = ADAM_B2 * v + (1.0 - ADAM_B2) * _jnp.square(g)
    m_hat = m / (1.0 - ADAM_B1 ** ADAM_STEP)
    v_hat = v / (1.0 - ADAM_B2 ** ADAM_STEP)
    delta = -ADAM_LR * (m_hat / (_jnp.sqrt(v_hat) + ADAM_EPS) + ADAM_WD * w)
    return delta, m, v


def reference(x, positions, norm_mix_pre, norm_mix_post, norm_ffn_pre, norm_ffn_post, w_in, b_gate, ret_gn_w, ssd_conv_w, ssd_conv_b, ssd_dt_bias, ssd_a_log, ssd_d, ssd_norm_w, w_branch_ret, w_branch_sb, w_branch_ssd, w_out, ffn_w_gate, ffn_w_up, ffn_w_down, loss_target, m_norm_mix_pre, m_norm_mix_post, m_norm_ffn_pre, m_norm_ffn_post, m_w_in, m_b_gate, m_ret_gn_w, m_ssd_conv_w, m_ssd_conv_b, m_ssd_dt_bias, m_ssd_a_log, m_ssd_d, m_ssd_norm_w, m_w_branch_ret, m_w_branch_sb, m_w_branch_ssd, m_w_out, m_ffn_w_gate, m_ffn_w_up, m_ffn_w_down, v_norm_mix_pre, v_norm_mix_post, v_norm_ffn_pre, v_norm_ffn_post, v_w_in, v_b_gate, v_ret_gn_w, v_ssd_conv_w, v_ssd_conv_b, v_ssd_dt_bias, v_ssd_a_log, v_ssd_d, v_ssd_norm_w, v_w_branch_ret, v_w_branch_sb, v_w_branch_ssd, v_w_out, v_ffn_w_gate, v_ffn_w_up, v_ffn_w_down):
    given = dict(x=x, positions=positions, norm_mix_pre=norm_mix_pre, norm_mix_post=norm_mix_post, norm_ffn_pre=norm_ffn_pre, norm_ffn_post=norm_ffn_post, w_in=w_in, b_gate=b_gate, ret_gn_w=ret_gn_w, ssd_conv_w=ssd_conv_w, ssd_conv_b=ssd_conv_b, ssd_dt_bias=ssd_dt_bias, ssd_a_log=ssd_a_log, ssd_d=ssd_d, ssd_norm_w=ssd_norm_w, w_branch_ret=w_branch_ret, w_branch_sb=w_branch_sb, w_branch_ssd=w_branch_ssd, w_out=w_out, ffn_w_gate=ffn_w_gate, ffn_w_up=ffn_w_up, ffn_w_down=ffn_w_down, loss_target=loss_target, m_norm_mix_pre=m_norm_mix_pre, m_norm_mix_post=m_norm_mix_post, m_norm_ffn_pre=m_norm_ffn_pre, m_norm_ffn_post=m_norm_ffn_post, m_w_in=m_w_in, m_b_gate=m_b_gate, m_ret_gn_w=m_ret_gn_w, m_ssd_conv_w=m_ssd_conv_w, m_ssd_conv_b=m_ssd_conv_b, m_ssd_dt_bias=m_ssd_dt_bias, m_ssd_a_log=m_ssd_a_log, m_ssd_d=m_ssd_d, m_ssd_norm_w=m_ssd_norm_w, m_w_branch_ret=m_w_branch_ret, m_w_branch_sb=m_w_branch_sb, m_w_branch_ssd=m_w_branch_ssd, m_w_out=m_w_out, m_ffn_w_gate=m_ffn_w_gate, m_ffn_w_up=m_ffn_w_up, m_ffn_w_down=m_ffn_w_down, v_norm_mix_pre=v_norm_mix_pre, v_norm_mix_post=v_norm_mix_post, v_norm_ffn_pre=v_norm_ffn_pre, v_norm_ffn_post=v_norm_ffn_post, v_w_in=v_w_in, v_b_gate=v_b_gate, v_ret_gn_w=v_ret_gn_w, v_ssd_conv_w=v_ssd_conv_w, v_ssd_conv_b=v_ssd_conv_b, v_ssd_dt_bias=v_ssd_dt_bias, v_ssd_a_log=v_ssd_a_log, v_ssd_d=v_ssd_d, v_ssd_norm_w=v_ssd_norm_w, v_w_branch_ret=v_w_branch_ret, v_w_branch_sb=v_w_branch_sb, v_w_branch_ssd=v_w_branch_ssd, v_w_out=v_w_out, v_ffn_w_gate=v_ffn_w_gate, v_ffn_w_up=v_ffn_w_up, v_ffn_w_down=v_ffn_w_down)
    weights = {n: given[n] for n in TWIN_WEIGHTS}
    shared = {n: given[n] for n in SHARED_INPUTS}
    per_example = {n: given[n] for n in ['x', 'positions']}
    grad_fn = _jax.value_and_grad(_loss, argnums=(0, 1))

    def one_microbatch(ex, loss_target):
        ex = dict(ex)
        diff = ex.pop(TWIN_DIFF_INPUT)
        return grad_fn(weights, diff, {**shared, **ex}, loss_target)

    if N_MICROBATCH == 1:
        loss, (grad_w, grad_x) = one_microbatch(per_example, given["loss_target"])
    else:
        def body(carry, xs):
            loss_sum, grad_sum = carry
            l_k, (gw_k, gx_k) = one_microbatch(xs[0], xs[1])
            with _jax.named_scope("update"):
                return (loss_sum + l_k, _jax.tree.map(_jnp.add, grad_sum, gw_k)), gx_k

        init = (_jnp.zeros((), _jnp.float32), _jax.tree.map(_jnp.zeros_like, weights))
        (loss, grad_w), grad_x = _jax.lax.scan(body, init, (per_example, given["loss_target"]))
    with _jax.named_scope("update"):
        delta_w, new_m, new_v = {}, {}, {}
        for n in TWIN_WEIGHTS:
            delta_w[n], new_m[n], new_v[n] = _adamw(weights[n], grad_w[n], given["m_" + n], given["v_" + n])
    return (loss, grad_x, *[grad_w[n] for n in TWIN_WEIGHTS], *[delta_w[n] for n in TWIN_WEIGHTS],
            *[new_m[n] for n in TWIN_WEIGHTS], *[new_v[n] for n in TWIN_WEIGHTS])
```

```python
import functools
import math

import numpy as np
import jax
import jax.numpy as jnp
from jax import lax
from jax.experimental import pallas as pl
from jax.experimental.pallas import tpu as pltpu

F32 = jnp.float32
BF16 = jnp.bfloat16
MESH = pl.DeviceIdType.MESH

D_MODEL = 2048
DEPTH = 4
RET_CHUNK = 64
NORM_EPS = 1e-6
HEAD = 128
N_HEADS = 8
BR_WIDTH = 1024
SSD_P = 64
SSD_HEADS = 16
SSD_GROUPS = 4
SSD_N = 128
SSD_CHUNK = 128
SSD_CONV = 4
FFN_HIDDEN = 5632
IN_COLS = 16400
IN_PAD = 16896
COL_RET, COL_SB, COL_Z, COL_XBC, COL_GATE, COL_DT = 0, 4096, 7168, 8192, 10240, 16384
N_SHARD = 4
LANES = 128
VMEM_LIMIT = 48 * 1024 * 1024

ADAM_LR, ADAM_B1, ADAM_B2, ADAM_EPS, ADAM_WD, ADAM_STEP = 0.001, 0.9, 0.999, 1e-08, 0.01, 10


def _pick(n, prefs):
    for p in prefs:
        if n % p == 0:
            return p
    return n


def _params(sem=None):
    kw = dict(vmem_limit_bytes=VMEM_LIMIT)
    if sem is not None:
        kw["dimension_semantics"] = sem
    return pltpu.CompilerParams(**kw)


def matmul(a, b, *, nt=False, out_dtype=F32, name="matmul"):
    m, k = a.shape
    n = b.shape[0] if nt else b.shape[1]
    tm = _pick(m, (1024, 512, 256, 128))
    tn = _pick(n, (512, 256, 128))
    tk = _pick(k, (1024, 512, 256, 128))
    nk = k // tk

    def body(a_ref, b_ref, o_ref, acc_ref):
        kk = pl.program_id(2)
        if nt:
            part = lax.dot_general(a_ref[...], b_ref[...], (((1,), (1,)), ((), ())), preferred_element_type=F32)
        else:
            part = jnp.dot(a_ref[...], b_ref[...], preferred_element_type=F32)

        @pl.when(kk == 0)
        def _():
            acc_ref[...] = part

        @pl.when(kk > 0)
        def _():
            acc_ref[...] += part

        @pl.when(kk == nk - 1)
        def _():
            o_ref[...] = acc_ref[...].astype(o_ref.dtype)

    b_spec = pl.BlockSpec((tn, tk), lambda i, j, kk: (j, kk)) if nt else pl.BlockSpec((tk, tn), lambda i, j, kk: (kk, j))
    return pl.pallas_call(
        body, name=name, grid=(m // tm, n // tn, nk),
        in_specs=[pl.BlockSpec((tm, tk), lambda i, j, kk: (i, kk)), b_spec],
        out_specs=pl.BlockSpec((tm, tn), lambda i, j, kk: (i, j)),
        out_shape=jax.ShapeDtypeStruct((m, n), out_dtype),
        scratch_shapes=[pltpu.VMEM((tm, tn), F32)],
        compiler_params=_params(("parallel", "parallel", "arbitrary")),
    )(a, b)


def bmap(name, fn, grid, ins, outs, accs=()):
    n_in, n_out = len(ins), len(outs)
    last = len(grid) - 1

    def store_acc(ref, val, first):
        @pl.when(first)
        def _():
            ref[...] = val

        @pl.when(jnp.logical_not(first))
        def _():
            ref[...] += val

    def body(*refs):
        vals = fn(*[r[...] for r in refs[:n_in]])
        if not isinstance(vals, (tuple, list)):
            vals = (vals,)
        for r, v in zip(refs[n_in:n_in + n_out], vals[:n_out]):
            r[...] = v.astype(r.dtype)
        first = pl.program_id(last) == 0
        for r, v in zip(refs[n_in + n_out:], vals[n_out:]):
            store_acc(r, v.astype(F32), first)

    sem = ("parallel",) * last + (("arbitrary",) if accs else ("parallel",))
    res = pl.pallas_call(
        body, name=name, grid=grid,
        in_specs=[pl.BlockSpec(bs, im) for (_, bs, im) in ins],
        out_specs=[pl.BlockSpec(bs, im) for (_, _, bs, im) in outs] + [pl.BlockSpec(bs, im) for (_, bs, im) in accs],
        out_shape=[jax.ShapeDtypeStruct(s, dt) for (s, dt, _, _) in outs] + [jax.ShapeDtypeStruct(s, F32) for (s, _, _) in accs],
        compiler_params=_params(sem),
    )(*[a for (a, _, _) in ins])
    return res


def rows(a, tm, width=None, col=0):
    width = a.shape[1] if width is None else width
    assert col % width == 0
    cb = col // width
    return (a, (tm, width), lambda i: (i, cb))


def par(a):
    return (a, a.shape, lambda *_: (0,) * a.ndim)


def row_out(s, width, dtype, tm):
    return ((s, width), dtype, (tm, width), lambda i: (i, 0))


def col_acc(width):
    return ((1, width), (1, width), lambda i: (0, 0))


def _sigmoid(x):
    return 1.0 / (1.0 + jnp.exp(-x))


def _silu(x):
    return x * _sigmoid(x)


def _dsilu(x):
    s = _sigmoid(x)
    return s * (1.0 + x * (1.0 - s))


def _softplus(x):
    return jnp.maximum(x, 0.0) + jnp.log(1.0 + jnp.exp(-jnp.abs(x)))


def _split_bf16(x):
    hi = x.astype(BF16)
    lo = (x - hi.astype(F32)).astype(BF16)
    return hi, lo


def _dot_exact_mask(x, mask_bf16):
    hi, lo = _split_bf16(x)
    return (jnp.dot(hi, mask_bf16, preferred_element_type=F32)
            + jnp.dot(lo, mask_bf16, preferred_element_type=F32))


def rms_fwd(x, w, out_dtype, name, resid=None):
    s, d = x.shape
    tm = _pick(s, (256, 128))

    def fn(xv, wv, *rest):
        r = lax.rsqrt(jnp.mean(xv * xv, axis=-1, keepdims=True) + NORM_EPS)
        y = xv * r * wv
        if rest:
            y = y + rest[0]
        return (y,)

    ins = [rows(x, tm), par(w.reshape(1, d))]
    if resid is not None:
        ins.append(rows(resid, tm))
    return bmap(name, fn, (s // tm,), ins, [row_out(s, d, out_dtype, tm)])[0]


def rms_bwd(x, w, dy, name, resid=None, out_dtype=F32):
    s, d = x.shape
    tm = _pick(s, (256, 128))

    def fn(xv, wv, dyv, *rest):
        dyv = dyv.astype(F32)
        r = lax.rsqrt(jnp.mean(xv * xv, axis=-1, keepdims=True) + NORM_EPS)
        xn = xv * r
        g = dyv * wv
        dx = r * (g - xn * jnp.mean(g * xn, axis=-1, keepdims=True))
        if rest:
            dx = dx + rest[0]
        return dx, jnp.sum(dyv * xn, axis=0, keepdims=True)

    ins = [rows(x, tm), par(w.reshape(1, d)), rows(dy, tm)]
    if resid is not None:
        ins.append(rows(resid, tm))
    dx, dw = bmap(name, fn, (s // tm,), ins, [row_out(s, d, out_dtype, tm)], [col_acc(d)])
    return dx, dw[0]


def linear_attention(q, k, v, ac, ar, *, chunk, causal, reverse, heads, name, out_dtype=F32, out_2d=False,
                     parts_out=False):
    def dims(op):
        return (op[0].shape[0], HEAD) if isinstance(op, tuple) else (op.shape[1], op.shape[2])

    s_len, dk = dims(q)
    dv = dims(v)[1]
    L = chunk
    cpb = _pick(s_len // L, (8, 4, 2, 1))
    nb = s_len // (L * cpb)
    a_chunks = ac.shape[1]

    def blk(i):
        return nb - 1 - i if reverse else i

    def seq_spec(op):
        if isinstance(op, tuple):
            _, first_col, nh = op
            rep, cb = heads // nh, first_col // HEAD
            return pl.BlockSpec((L * cpb, HEAD), lambda h, i: (blk(i), cb + h // rep))
        rep = heads // op.shape[0]
        return pl.BlockSpec((None, L * cpb, op.shape[2]), lambda h, i: (h // rep, blk(i), 0))

    def a_spec(arr, shape):
        rep = heads // arr.shape[0]
        if a_chunks == 1:
            return pl.BlockSpec((None, 1) + shape, lambda h, i: (h // rep, 0, 0, 0))
        return pl.BlockSpec((None, cpb) + shape, lambda h, i: (h // rep, blk(i), 0, 0))

    def body(q_ref, k_ref, v_ref, ac_ref, ar_ref, *rest):
        o_refs, state_ref = rest[:-1], rest[-1]

        @pl.when(pl.program_id(1) == 0)
        def _():
            state_ref[...] = jnp.zeros_like(state_ref)

        row = lax.broadcasted_iota(jnp.int32, (L, L), 0)
        col = lax.broadcasted_iota(jnp.int32, (L, L), 1)

        def one(step, carry):
            c = cpb - 1 - step if reverse else step
            ai = 0 if a_chunks == 1 else c
            sl = pl.ds(pl.multiple_of(c * L, L), L)
            qc = q_ref[sl, :].astype(F32)
            kc = k_ref[sl, :].astype(F32)
            vf = v_ref[sl, :].astype(F32)
            vc = vf.astype(BF16)
            a_col = ac_ref[ai]
            a_row = ar_ref[ai]
            a_last = ac_ref[ai, pl.ds(L - 1, 1), :]
            seg = a_col - a_row
            if not causal:
                dec = jnp.exp(-jnp.abs(seg))
            elif reverse:
                dec = jnp.where(col > row, jnp.exp(jnp.minimum(-seg, 0.0)), 0.0)
            else:
                dec = jnp.where(row > col, jnp.exp(jnp.minimum(seg, 0.0)), 0.0)
            if reverse:
                qd, kd = jnp.exp(a_last - a_col), jnp.exp(a_col)
            else:
                qd, kd = jnp.exp(a_col), jnp.exp(a_last - a_col)
            sc = lax.dot_general(qc.astype(BF16), kc.astype(BF16), (((1,), (1,)), ((), ())),
                                 preferred_element_type=F32) * dec
            state = state_ref[...]
            intra = jnp.dot(sc.astype(BF16), vc, preferred_element_type=F32)
            cross = jnp.dot((qc * qd).astype(BF16), state.astype(BF16), preferred_element_type=F32)
            out = intra + cross
            if parts_out:
                o_refs[1][sl, :] = intra
                o_refs[2][sl, :] = cross
                o_refs[3][c] = state
            if causal:
                out = out + jnp.sum(qc * kc, axis=-1, keepdims=True) * vf
            o_refs[0][sl, :] = out.astype(o_refs[0].dtype)
            upd = lax.dot_general((kc * kd).astype(BF16), vc, (((0,), (0,)), ((), ())),
                                  preferred_element_type=F32)
            state_ref[...] = state * jnp.exp(a_last) + upd
            return carry

        lax.fori_loop(0, cpb, one, 0)

    if out_2d:
        out_spec = pl.BlockSpec((L * cpb, dv), lambda h, i: (blk(i), h))
        out_shape = jax.ShapeDtypeStruct((s_len, heads * dv), out_dtype)
    else:
        out_spec = pl.BlockSpec((None, L * cpb, dv), lambda h, i: (h, blk(i), 0))
        out_shape = jax.ShapeDtypeStruct((heads, s_len, dv), out_dtype)
    out_specs, out_shapes = [out_spec], [out_shape]
    if parts_out:
        assert not out_2d and out_dtype == F32
        out_specs += [out_spec, out_spec, pl.BlockSpec((None, cpb, dk, dv), lambda h, i: (h, blk(i), 0, 0))]
        out_shapes += [out_shape, out_shape, jax.ShapeDtypeStruct((heads, s_len // L, dk, dv), F32)]
    arrs = [op[0] if isinstance(op, tuple) else op for op in (q, k, v)]
    res = pl.pallas_call(
        body, name=name, grid=(heads, nb),
        in_specs=[seq_spec(q), seq_spec(k), seq_spec(v), a_spec(ac, (L, 1)), a_spec(ar, (1, L))],
        out_specs=out_specs, out_shape=out_shapes,
        scratch_shapes=[pltpu.VMEM((dk, dv), F32)],
        compiler_params=_params(("parallel", "arbitrary")),
    )(*arrs, ac, ar)
    return res if parts_out else res[0]


SB_BLOCK = 128


def _sb_tile(q, kb, scale, valid, tri):
    z = lax.dot_general(q, kb, (((1,), (1,)), ((), ())), preferred_element_type=F32) * scale
    sp = _softplus(z)
    lk = jnp.where(valid, -sp, 0.0)
    later_in = _dot_exact_mask(lk, tri)
    return z, lk, (z - sp) + later_in


def sb_forward(qkv, name="sb_fwd"):
    s_len = qkv.shape[0]
    B = SB_BLOCK
    nq = s_len // B
    scale = HEAD ** -0.5

    def body(q_ref, k_ref, v_ref, o_ref):
        i = pl.program_id(1)
        q = q_ref[...]
        row = lax.broadcasted_iota(jnp.int32, (B, B), 0)
        col = lax.broadcasted_iota(jnp.int32, (B, B), 1)
        tri_gt = (row > col).astype(BF16)
        strict = col < row

        def step(n, carry):
            acc, a_run = carry
            kbi = i - n
            sl = pl.ds(pl.multiple_of(kbi * B, B), B)
            valid = jnp.logical_or(kbi < i, strict)
            _, lk, expo = _sb_tile(q, k_ref[sl, :], scale, valid, tri_gt)
            w = jnp.where(valid, jnp.exp(expo + a_run), 0.0)
            acc = acc + jnp.dot(w.astype(BF16), v_ref[sl, :], preferred_element_type=F32)
            return acc, a_run + jnp.sum(lk, axis=-1, keepdims=True)

        acc, _ = lax.fori_loop(0, i + 1, step, (jnp.zeros((B, HEAD), F32), jnp.zeros((B, 1), F32)))
        o_ref[...] = acc

    return pl.pallas_call(
        body, name=name, grid=(N_HEADS, nq),
        in_specs=[pl.BlockSpec((B, HEAD), lambda h, i: (i, h)),
                  pl.BlockSpec((s_len, HEAD), lambda h, i: (0, N_HEADS + h)),
                  pl.BlockSpec((s_len, HEAD), lambda h, i: (0, 2 * N_HEADS + h))],
        out_specs=pl.BlockSpec((B, HEAD), lambda h, i: (i, h)),
        out_shape=jax.ShapeDtypeStruct((s_len, BR_WIDTH), F32),
        compiler_params=_params(("parallel", "arbitrary")),
    )(qkv, qkv, qkv)


def sb_backward(qkv, o, do, name="sb_bwd"):
    s_len = qkv.shape[0]
    B = SB_BLOCK
    nq = s_len // B
    scale = HEAD ** -0.5

    def body(q_ref, k_ref, v_ref, o_ref, do_ref, dq_ref, dk_ref, dv_ref):
        i = pl.program_id(1)

        @pl.when(i == 0)
        def _():
            dk_ref[...] = jnp.zeros_like(dk_ref)
            dv_ref[...] = jnp.zeros_like(dv_ref)

        q = q_ref[...]
        dof = do_ref[...]
        dob = dof.astype(BF16)
        delta = jnp.sum(dob.astype(F32) * o_ref[...], axis=-1, keepdims=True)
        row = lax.broadcasted_iota(jnp.int32, (B, B), 0)
        col = lax.broadcasted_iota(jnp.int32, (B, B), 1)
        tri_gt = (row > col).astype(BF16)
        tri_ge = (row >= col).astype(BF16)
        strict = col < row

        def step(n, carry):
            dq, a_run, e_run = carry
            kbi = i - n
            sl = pl.ds(pl.multiple_of(kbi * B, B), B)
            valid = jnp.logical_or(kbi < i, strict)
            kb = k_ref[sl, :]
            vb = v_ref[sl, :]
            z, lk, expo = _sb_tile(q, kb, scale, valid, tri_gt)
            wb = jnp.where(valid, jnp.exp(expo + a_run), 0.0).astype(BF16)
            beta = _sigmoid(z)
            dw = lax.dot_general(dob, vb, (((1,), (1,)), ((), ())), preferred_element_type=F32)
            e = dw * wb.astype(F32)
            suffix = e_run + _dot_exact_mask(e, tri_ge)
            dz = jnp.where(valid, e * (1.0 - beta) - beta * (delta - suffix), 0.0) * scale
            dzb = dz.astype(BF16)
            dq = dq + jnp.dot(dzb, kb, preferred_element_type=F32)
            dk_ref[sl, :] += lax.dot_general(dzb, q, (((0,), (0,)), ((), ())), preferred_element_type=F32)
            dv_ref[sl, :] += lax.dot_general(wb, dob, (((0,), (0,)), ((), ())),
                                             preferred_element_type=F32)
            return (dq, a_run + jnp.sum(lk, axis=-1, keepdims=True),
                    e_run + jnp.sum(e, axis=-1, keepdims=True))

        zero = jnp.zeros((B, 1), F32)
        dq, _, _ = lax.fori_loop(0, i + 1, step, (jnp.zeros((B, HEAD), F32), zero, zero))
        dq_ref[...] = dq

    blk = pl.BlockSpec((B, HEAD), lambda h, i: (i, h))
    full = pl.BlockSpec((s_len, HEAD), lambda h, i: (0, h))
    shape = jax.ShapeDtypeStruct((s_len, BR_WIDTH), F32)
    return pl.pallas_call(
        body, name=name, grid=(N_HEADS, nq),
        in_specs=[blk,
                  pl.BlockSpec((s_len, HEAD), lambda h, i: (0, N_HEADS + h)),
                  pl.BlockSpec((s_len, HEAD), lambda h, i: (0, 2 * N_HEADS + h)),
                  blk, blk],
        out_specs=[blk, full, full],
        out_shape=[shape, shape, shape],
        compiler_params=_params(("parallel", "arbitrary")),
    )(qkv, qkv, qkv, o, do)


ROPE_BASE = 10000.0


def rope_tables(positions):
    half = HEAD // 2
    inv_freq = ROPE_BASE ** (-2.0 * jnp.arange(half, dtype=F32) / HEAD)
    ang = positions.astype(F32)[:, None] * inv_freq
    cos, sin = jnp.cos(ang), jnp.sin(ang)
    return jnp.concatenate([cos, cos], 1), jnp.concatenate([-sin, sin], 1)


def ret_decay_tables():
    lg = np.log1p(-np.exp2(-5.0 - np.arange(N_HEADS, dtype=np.float64)))
    ac = (lg[:, None] * (np.arange(RET_CHUNK) + 1.0)[None, :]).astype(np.float32)
    return jnp.asarray(ac[:, None, :, None]), jnp.asarray(ac[:, None, None, :])


def _head_major(s, tm, d=HEAD, heads=N_HEADS, dtype=F32):
    return ((heads, s, d), dtype, (None, tm, d), lambda h, i: (h, i, 0))


def _head_cols(arr, tm, first_col):
    cb = first_col // HEAD
    return (arr, (tm, HEAD), lambda h, i: (i, cb + h))


def _head_rows(arr, tm):
    return (arr, (None, tm, arr.shape[2]), lambda h, i: (h, i, 0))


def rotary_fwd(proj, cos2, sin2):
    s = proj.shape[0]
    tm = _pick(s, (512, 256, 128))

    def fn(qv, kv, c, sn):
        def rot(t):
            return t * c + pltpu.roll(t, HEAD // 2, 1) * sn
        return rot(qv), rot(kv) * (HEAD ** -0.5)

    tab = lambda t: (t, (tm, HEAD), lambda h, i: (i, 0))
    return bmap("rotary_fwd", fn, (N_HEADS, s // tm),
                [_head_cols(proj, tm, COL_RET), _head_cols(proj, tm, COL_RET + BR_WIDTH), tab(cos2), tab(sin2)],
                [_head_major(s, tm), _head_major(s, tm)])


def rotary_bwd(dqr, dkr, cos2, sin2):
    s = dqr.shape[1]
    tm = _pick(s, (512, 256, 128))

    def fn(dq, dk, c, sn):
        def unrot(t):
            return t * c - pltpu.roll(t, HEAD // 2, 1) * sn
        return unrot(dq), unrot(dk) * (HEAD ** -0.5)

    tab = lambda t: (t, (tm, HEAD), lambda h, i: (i, 0))
    out = ((s, BR_WIDTH), BF16, (tm, HEAD), lambda h, i: (i, h))
    return bmap("rotary_bwd", fn, (N_HEADS, s // tm),
                [_head_rows(dqr, tm), _head_rows(dkr, tm), tab(cos2), tab(sin2)], [out, out])


def _group_norm(o):
    mu = jnp.mean(o, axis=-1, keepdims=True)
    oc = o - mu
    r = lax.rsqrt(jnp.mean(oc * oc, axis=-1, keepdims=True) + NORM_EPS)
    return oc * r, r


def gn_gate_fwd(o, gn_w, proj):
    s = o.shape[1]
    tm = _pick(s, (512, 256, 128))

    def fn(ov, wv, gv):
        on, _ = _group_norm(ov)
        return (on * wv * _silu(gv),)

    w3 = gn_w.reshape(N_HEADS, 1, HEAD)
    return bmap("gn_gate_fwd", fn, (N_HEADS, s // tm),
                [_head_rows(o, tm), (w3, (None, 1, HEAD), lambda h, i: (h, 0, 0)),
                 _head_cols(proj, tm, COL_RET + 3 * BR_WIDTH)],
                [((s, BR_WIDTH), BF16, (tm, HEAD), lambda h, i: (i, h))])[0]


def gn_gate_bwd(o, gn_w, proj, dy):
    s = o.shape[1]
    tm = _pick(s, (512, 256, 128))

    def fn(ov, wv, gv, dyv):
        on, r = _group_norm(ov)
        yn = on * wv
        dyn = dyv * _silu(gv)
        dg = dyv * yn * _dsilu(gv)
        don = dyn * wv
        do = r * (don - jnp.mean(don, axis=-1, keepdims=True) - on * jnp.mean(don * on, axis=-1, keepdims=True))
        return do, dg, jnp.sum(dyn * on, axis=0, keepdims=True)

    w3 = gn_w.reshape(N_HEADS, 1, HEAD)
    do, dg, dw = bmap("gn_gate_bwd", fn, (N_HEADS, s // tm),
                      [_head_rows(o, tm), (w3, (None, 1, HEAD), lambda h, i: (h, 0, 0)),
                       _head_cols(proj, tm, COL_RET + 3 * BR_WIDTH), _head_cols(dy, tm, 0)],
                      [_head_major(s, tm), ((s, BR_WIDTH), BF16, (tm, HEAD), lambda h, i: (i, h))],
                      [((N_HEADS, 1, HEAD), (None, 1, HEAD), lambda h, i: (h, 0, 0))])
    return do, dg, dw.reshape(BR_WIDTH)


XBC_WIDTH = 2048


def _shifted(u, back):
    s = u.shape[0]
    n = SSD_CONV - 1
    if back:
        return [jnp.pad(u, ((0, n - k), (0, 0)))[n - k:] for k in range(n)]
    return [jnp.pad(u, ((n - k, 0), (0, 0)))[:s] for k in range(n)]


def _conv(us, w, b):
    return us[0] * w[0:1] + us[1] * w[1:2] + us[2] * w[2:3] + us[3] * w[3:4] + b


def ssd_conv_fwd(us, conv_w, conv_b):
    s = us[0].shape[0]
    tm = _pick(s, (256, 128))

    def fn(u0, u1, u2, u3, w, b):
        return (_silu(_conv((u0, u1, u2, u3), w, b)),)

    return bmap("ssd_conv_fwd", fn, (s // tm,), [rows(u, tm) for u in us] + [par(conv_w), par(conv_b)],
                [row_out(s, XBC_WIDTH, F32, tm)])[0]


def ssd_conv_bwd_a(us, conv_w, conv_b, dact):
    s = dact.shape[0]
    tm = _pick(s, (256, 128))

    def fn(u0, u1, u2, u3, w, b, da):
        dcv = da * _dsilu(_conv((u0, u1, u2, u3), w, b))
        sums = [jnp.sum(dcv * u, axis=0, keepdims=True) for u in (u0, u1, u2, u3)]
        return (dcv, *sums, jnp.sum(dcv, axis=0, keepdims=True))

    res = bmap("ssd_conv_bwd_a", fn, (s // tm,),
               [rows(u, tm) for u in us] + [par(conv_w), par(conv_b), rows(dact, tm)],
               [row_out(s, XBC_WIDTH, F32, tm)], [col_acc(XBC_WIDTH)] * 5)
    return res[0], jnp.concatenate(res[1:5], axis=0), res[5][0]


def ssd_conv_bwd_b(dcs, conv_w):
    s = dcs[0].shape[0]
    tm = _pick(s, (256, 128))

    def fn(d0, d1, d2, d3, w):
        return (d0 * w[0:1] + d1 * w[1:2] + d2 * w[2:3] + d3 * w[3:4],)

    return bmap("ssd_conv_bwd_b", fn, (s // tm,), [rows(d, tm) for d in dcs] + [par(conv_w)],
                [row_out(s, XBC_WIDTH, BF16, tm)])[0]


def _mask_dot_exact(mask_bf16, x):
    hi = x.astype(BF16)
    r1 = x - hi.astype(F32)
    mid = r1.astype(BF16)
    lo = (r1 - mid.astype(F32)).astype(BF16)
    return (jnp.dot(mask_bf16, hi, preferred_element_type=F32) + jnp.dot(mask_bf16, mid, preferred_element_type=F32)
            + jnp.dot(mask_bf16, lo, preferred_element_type=F32))


def ssd_dt_fwd(proj, dt_bias, a_log):
    s = proj.shape[0]
    L = SSD_CHUNK

    def fn(dtr, bias, alog):
        dt = _softplus(dtr + bias)
        a = dt * (-jnp.exp(alog))
        row = lax.broadcasted_iota(jnp.int32, (L, L), 0)
        col = lax.broadcasted_iota(jnp.int32, (L, L), 1)
        return dt, _mask_dot_exact((row >= col).astype(BF16), a)

    return bmap("ssd_dt_fwd", fn, (s // L,), [rows(proj, L, LANES, COL_DT), par(dt_bias), par(a_log)],
                [row_out(s, LANES, F32, L), row_out(s, LANES, F32, L)])


def ssd_state_dot(s_fwd, s_rev, a_tot):
    heads, nc = s_fwd.shape[0], s_fwd.shape[1]
    cb = _pick(nc, (8, 4, 2, 1))

    def fn(sf, sr, at):
        return (jnp.sum(jnp.sum(sf * sr, axis=2, keepdims=True), axis=1, keepdims=True) * jnp.exp(at),)

    st = lambda a: (a, (None, cb) + a.shape[2:], lambda h, i: (h, i, 0, 0))
    return bmap("ssd_state_dot", fn, (heads, nc // cb), [st(s_fwd), st(s_rev), st(a_tot)],
                [((heads, nc, 1, 1), F32, (None, cb, 1, 1), lambda h, i: (h, i, 0, 0))])[0]


def ssd_decay_grad(e_incl, v_excl, g_chunk):
    s = e_incl.shape[0]
    L = SSD_CHUNK

    def fn(ev, vv, gv):
        row = lax.broadcasted_iota(jnp.int32, (L, L), 0)
        col = lax.broadcasted_iota(jnp.int32, (L, L), 1)
        return (_mask_dot_exact((col >= row).astype(BF16), ev) + _mask_dot_exact((col < row).astype(BF16), vv) + gv,)

    return bmap("ssd_decay_grad", fn, (s // L,),
                [rows(e_incl, L), rows(v_excl, L), (g_chunk, (None, 1, LANES), lambda i: (i, 0, 0))],
                [row_out(s, LANES, F32, L)])[0]


def ssd_dt_bwd(da, ddt_x, proj, dt_bias, a_log):
    s = da.shape[0]
    tm = _pick(s, (512, 256, 128))

    def fn(dav, dxv, dtr, bias, alog):
        pre = dtr + bias
        a_coef = -jnp.exp(alog)
        ddtr = (dav * a_coef + dxv) * _sigmoid(pre)
        return (ddtr, jnp.sum(ddtr, axis=0, keepdims=True),
                jnp.sum(dav * _softplus(pre) * a_coef, axis=0, keepdims=True))

    ddtr, dbias, dalog = bmap("ssd_dt_bwd", fn, (s // tm,),
                              [rows(da, tm), rows(ddt_x, tm), rows(proj, tm, LANES, COL_DT), par(dt_bias), par(a_log)],
                              [row_out(s, LANES, BF16, tm)], [col_acc(LANES), col_acc(LANES)])
    return ddtr, dbias[0], dalog[0]


def ssd_xdt(x_h, dt_h):
    s = x_h.shape[1]
    tm = _pick(s, (512, 256, 128))
    return bmap("ssd_xdt", lambda xv, dv: (xv * dv,), (SSD_HEADS, s // tm),
                [_head_rows(x_h, tm), _head_rows(dt_h, tm)], [_head_major(s, tm, SSD_P, SSD_HEADS)])[0]


def ssd_gate_fwd(ytil, act, proj, d_full, norm_w):
    s = ytil.shape[0]
    tm = _pick(s, (256, 128))

    def fn(yt, xv, zv, dv, wv):
        u = (yt + xv * dv) * _silu(zv)
        return (u * lax.rsqrt(jnp.mean(u * u, axis=-1, keepdims=True) + NORM_EPS) * wv,)

    return bmap("ssd_gate_fwd", fn, (s // tm,),
                [rows(ytil, tm), rows(act, tm, BR_WIDTH, 0), rows(proj, tm, BR_WIDTH, COL_Z), par(d_full), par(norm_w)],
                [row_out(s, BR_WIDTH, BF16, tm)])[0]


def ssd_gate_bwd(ytil, act, proj, d_full, norm_w, dout):
    s = ytil.shape[0]
    tm = _pick(s, (256, 128))

    def fn(yt, xv, zv, dv, wv, dov):
        y = yt + xv * dv
        sz = _silu(zv)
        u = y * sz
        r = lax.rsqrt(jnp.mean(u * u, axis=-1, keepdims=True) + NORM_EPS)
        un = u * r
        g = dov * wv
        du = r * (g - un * jnp.mean(g * un, axis=-1, keepdims=True))
        dy = du * sz
        return (dy, du * y * _dsilu(zv), jnp.sum(dov * un, axis=0, keepdims=True),
                jnp.sum(dy * xv, axis=0, keepdims=True))

    dy, dz, dw, dd = bmap("ssd_gate_bwd", fn, (s // tm,),
                          [rows(ytil, tm), rows(act, tm, BR_WIDTH, 0), rows(proj, tm, BR_WIDTH, COL_Z), par(d_full),
                           par(norm_w), rows(dout, tm)],
                          [row_out(s, BR_WIDTH, F32, tm), row_out(s, BR_WIDTH, BF16, tm)],
                          [col_acc(BR_WIDTH), col_acc(BR_WIDTH)])
    return dy, dz, dw[0], dd[0]


def ssd_head_bwd(dy_h, y_parts, dxdt_h, dx_parts, xdt_h, x_h, dt_h, d_h):
    s = dy_h.shape[1]
    tm = _pick(s, (512, 256, 128))

    def fn(dy, yi, yc, dxdt, dxi, dxc, xdt, xv, dt, dskip):
        return (dy * dskip + dxdt * dt,
                jnp.sum(dy * (yi + yc) - dxi * xdt, axis=-1, keepdims=True),
                jnp.sum(dxc * xdt, axis=-1, keepdims=True),
                jnp.sum(dxdt * xv, axis=-1, keepdims=True))

    one = ((SSD_HEADS, s, 1), F32, (None, tm, 1), lambda h, i: (h, i, 0))
    return bmap("ssd_head_bwd", fn, (SSD_HEADS, s // tm),
                [_head_rows(a, tm) for a in (dy_h, *y_parts, dxdt_h, *dx_parts, xdt_h, x_h, dt_h)]
                + [(d_h, (None, 1, 1), lambda h, i: (h, 0, 0))],
                [_head_major(s, tm, SSD_P, SSD_HEADS), one, one, one])


def ssd_group_sum(db_h, dc_h):
    s = db_h.shape[1]
    tm = _pick(s, (512, 256, 128))
    e = SSD_HEADS // SSD_GROUPS

    def fn(*v):
        return v[0] + v[1] + v[2] + v[3], v[4] + v[5] + v[6] + v[7]

    def head(arr, j):
        return (arr, (None, tm, SSD_N), lambda g, i: (g * e + j, i, 0))

    out = ((s, SSD_GROUPS * SSD_N), F32, (tm, SSD_N), lambda g, i: (i, g))
    return bmap("ssd_group_sum", fn, (SSD_GROUPS, s // tm),
                [head(db_h, j) for j in range(e)] + [head(dc_h, j) for j in range(e)], [out, out])


def _gate_ins(proj, b_gate, tm):
    b2 = b_gate.reshape(1, 3 * D_MODEL)
    ins = [rows(proj, tm, D_MODEL, COL_GATE + j * D_MODEL) for j in range(3)]
    ins += [(b2, (1, D_MODEL), (lambda j: lambda i: (0, j))(j)) for j in range(3)]
    return ins


def merge_fwd(proj, b_gate, ps):
    s = proj.shape[0]
    tm = 128

    def fn(g0, g1, g2, b0, b1, b2, p0, p1, p2):
        return (_sigmoid(g0 + b0) * p0 + _sigmoid(g1 + b1) * p1 + _sigmoid(g2 + b2) * p2,)

    return bmap("merge_fwd", fn, (s // tm,), _gate_ins(proj, b_gate, tm) + [rows(p, tm) for p in ps],
                [row_out(s, D_MODEL, BF16, tm)])[0]


def merge_bwd(proj, b_gate, ps, dmerged):
    s = proj.shape[0]
    tm = 128

    def fn(g0, g1, g2, b0, b1, b2, p0, p1, p2, dm):
        dps, dgs = [], []
        for g, b, p in ((g0, b0, p0), (g1, b1, p1), (g2, b2, p2)):
            gate = _sigmoid(g + b)
            dps.append(dm * gate)
            dgs.append(dm * p * gate * (1.0 - gate))
        dgl = jnp.concatenate(dgs, axis=1)
        return (*dps, dgl, jnp.sum(dgl, axis=0, keepdims=True))

    res = bmap("merge_bwd", fn, (s // tm,),
               _gate_ins(proj, b_gate, tm) + [rows(p, tm) for p in ps] + [rows(dmerged, tm)],
               [row_out(s, D_MODEL, BF16, tm)] * 3 + [row_out(s, 3 * D_MODEL, BF16, tm)], [col_acc(3 * D_MODEL)])
    return res[0:3], res[3], res[4][0]


def swiglu_fwd(gu):
    s = gu.shape[0]
    tm = 128
    return bmap("swiglu_fwd", lambda g, u: (_silu(g) * u,), (s // tm,),
                [rows(gu, tm, FFN_HIDDEN, 0), rows(gu, tm, FFN_HIDDEN, FFN_HIDDEN)],
                [row_out(s, FFN_HIDDEN, BF16, tm)])[0]


def swiglu_bwd(gu, dact):
    s = gu.shape[0]
    tm = 128

    def fn(g, u, da):
        return (jnp.concatenate([da * u * _dsilu(g), da * _silu(g)], axis=1),)

    return bmap("swiglu_bwd", fn, (s // tm,),
                [rows(gu, tm, FFN_HIDDEN, 0), rows(gu, tm, FFN_HIDDEN, FFN_HIDDEN), rows(dact, tm)],
                [row_out(s, 2 * FFN_HIDDEN, BF16, tm)])[0]


def loss_head(y, target):
    s, d = y.shape
    tm = _pick(s, (256, 128))

    def fn(yv, tv):
        err = yv - tv
        part = jnp.sum(jnp.mean(err * err, axis=-1, keepdims=True), axis=0, keepdims=True)
        return err * (1.0 / d), 0.5 * part

    dy, loss = bmap("loss_head", fn, (s // tm,), [rows(y, tm), rows(target, tm)], [row_out(s, d, F32, tm)],
                    [((1, 1), (1, 1), lambda i: (0, 0))])
    return dy, loss[0, 0]


def _to_heads(a2d, heads, d):
    return a2d.reshape(a2d.shape[0], heads, d).transpose(1, 0, 2)


def _from_heads(a3d):
    return a3d.transpose(1, 0, 2).reshape(a3d.shape[1], a3d.shape[0] * a3d.shape[2])


def _pad_lanes(v):
    return jnp.pad(v.reshape(1, -1), ((0, 0), (0, LANES - v.shape[-1])))


def layer_forward(x, w, p, tabs):
    s = x.shape[0]
    cos2, sin2, ac_ret, ar_ret = tabs
    sv = {"x": x}
    h = rms_fwd(x, p["norm_mix_pre"], BF16, "rms_mix_pre")
    proj = matmul(h, w["w_in"], name="mm_in")
    sv.update(h=h, proj=proj)

    qr, kr = rotary_fwd(proj, cos2, sin2)
    v_ret = (proj, COL_RET + 2 * BR_WIDTH, N_HEADS)
    o_ret = linear_attention(qr, kr, v_ret, ac_ret, ar_ret, chunk=RET_CHUNK, causal=False, reverse=False,
                             heads=N_HEADS, name="ret_fwd")
    y_ret = gn_gate_fwd(o_ret, p["ret_gn_w"], proj)
    sv.update(qr=qr, kr=kr, o_ret=o_ret, y_ret=y_ret)

    qkv = proj[:, COL_SB:COL_SB + 3 * BR_WIDTH].astype(BF16)
    o_sb = sb_forward(qkv)
    y_sb = o_sb.astype(BF16)
    sv.update(qkv=qkv, o_sb=o_sb, y_sb=y_sb)

    xbc = proj[:, COL_XBC:COL_XBC + XBC_WIDTH]
    us = _shifted(xbc, False) + [xbc]
    conv_b = p["ssd_conv_b"].reshape(1, XBC_WIDTH)
    act = ssd_conv_fwd(us, p["ssd_conv_w"], conv_b)
    dt, acum = ssd_dt_fwd(proj, _pad_lanes(p["ssd_dt_bias"]), _pad_lanes(p["ssd_a_log"]))
    nc = s // SSD_CHUNK
    x_h = _to_heads(act[:, :BR_WIDTH], SSD_HEADS, SSD_P)
    dt_h = dt[:, :SSD_HEADS].T[:, :, None]
    acs = acum[:, :SSD_HEADS].T.reshape(SSD_HEADS, nc, SSD_CHUNK)
    ac_h, ar_h = acs[:, :, :, None], acs[:, :, None, :]
    xdt_h = ssd_xdt(x_h, dt_h)
    b_grp = (act, BR_WIDTH, SSD_GROUPS)
    c_grp = (act, BR_WIDTH + SSD_GROUPS * SSD_N, SSD_GROUPS)
    ytil_h, *y_parts, s_fwd = linear_attention(c_grp, b_grp, xdt_h, ac_h, ar_h, chunk=SSD_CHUNK, causal=True,
                                               reverse=False, heads=SSD_HEADS, name="ssd_fwd", parts_out=True)
    ytil = _from_heads(ytil_h)
    d_full = jnp.repeat(p["ssd_d"], SSD_P).reshape(1, BR_WIDTH)
    norm_w = p["ssd_norm_w"].reshape(1, BR_WIDTH)
    y_ssd = ssd_gate_fwd(ytil, act, proj, d_full, norm_w)
    sv.update(act=act, x_h=x_h, dt_h=dt_h, ac_h=ac_h, ar_h=ar_h, xdt_h=xdt_h, y_parts=y_parts, s_fwd=s_fwd, ytil=ytil,
              y_ssd=y_ssd)

    ys = (y_ret, y_sb, y_ssd)
    ps = [matmul(y, wb, name="mm_branch") for y, wb in zip(ys, w["w_branch"])]
    merged = merge_fwd(proj, p["b_gate"], ps)
    mo = matmul(merged, w["w_out"], name="mm_out")
    x1 = rms_fwd(mo, p["norm_mix_post"], F32, "rms_mix_post", resid=x)
    sv.update(ps=ps, merged=merged, mo=mo, x1=x1)

    h2 = rms_fwd(x1, p["norm_ffn_pre"], BF16, "rms_ffn_pre")
    gu = matmul(h2, w["w_gu"], name="mm_gu")
    actf = swiglu_fwd(gu)
    f = matmul(actf, w["w_down"], name="mm_down")
    x2 = rms_fwd(f, p["norm_ffn_post"], F32, "rms_ffn_post", resid=x1)
    sv.update(h2=h2, gu=gu, actf=actf, f=f)
    return x2, sv


def layer_backward(dx2, sv, w, p, tabs):
    cos2, sin2, ac_ret, ar_ret = tabs
    proj = sv["proj"]
    s = proj.shape[0]
    gs = {}

    df, gs["norm_ffn_post"] = rms_bwd(sv["f"], p["norm_ffn_post"], dx2, "rms_ffn_post_bwd", out_dtype=BF16)
    dactf = matmul(df, w["w_down"], nt=True, name="mm_down_dx")
    gw_down = matmul(sv["actf"].T, df, out_dtype=BF16, name="mm_down_dw")
    dgu = swiglu_bwd(sv["gu"], dactf)
    dh2 = matmul(dgu, w["w_gu"], nt=True, name="mm_gu_dx")
    gw_gu = matmul(sv["h2"].T, dgu, out_dtype=BF16, name="mm_gu_dw")
    dx1, gs["norm_ffn_pre"] = rms_bwd(sv["x1"], p["norm_ffn_pre"], dh2, "rms_ffn_pre_bwd", resid=dx2)

    dmo, gs["norm_mix_post"] = rms_bwd(sv["mo"], p["norm_mix_post"], dx1, "rms_mix_post_bwd", out_dtype=BF16)
    dmerged = matmul(dmo, w["w_out"], nt=True, name="mm_out_dx")
    gw_out = matmul(sv["merged"].T, dmo, out_dtype=BF16, name="mm_out_dw")
    dps, dgl, gs["b_gate"] = merge_bwd(proj, p["b_gate"], sv["ps"], dmerged)
    ys = (sv["y_ret"], sv["y_sb"], sv["y_ssd"])
    dys = [matmul(dp, wb, nt=True, name="mm_branch_dx") for dp, wb in zip(dps, w["w_branch"])]
    gw_branch = [matmul(y.T, dp, out_dtype=BF16, name="mm_branch_dw") for y, dp in zip(ys, dps)]

    do_h, dg, gs["ret_gn_w"] = gn_gate_bwd(sv["o_ret"], p["ret_gn_w"], proj, dys[0])
    v_ret = (proj, COL_RET + 2 * BR_WIDTH, N_HEADS)
    ret = dict(chunk=RET_CHUNK, causal=False, heads=N_HEADS)
    dqr = linear_attention(do_h, v_ret, sv["kr"], ac_ret, ar_ret, reverse=False, name="ret_dq", **ret)
    dkr = linear_attention(v_ret, do_h, sv["qr"], ac_ret, ar_ret, reverse=True, name="ret_dk", **ret)
    dv = linear_attention(sv["kr"], sv["qr"], do_h, ac_ret, ar_ret, reverse=True, name="ret_dv",
                          out_dtype=BF16, out_2d=True, **ret)
    dq, dk = rotary_bwd(dqr, dkr, cos2, sin2)

    dsq, dsk, dsv = sb_backward(sv["qkv"], sv["o_sb"], dys[1])

    act = sv["act"]
    d_full = jnp.repeat(p["ssd_d"], SSD_P).reshape(1, BR_WIDTH)
    norm_w = p["ssd_norm_w"].reshape(1, BR_WIDTH)
    dy, dz, gs["ssd_norm_w"], dd_lanes = ssd_gate_bwd(sv["ytil"], act, proj, d_full, norm_w, dys[2])
    gs["ssd_d"] = dd_lanes.reshape(SSD_HEADS, SSD_P).sum(axis=1)
    dy_h = _to_heads(dy, SSD_HEADS, SSD_P)
    b_grp = (act, BR_WIDTH, SSD_GROUPS)
    c_grp = (act, BR_WIDTH + SSD_GROUPS * SSD_N, SSD_GROUPS)
    ssd = dict(chunk=SSD_CHUNK, causal=True, heads=SSD_HEADS)
    ac_h, ar_h, xdt_h = sv["ac_h"], sv["ar_h"], sv["xdt_h"]
    dxdt_h, *dx_parts, s_rev = linear_attention(b_grp, c_grp, dy_h, ac_h, ar_h, reverse=True, name="ssd_dx",
                                                parts_out=True, **ssd)
    dc_h = linear_attention(dy_h, xdt_h, b_grp, ac_h, ar_h, reverse=False, name="ssd_dc", **ssd)
    db_h = linear_attention(xdt_h, dy_h, c_grp, ac_h, ar_h, reverse=True, name="ssd_db", **ssd)
    d_h = p["ssd_d"].reshape(SSD_HEADS, 1, 1)
    dx_h, e_h, v_h, ddtx_h = ssd_head_bwd(dy_h, sv["y_parts"], dxdt_h, dx_parts, xdt_h, sv["x_h"], sv["dt_h"], d_h)
    lanes = lambda a: jnp.pad(a[:, :, 0].T, ((0, 0), (0, LANES - SSD_HEADS)))
    g_chunk = ssd_state_dot(sv["s_fwd"], s_rev, ac_h[:, :, SSD_CHUNK - 1:, :])
    da = ssd_decay_grad(lanes(e_h), lanes(v_h), lanes(g_chunk[:, :, :, 0])[:, None, :])
    dt_bias, a_log = _pad_lanes(p["ssd_dt_bias"]), _pad_lanes(p["ssd_a_log"])
    ddtr, dbias, dalog = ssd_dt_bwd(da, lanes(ddtx_h), proj, dt_bias, a_log)
    gs["ssd_dt_bias"], gs["ssd_a_log"] = dbias[:SSD_HEADS], dalog[:SSD_HEADS]
    db2, dc2 = ssd_group_sum(db_h, dc_h)
    dact = jnp.concatenate([_from_heads(dx_h), db2, dc2], axis=1)
    xbc = proj[:, COL_XBC:COL_XBC + XBC_WIDTH]
    us = _shifted(xbc, False) + [xbc]
    conv_b = p["ssd_conv_b"].reshape(1, XBC_WIDTH)
    dcv, gs["ssd_conv_w"], gs["ssd_conv_b"] = ssd_conv_bwd_a(us, p["ssd_conv_w"], conv_b, dact)
    dxbc = ssd_conv_bwd_b(_shifted(dcv, True) + [dcv], p["ssd_conv_w"])

    dt_cols = jnp.pad(ddtr, ((0, 0), (0, IN_PAD - COL_DT - LANES)))
    dproj = jnp.concatenate([dq, dk, dv, dg, dsq.astype(BF16), dsk.astype(BF16), dsv.astype(BF16),
                             dz, dxbc, dgl, dt_cols], axis=1)
    dh = matmul(dproj, w["w_in"], nt=True, name="mm_in_dx")
    gw_in = matmul(sv["h"].T, dproj, out_dtype=BF16, name="mm_in_dw")
    dx, gs["norm_mix_pre"] = rms_bwd(sv["x"], p["norm_mix_pre"], dh, "rms_mix_pre_bwd", resid=dx1)
    gw = dict(w_in=gw_in, w_branch=gw_branch, w_out=gw_out, w_gu=gw_gu, w_down=gw_down)
    return dx, gw, gs


def in_cols_pad(w):
    zeros = jnp.zeros(w.shape[:-1] + (IN_PAD - IN_COLS,), w.dtype)
    return jnp.concatenate([w[..., :COL_GATE], w[..., COL_GATE + SSD_HEADS:], w[..., COL_GATE:COL_GATE + SSD_HEADS], zeros],
                           axis=-1)


def in_cols_unpad(g):
    return jnp.concatenate([g[..., :COL_GATE], g[..., COL_DT:COL_DT + SSD_HEADS], g[..., COL_GATE:COL_DT]], axis=-1)


BIG = (("w_in", (D_MODEL, IN_COLS // N_SHARD), True),
       ("w_branch_ret", (BR_WIDTH, D_MODEL // N_SHARD), True),
       ("w_branch_sb", (BR_WIDTH, D_MODEL // N_SHARD), True),
       ("w_branch_ssd", (BR_WIDTH, D_MODEL // N_SHARD), True),
       ("w_out", (D_MODEL // N_SHARD, D_MODEL), False),
       ("ffn_w_gate", (D_MODEL, FFN_HIDDEN // N_SHARD), True),
       ("ffn_w_up", (D_MODEL, FFN_HIDDEN // N_SHARD), True),
       ("ffn_w_down", (FFN_HIDDEN // N_SHARD, D_MODEL), False))
FLAT_ROWS = sum(r * c for _, (r, c), _ in BIG) // LANES
assert FLAT_ROWS * LANES == sum(r * c for _, (r, c), _ in BIG) and FLAT_ROWS % 32 == 0
HALF_ROWS = FLAT_ROWS // 2


def pack_shard(layer_shards, dtype):
    flat = jnp.concatenate([layer_shards[n].reshape(-1).astype(dtype) for n, _, _ in BIG])
    return flat.reshape(FLAT_ROWS, LANES)


def unpack_shard(flat):
    out, off = {}, 0
    v = flat.reshape(-1)
    for n, (r, c), _ in BIG:
        out[n] = v[off:off + r * c].reshape(r, c)
        off += r * c
    return out


def unpack_gathered(full):
    v = full.reshape(N_SHARD, -1)
    out, off = {}, 0
    for n, (r, c), by_cols in BIG:
        seg = v[:, off:off + r * c].reshape(N_SHARD, r, c)
        out[n] = seg.transpose(1, 0, 2).reshape(r, N_SHARD * c) if by_cols else seg.reshape(N_SHARD * r, c)
        off += r * c
    return out


def pack_for_scatter(whole):
    segs = []
    for n, (r, c), by_cols in BIG:
        g = whole[n].astype(BF16)
        g = g.reshape(r, N_SHARD, c).transpose(1, 0, 2) if by_cols else g.reshape(N_SHARD, r, c)
        segs.append(g.reshape(N_SHARD, r * c))
    return jnp.concatenate(segs, axis=1).reshape(N_SHARD, FLAT_ROWS, LANES)


ANY = pl.BlockSpec(memory_space=pl.ANY)


def _place():
    return lax.axis_index("x"), lax.axis_index("y"), lax.axis_index("c")


def _other_chips(x, y):
    return [(1 - x, y), (x, 1 - y), (1 - x, 1 - y)]


def _half(c):
    return pl.ds(pl.multiple_of(c * HALF_ROWS, 16), HALF_ROWS)


def all_gather_shards(shard):
    def body(x_ref, o_ref, send_sems, recv_sems, local_sem):
        x, y, c = _place()
        j = 2 * x + y
        sibling = (x, y, 1 - c)
        mine, other = _half(c), _half(1 - c)

        def copy(k, src, dst, to):
            return pltpu.make_async_remote_copy(src_ref=src, dst_ref=dst, send_sem=send_sems.at[k],
                                                recv_sem=recv_sems.at[k], device_id=to, device_id_type=MESH)

        local = pltpu.make_async_copy(x_ref, o_ref.at[j], local_sem)
        local.start()
        chips = _other_chips(x, y)
        sent = [copy(k, x_ref.at[mine], o_ref.at[j, mine], (cx, cy, c)) for k, (cx, cy) in enumerate(chips)]
        for cp in sent:
            cp.start()
        passed = []
        for k, (cx, cy) in enumerate(chips):
            landed = o_ref.at[2 * cx + cy, mine]
            copy(k, landed, landed, (cx, cy, c)).wait_recv()
            cp = copy(3 + k, landed, landed, sibling)
            cp.start()
            passed.append(cp)
        for k, (cx, cy) in enumerate(chips):
            landed = o_ref.at[2 * cx + cy, other]
            copy(3 + k, landed, landed, sibling).wait_recv()
        for cp in sent + passed:
            cp.wait_send()
        local.wait()

    return pl.pallas_call(
        body, name="all_gather_shards", in_specs=[ANY], out_specs=ANY,
        out_shape=jax.ShapeDtypeStruct((N_SHARD, FLAT_ROWS, LANES), shard.dtype),
        scratch_shapes=[pltpu.SemaphoreType.DMA((6,)), pltpu.SemaphoreType.DMA((6,)), pltpu.SemaphoreType.DMA],
    )(shard)


def exchange_siblings(g):
    def body(g_ref, o_ref, send_sem, recv_sem):
        x, y, c = _place()
        cp = pltpu.make_async_remote_copy(src_ref=g_ref.at[:, _half(1 - c)], dst_ref=o_ref, send_sem=send_sem,
                                          recv_sem=recv_sem, device_id=(x, y, 1 - c), device_id_type=MESH)
        cp.start()
        cp.wait()

    return pl.pallas_call(
        body, name="exchange_siblings", in_specs=[ANY], out_specs=ANY,
        out_shape=jax.ShapeDtypeStruct((N_SHARD, HALF_ROWS, LANES), g.dtype),
        scratch_shapes=[pltpu.SemaphoreType.DMA, pltpu.SemaphoreType.DMA],
    )(g)


def exchange_chips(p):
    def body(p_ref, o_ref, send_sems, recv_sems, local_sem):
        x, y, c = _place()
        j = 2 * x + y
        local = pltpu.make_async_copy(p_ref.at[j], o_ref.at[j], local_sem)
        local.start()
        copies = []
        for k, (cx, cy) in enumerate(_other_chips(x, y)):
            cp = pltpu.make_async_remote_copy(src_ref=p_ref.at[2 * cx + cy], dst_ref=o_ref.at[j],
                                              send_sem=send_sems.at[k], recv_sem=recv_sems.at[k],
                                              device_id=(cx, cy, c), device_id_type=MESH)
            cp.start()
            copies.append(cp)
        for cp in copies:
            cp.wait()
        local.wait()

    return pl.pallas_call(
        body, name="exchange_chips", in_specs=[ANY], out_specs=ANY,
        out_shape=jax.ShapeDtypeStruct((N_SHARD, HALF_ROWS, LANES), p.dtype),
        scratch_shapes=[pltpu.SemaphoreType.DMA((3,)), pltpu.SemaphoreType.DMA((3,)), pltpu.SemaphoreType.DMA],
    )(p)


def share_halves(red):
    def body(r_ref, o_ref, send_sem, recv_sem, local_sem):
        x, y, c = _place()
        local = pltpu.make_async_copy(r_ref, o_ref.at[_half(c)], local_sem)
        local.start()
        cp = pltpu.make_async_remote_copy(src_ref=r_ref, dst_ref=o_ref.at[_half(c)], send_sem=send_sem,
                                          recv_sem=recv_sem, device_id=(x, y, 1 - c), device_id_type=MESH)
        cp.start()
        cp.wait()
        local.wait()

    return pl.pallas_call(
        body, name="share_halves", in_specs=[ANY], out_specs=ANY,
        out_shape=jax.ShapeDtypeStruct((FLAT_ROWS, LANES), red.dtype),
        scratch_shapes=[pltpu.SemaphoreType.DMA, pltpu.SemaphoreType.DMA, pltpu.SemaphoreType.DMA],
    )(red)


SUM_ROWS = 1568


def reduce_scatter(g):
    c = lax.axis_index("c")
    got = exchange_siblings(g)
    own = lax.dynamic_slice_in_dim(g, c * HALF_ROWS, HALF_ROWS, axis=1)
    blk = lambda a: (a, (None, SUM_ROWS, LANES), lambda j, i: (j, i, 0))
    pair = bmap("sum_siblings", lambda a, b: (a.astype(F32) + b.astype(F32),), (N_SHARD, HALF_ROWS // SUM_ROWS),
                [blk(own), blk(got)],
                [((N_SHARD, HALF_ROWS, LANES), BF16, (None, SUM_ROWS, LANES), lambda j, i: (j, i, 0))])[0]
    parts = exchange_chips(pair)
    part = lambda j: (parts, (None, SUM_ROWS, LANES), lambda i: (j, i, 0))
    red = bmap("sum_chips", lambda a, b, c_, d: (((a.astype(F32) + b.astype(F32)) + c_.astype(F32)) + d.astype(F32),),
               (HALF_ROWS // SUM_ROWS,), [part(j) for j in range(N_SHARD)],
               [((HALF_ROWS, LANES), F32, (SUM_ROWS, LANES), lambda i: (i, 0))])[0]
    return share_halves(red)


def all_reduce_small(buf):
    r = buf.shape[0]

    def body(x_ref, o_ref, land_ref, send_sems, recv_sems):
        x, y, c = _place()
        me = 4 * x + 2 * y + c
        land_ref[me] = x_ref[...]
        copies = []
        for d in range(1, 8):
            peer = tuple(1 - v if (d >> s) & 1 else v for v, s in ((x, 2), (y, 1), (c, 0)))
            cp = pltpu.make_async_remote_copy(src_ref=x_ref, dst_ref=land_ref.at[me], send_sem=send_sems.at[d - 1],
                                              recv_sem=recv_sems.at[d - 1], device_id=peer, device_id_type=MESH)
            cp.start()
            copies.append(cp)
        for cp in copies:
            cp.wait()
        acc = land_ref[0]
        for s in range(1, 8):
            acc = acc + land_ref[s]
        o_ref[...] = acc

    vmem = pl.BlockSpec(memory_space=pltpu.VMEM)
    return pl.pallas_call(
        body, name="all_reduce_small", in_specs=[vmem], out_specs=vmem,
        out_shape=jax.ShapeDtypeStruct((r, LANES), F32),
        scratch_shapes=[pltpu.VMEM((8, r, LANES), F32), pltpu.SemaphoreType.DMA((7,)), pltpu.SemaphoreType.DMA((7,))],
    )(buf)


def adamw(w, g, m, v, name):
    r, c = w.shape
    tm = r
    for cand in (512, 256, 128, 64, 32, 16, 8):
        if r % cand == 0 and cand * c * 4 * 14 <= 24 * 1024 * 1024:
            tm = cand
            break
    bc1 = 1.0 - ADAM_B1 ** ADAM_STEP
    bc2 = 1.0 - ADAM_B2 ** ADAM_STEP

    def fn(wv, gv, mv, vv):
        m2 = ADAM_B1 * mv + (1.0 - ADAM_B1) * gv
        v2 = ADAM_B2 * vv + (1.0 - ADAM_B2) * (gv * gv)
        delta = -ADAM_LR * ((m2 / bc1) / (jnp.sqrt(v2 / bc2) + ADAM_EPS) + ADAM_WD * wv)
        return delta, m2, v2

    return bmap(name, fn, (r // tm,), [rows(a, tm) for a in (w, g, m, v)], [row_out(r, c, F32, tm)] * 3)


WEIGHTS = ("norm_mix_pre", "norm_mix_post", "norm_ffn_pre", "norm_ffn_post", "w_in", "b_gate", "ret_gn_w", "ssd_conv_w",
           "ssd_conv_b", "ssd_dt_bias", "ssd_a_log", "ssd_d", "ssd_norm_w", "w_branch_ret", "w_branch_sb", "w_branch_ssd",
           "w_out", "ffn_w_gate", "ffn_w_up", "ffn_w_down")
BIG_NAMES = tuple(n for n, _, _ in BIG)
SMALL = tuple(n for n in WEIGHTS if n not in BIG_NAMES and n != "ssd_conv_w")
CONV_SHARD = XBC_WIDTH // N_SHARD


def _pack_small(parts):
    flat = jnp.concatenate([p.reshape(-1).astype(F32) for p in parts])
    rows_ = -(-flat.shape[0] // (8 * LANES)) * 8
    return jnp.pad(flat, (0, rows_ * LANES - flat.shape[0])).reshape(rows_, LANES)


def _unpack_small(buf, shapes):
    v, out, off = buf.reshape(-1), [], 0
    for shp in shapes:
        n = int(np.prod(shp))
        out.append(v[off:off + n].reshape(shp))
        off += n
    return out


def kernel(x, positions, norm_mix_pre, norm_mix_post, norm_ffn_pre, norm_ffn_post, w_in, b_gate, ret_gn_w, ssd_conv_w, ssd_conv_b, ssd_dt_bias, ssd_a_log, ssd_d, ssd_norm_w, w_branch_ret, w_branch_sb, w_branch_ssd, w_out, ffn_w_gate, ffn_w_up, ffn_w_down, loss_target, m_norm_mix_pre, m_norm_mix_post, m_norm_ffn_pre, m_norm_ffn_post, m_w_in, m_b_gate, m_ret_gn_w, m_ssd_conv_w, m_ssd_conv_b, m_ssd_dt_bias, m_ssd_a_log, m_ssd_d, m_ssd_norm_w, m_w_branch_ret, m_w_branch_sb, m_w_branch_ssd, m_w_out, m_ffn_w_gate, m_ffn_w_up, m_ffn_w_down, v_norm_mix_pre, v_norm_mix_post, v_norm_ffn_pre, v_norm_ffn_post, v_w_in, v_b_gate, v_ret_gn_w, v_ssd_conv_w, v_ssd_conv_b, v_ssd_dt_bias, v_ssd_a_log, v_ssd_d, v_ssd_norm_w, v_w_branch_ret, v_w_branch_sb, v_w_branch_ssd, v_w_out, v_ffn_w_gate, v_ffn_w_up, v_ffn_w_down):
    given = dict(locals())
    wts = {n: given[n] for n in WEIGHTS}
    mom = {n: given["m_" + n] for n in WEIGHTS}
    var = {n: given["v_" + n] for n in WEIGHTS}
    xi, yi, ci = lax.axis_index("x"), lax.axis_index("y"), lax.axis_index("c")
    shard_id = 2 * xi + yi

    conv_mine = jnp.where(ci == 0, ssd_conv_w, 0.0)
    conv_slots = lax.dynamic_update_slice_in_dim(jnp.zeros((DEPTH, SSD_CONV, XBC_WIDTH), F32), conv_mine,
                                                 shard_id * CONV_SHARD, axis=2)
    conv_buf = _pack_small([conv_slots])
    conv_whole = _unpack_small(all_reduce_small(conv_buf), [(DEPTH, SSD_CONV, XBC_WIDTH)])[0]

    tabs = rope_tables(positions[0]) + ret_decay_tables()
    layer_w, layer_p = [], []
    for l in range(DEPTH):
        full = unpack_gathered(all_gather_shards(pack_shard({n: wts[n][l] for n in BIG_NAMES}, BF16)))
        layer_w.append(dict(
            w_in=in_cols_pad(full["w_in"]),
            w_branch=[full["w_branch_ret"], full["w_branch_sb"], full["w_branch_ssd"]],
            w_out=full["w_out"],
            w_gu=jnp.concatenate([full["ffn_w_gate"], full["ffn_w_up"]], axis=1),
            w_down=full["ffn_w_down"]))
        p = {n: wts[n][l] for n in SMALL}
        p["ssd_conv_w"] = conv_whole[l]
        layer_p.append(p)

    act = x[0]
    saved = []
    for l in range(DEPTH):
        act, sv = layer_forward(act, layer_w[l], layer_p[l], tabs)
        saved.append(sv)
    dact, loss_part = loss_head(act, loss_target[0])

    big_grads = [None] * DEPTH
    small_grads = [None] * DEPTH
    for l in reversed(range(DEPTH)):
        dact, gw, small_grads[l] = layer_backward(dact, saved[l], layer_w[l], layer_p[l], tabs)
        gu = gw["w_gu"]
        whole = dict(w_in=in_cols_unpad(gw["w_in"]), w_branch_ret=gw["w_branch"][0], w_branch_sb=gw["w_branch"][1],
                     w_branch_ssd=gw["w_branch"][2], w_out=gw["w_out"], ffn_w_gate=gu[:, :FFN_HIDDEN],
                     ffn_w_up=gu[:, FFN_HIDDEN:], ffn_w_down=gw["w_down"])
        big_grads[l] = unpack_shard(reduce_scatter(pack_for_scatter(whole)))

    small_names = SMALL + ("ssd_conv_w",)
    small_parts = [jnp.stack([small_grads[l][n] for l in range(DEPTH)]) for n in small_names]
    small_shapes = [p.shape for p in small_parts]
    summed = all_reduce_small(_pack_small([loss_part.reshape(1)] + small_parts))
    loss, *small_sum = _unpack_small(summed, [(1,)] + small_shapes)
    grads = dict(zip(small_names, small_sum))
    grads["ssd_conv_w"] = lax.dynamic_slice_in_dim(grads["ssd_conv_w"], shard_id * CONV_SHARD, CONV_SHARD, axis=2)
    for n in BIG_NAMES:
        grads[n] = jnp.stack([big_grads[l][n] for l in range(DEPTH)])

    delta, new_m, new_v = {}, {}, {}
    for n in WEIGHTS:
        shape = wts[n].shape
        two_d = (-1, shape[-1]) if len(shape) == 3 else (8, -1) if shape[-1] * shape[0] % (8 * LANES) == 0 else shape
        args = [a.reshape(two_d) for a in (wts[n], grads[n], mom[n], var[n])]
        d, m2, v2 = adamw(*args, name="adamw_" + n)
        delta[n], new_m[n], new_v[n] = d.reshape(shape), m2.reshape(shape), v2.reshape(shape)

    return (loss.reshape(()), dact[None], *[grads[n] for n in WEIGHTS], *[delta[n] for n in WEIGHTS],
            *[new_m[n] for n in WEIGHTS], *[new_v[n] for n in WEIGHTS])
```

```python
import functools
import math

import numpy as np
import jax
import jax.numpy as jnp
from jax import lax
from jax.experimental import pallas as pl
from jax.experimental.pallas import tpu as pltpu

F32 = jnp.float32
BF16 = jnp.bfloat16
MESH = pl.DeviceIdType.MESH

D_MODEL = 2048
DEPTH = 4
RET_CHUNK = 64
NORM_EPS = 1e-6
HEAD = 128
N_HEADS = 8
BR_WIDTH = 1024
SSD_P = 64
SSD_HEADS = 16
SSD_GROUPS = 4
SSD_N = 128
SSD_CHUNK = 128
SSD_CONV = 4
FFN_HIDDEN = 5632
IN_COLS = 16400
IN_PAD = 16896
COL_RET, COL_SB, COL_Z, COL_XBC, COL_GATE, COL_DT = 0, 4096, 7168, 8192, 10240, 16384
N_SHARD = 4
LANES = 128
VMEM_LIMIT = 48 * 1024 * 1024

ADAM_LR, ADAM_B1, ADAM_B2, ADAM_EPS, ADAM_WD, ADAM_STEP = 0.001, 0.9, 0.999, 1e-08, 0.01, 10


def _pick(n, prefs):
    for p in prefs:
        if n % p == 0:
            return p
    return n


def _params(sem=None):
    kw = dict(vmem_limit_bytes=VMEM_LIMIT)
    if sem is not None:
        kw["dimension_semantics"] = sem
    return pltpu.CompilerParams(**kw)


def matmul(a, b, *, nt=False, out_dtype=F32, name="matmul"):
    m, k = a.shape
    n = b.shape[0] if nt else b.shape[1]
    tm = _pick(m, (1024, 512, 256, 128))
    tn = _pick(n, (512, 256, 128))
    tk = _pick(k, (1024, 512, 256, 128))
    nk = k // tk

    def body(a_ref, b_ref, o_ref, acc_ref):
        kk = pl.program_id(2)
        if nt:
            part = lax.dot_general(a_ref[...], b_ref[...], (((1,), (1,)), ((), ())), preferred_element_type=F32)
        else:
            part = jnp.dot(a_ref[...], b_ref[...], preferred_element_type=F32)

        @pl.when(kk == 0)
        def _():
            acc_ref[...] = part

        @pl.when(kk > 0)
        def _():
            acc_ref[...] += part

        @pl.when(kk == nk - 1)
        def _():
            o_ref[...] = acc_ref[...].astype(o_ref.dtype)

    b_spec = pl.BlockSpec((tn, tk), lambda i, j, kk: (j, kk)) if nt else pl.BlockSpec((tk, tn), lambda i, j, kk: (kk, j))
    return pl.pallas_call(
        body, name=name, grid=(m // tm, n // tn, nk),
        in_specs=[pl.BlockSpec((tm, tk), lambda i, j, kk: (i, kk)), b_spec],
        out_specs=pl.BlockSpec((tm, tn), lambda i, j, kk: (i, j)),
        out_shape=jax.ShapeDtypeStruct((m, n), out_dtype),
        scratch_shapes=[pltpu.VMEM((tm, tn), F32)],
        compiler_params=_params(("parallel", "parallel", "arbitrary")),
    )(a, b)


def bmap(name, fn, grid, ins, outs, accs=()):
    n_in, n_out = len(ins), len(outs)
    last = len(grid) - 1

    def store_acc(ref, val, first):
        @pl.when(first)
        def _():
            ref[...] = val

        @pl.when(jnp.logical_not(first))
        def _():
            ref[...] += val

    def body(*refs):
        vals = fn(*[r[...] for r in refs[:n_in]])
        if not isinstance(vals, (tuple, list)):
            vals = (vals,)
        for r, v in zip(refs[n_in:n_in + n_out], vals[:n_out]):
            r[...] = v.astype(r.dtype)
        first = pl.program_id(last) == 0
        for r, v in zip(refs[n_in + n_out:], vals[n_out:]):
            store_acc(r, v.astype(F32), first)

    sem = ("parallel",) * last + (("arbitrary",) if accs else ("parallel",))
    res = pl.pallas_call(
        body, name=name, grid=grid,
        in_specs=[pl.BlockSpec(bs, im) for (_, bs, im) in ins],
        out_specs=[pl.BlockSpec(bs, im) for (_, _, bs, im) in outs] + [pl.BlockSpec(bs, im) for (_, bs, im) in accs],
        out_shape=[jax.ShapeDtypeStruct(s, dt) for (s, dt, _, _) in outs] + [jax.ShapeDtypeStruct(s, F32) for (s, _, _) in accs],
        compiler_params=_params(sem),
    )(*[a for (a, _, _) in ins])
    return res


def rows(a, tm, width=None, col=0):
    width = a.shape[1] if width is None else width
    assert col % width == 0
    cb = col // width
    return (a, (tm, width), lambda i: (i, cb))


def par(a):
    return (a, a.shape, lambda *_: (0,) * a.ndim)


def row_out(s, width, dtype, tm):
    return ((s, width), dtype, (tm, width), lambda i: (i, 0))


def col_acc(width):
    return ((1, width), (1, width), lambda i: (0, 0))


def _sigmoid(x):
    return 1.0 / (1.0 + jnp.exp(-x))


def _silu(x):
    return x * _sigmoid(x)


def _dsilu(x):
    s = _sigmoid(x)
    return s * (1.0 + x * (1.0 - s))


def _softplus(x):
    return jnp.maximum(x, 0.0) + jnp.log(1.0 + jnp.exp(-jnp.abs(x)))


def _split_bf16(x):
    hi = x.astype(BF16)
    lo = (x - hi.astype(F32)).astype(BF16)
    return hi, lo


def _dot_exact_mask(x, mask_bf16):
    hi, lo = _split_bf16(x)
    return (jnp.dot(hi, mask_bf16, preferred_element_type=F32)
            + jnp.dot(lo, mask_bf16, preferred_element_type=F32))


def rms_fwd(x, w, out_dtype, name, resid=None):
    s, d = x.shape
    tm = _pick(s, (256, 128))

    def fn(xv, wv, *rest):
        r = lax.rsqrt(jnp.mean(xv * xv, axis=-1, keepdims=True) + NORM_EPS)
        y = xv * r * wv
        if rest:
            y = y + rest[0]
        return (y,)

    ins = [rows(x, tm), par(w.reshape(1, d))]
    if resid is not None:
        ins.append(rows(resid, tm))
    return bmap(name, fn, (s // tm,), ins, [row_out(s, d, out_dtype, tm)])[0]


def rms_bwd(x, w, dy, name, resid=None, out_dtype=F32):
    s, d = x.shape
    tm = _pick(s, (256, 128))

    def fn(xv, wv, dyv, *rest):
        dyv = dyv.astype(F32)
        r = lax.rsqrt(jnp.mean(xv * xv, axis=-1, keepdims=True) + NORM_EPS)
        xn = xv * r
        g = dyv * wv
        dx = r * (g - xn * jnp.mean(g * xn, axis=-1, keepdims=True))
        if rest:
            dx = dx + rest[0]
        return dx, jnp.sum(dyv * xn, axis=0, keepdims=True)

    ins = [rows(x, tm), par(w.reshape(1, d)), rows(dy, tm)]
    if resid is not None:
        ins.append(rows(resid, tm))
    dx, dw = bmap(name, fn, (s // tm,), ins, [row_out(s, d, out_dtype, tm)], [col_acc(d)])
    return dx, dw[0]


def linear_attention(q, k, v, ac, ar, *, chunk, causal, reverse, heads, name, out_dtype=F32, out_2d=False,
                     parts_out=False):
    def dims(op):
        return (op[0].shape[0], HEAD) if isinstance(op, tuple) else (op.shape[1], op.shape[2])

    s_len, dk = dims(q)
    dv = dims(v)[1]
    L = chunk
    cpb = _pick(s_len // L, (8, 4, 2, 1))
    nb = s_len // (L * cpb)
    a_chunks = ac.shape[1]

    def blk(i):
        return nb - 1 - i if reverse else i

    def seq_spec(op):
        if isinstance(op, tuple):
            _, first_col, nh = op
            rep, cb = heads // nh, first_col // HEAD
            return pl.BlockSpec((L * cpb, HEAD), lambda h, i: (blk(i), cb + h // rep))
        rep = heads // op.shape[0]
        return pl.BlockSpec((None, L * cpb, op.shape[2]), lambda h, i: (h // rep, blk(i), 0))

    def a_spec(arr, shape):
        rep = heads // arr.shape[0]
        if a_chunks == 1:
            return pl.BlockSpec((None, 1) + shape, lambda h, i: (h // rep, 0, 0, 0))
        return pl.BlockSpec((None, cpb) + shape, lambda h, i: (h // rep, blk(i), 0, 0))

    def body(q_ref, k_ref, v_ref, ac_ref, ar_ref, *rest):
        o_refs, state_ref = rest[:-1], rest[-1]

        @pl.when(pl.program_id(1) == 0)
        def _():
            state_ref[...] = jnp.zeros_like(state_ref)

        row = lax.broadcasted_iota(jnp.int32, (L, L), 0)
        col = lax.broadcasted_iota(jnp.int32, (L, L), 1)

        def one(step, carry):
            c = cpb - 1 - step if reverse else step
            ai = 0 if a_chunks == 1 else c
            sl = pl.ds(pl.multiple_of(c * L, L), L)
            qc = q_ref[sl, :].astype(F32)
            kc = k_ref[sl, :].astype(F32)
            vf = v_ref[sl, :].astype(F32)
            vc = vf.astype(BF16)
            a_col = ac_ref[ai]
            a_row = ar_ref[ai]
            a_last = ac_ref[ai, pl.ds(L - 1, 1), :]
            seg = a_col - a_row
            if not causal:
                dec = jnp.exp(-jnp.abs(seg))
            elif reverse:
                dec = jnp.where(col > row, jnp.exp(jnp.minimum(-seg, 0.0)), 0.0)
            else:
                dec = jnp.where(row > col, jnp.exp(jnp.minimum(seg, 0.0)), 0.0)
            if reverse:
                qd, kd = jnp.exp(a_last - a_col), jnp.exp(a_col)
            else:
                qd, kd = jnp.exp(a_col), jnp.exp(a_last - a_col)
            sc = lax.dot_general(qc.astype(BF16), kc.astype(BF16), (((1,), (1,)), ((), ())),
                                 preferred_element_type=F32) * dec
            state = state_ref[...]
            intra = jnp.dot(sc.astype(BF16), vc, preferred_element_type=F32)
            cross = jnp.dot((qc * qd).astype(BF16), state.astype(BF16), preferred_element_type=F32)
            out = intra + cross
            if parts_out:
                o_refs[1][sl, :] = intra
                o_refs[2][sl, :] = cross
                o_refs[3][c] = state
            if causal:
                out = out + jnp.sum(qc * kc, axis=-1, keepdims=True) * vf
            o_refs[0][sl, :] = out.astype(o_refs[0].dtype)
            upd = lax.dot_general((kc * kd).astype(BF16), vc, (((0,), (0,)), ((), ())),
                                  preferred_element_type=F32)
            state_ref[...] = state * jnp.exp(a_last) + upd
            return carry

        lax.fori_loop(0, cpb, one, 0)

    if out_2d:
        out_spec = pl.BlockSpec((L * cpb, dv), lambda h, i: (blk(i), h))
        out_shape = jax.ShapeDtypeStruct((s_len, heads * dv), out_dtype)
    else:
        out_spec = pl.BlockSpec((None, L * cpb, dv), lambda h, i: (h, blk(i), 0))
        out_shape = jax.ShapeDtypeStruct((heads, s_len, dv), out_dtype)
    out_specs, out_shapes = [out_spec], [out_shape]
    if parts_out:
        assert not out_2d and out_dtype == F32
        out_specs += [out_spec, out_spec, pl.BlockSpec((None, cpb, dk, dv), lambda h, i: (h, blk(i), 0, 0))]
        out_shapes += [out_shape, out_shape, jax.ShapeDtypeStruct((heads, s_len // L, dk, dv), F32)]
    arrs = [op[0] if isinstance(op, tuple) else op for op in (q, k, v)]
    res = pl.pallas_call(
        body, name=name, grid=(heads, nb),
        in_specs=[seq_spec(q), seq_spec(k), seq_spec(v), a_spec(ac, (L, 1)), a_spec(ar, (1, L))],
        out_specs=out_specs, out_shape=out_shapes,
        scratch_shapes=[pltpu.VMEM((dk, dv), F32)],
        compiler_params=_params(("parallel", "arbitrary")),
    )(*arrs, ac, ar)
    return res if parts_out else res[0]


SB_KEYS = 128


def _sb_tile(q, kb, scale, valid, tri):
    z = lax.dot_general(q, kb, (((1,), (1,)), ((), ())), preferred_element_type=F32) * scale
    sp = _softplus(z)
    lk = -sp if valid is None else jnp.where(valid, -sp, 0.0)
    later_in = _dot_exact_mask(lk, tri)
    return z, lk, (z - sp) + later_in


def _sb_loops(i, ratio, step, init):
    carry = lax.fori_loop(0, ratio, lambda n, c: step((i + 1) * ratio - 1 - n, True, c), init)
    return lax.fori_loop(0, i * ratio, lambda n, c: step(i * ratio - 1 - n, False, c), carry)


def _sb_masks(i, bq):
    bk = SB_KEYS
    row = lax.broadcasted_iota(jnp.int32, (bk, bk), 0)
    col = lax.broadcasted_iota(jnp.int32, (bk, bk), 1)
    qpos = i * bq + lax.broadcasted_iota(jnp.int32, (bq, bk), 0)
    kcol = lax.broadcasted_iota(jnp.int32, (bq, bk), 1)
    return row, col, lambda kbi: (kbi * bk + kcol) < qpos


def sb_forward(qkv, name="sb_fwd"):
    s_len = qkv.shape[0]
    bk = SB_KEYS
    bq = _pick(s_len, (512, 256, 128))
    scale = HEAD ** -0.5

    def body(q_ref, k_ref, v_ref, o_ref):
        i = pl.program_id(1)
        q = q_ref[...]
        row, col, valid_of = _sb_masks(i, bq)
        tri_gt = (row > col).astype(BF16)

        def step(kbi, masked, carry):
            acc, a_run = carry
            sl = pl.ds(pl.multiple_of(kbi * bk, bk), bk)
            valid = valid_of(kbi) if masked else None
            _, lk, expo = _sb_tile(q, k_ref[sl, :], scale, valid, tri_gt)
            w = jnp.exp(expo + a_run)
            if masked:
                w = jnp.where(valid, w, 0.0)
            acc = acc + jnp.dot(w.astype(BF16), v_ref[sl, :], preferred_element_type=F32)
            return acc, a_run + jnp.sum(lk, axis=-1, keepdims=True)

        acc, _ = _sb_loops(i, bq // bk, step, (jnp.zeros((bq, HEAD), F32), jnp.zeros((bq, 1), F32)))
        o_ref[...] = acc

    return pl.pallas_call(
        body, name=name, grid=(N_HEADS, s_len // bq),
        in_specs=[pl.BlockSpec((bq, HEAD), lambda h, i: (i, h)),
                  pl.BlockSpec((s_len, HEAD), lambda h, i: (0, N_HEADS + h)),
                  pl.BlockSpec((s_len, HEAD), lambda h, i: (0, 2 * N_HEADS + h))],
        out_specs=pl.BlockSpec((bq, HEAD), lambda h, i: (i, h)),
        out_shape=jax.ShapeDtypeStruct((s_len, BR_WIDTH), F32),
        compiler_params=_params(("parallel", "arbitrary")),
    )(qkv, qkv, qkv)


def sb_backward(qkv, o, do, name="sb_bwd"):
    s_len = qkv.shape[0]
    bk = SB_KEYS
    bq = _pick(s_len, (512, 256, 128))
    scale = HEAD ** -0.5

    def body(q_ref, k_ref, v_ref, o_ref, do_ref, dq_ref, dk_ref, dv_ref):
        i = pl.program_id(1)

        @pl.when(i == 0)
        def _():
            dk_ref[...] = jnp.zeros_like(dk_ref)
            dv_ref[...] = jnp.zeros_like(dv_ref)

        q = q_ref[...]
        dob = do_ref[...].astype(BF16)
        delta = jnp.sum(dob.astype(F32) * o_ref[...], axis=-1, keepdims=True)
        row, col, valid_of = _sb_masks(i, bq)
        tri_gt = (row > col).astype(BF16)
        tri_ge = (row >= col).astype(BF16)

        def step(kbi, masked, carry):
            dq, a_run, e_run = carry
            sl = pl.ds(pl.multiple_of(kbi * bk, bk), bk)
            valid = valid_of(kbi) if masked else None
            kb = k_ref[sl, :]
            vb = v_ref[sl, :]
            z, lk, expo = _sb_tile(q, kb, scale, valid, tri_gt)
            w = jnp.exp(expo + a_run)
            if masked:
                w = jnp.where(valid, w, 0.0)
            wb = w.astype(BF16)
            beta = _sigmoid(z)
            dw = lax.dot_general(dob, vb, (((1,), (1,)), ((), ())), preferred_element_type=F32)
            e = dw * wb.astype(F32)
            suffix = e_run + _dot_exact_mask(e, tri_ge)
            dz = (e * (1.0 - beta) - beta * (delta - suffix)) * scale
            if masked:
                dz = jnp.where(valid, dz, 0.0)
            dzb = dz.astype(BF16)
            dq = dq + jnp.dot(dzb, kb, preferred_element_type=F32)
            dk_ref[sl, :] += lax.dot_general(dzb, q, (((0,), (0,)), ((), ())), preferred_element_type=F32)
            dv_ref[sl, :] += lax.dot_general(wb, dob, (((0,), (0,)), ((), ())),
                                             preferred_element_type=F32)
            return (dq, a_run + jnp.sum(lk, axis=-1, keepdims=True),
                    e_run + jnp.sum(e, axis=-1, keepdims=True))

        zero = jnp.zeros((bq, 1), F32)
        dq, _, _ = _sb_loops(i, bq // bk, step, (jnp.zeros((bq, HEAD), F32), zero, zero))
        dq_ref[...] = dq

    blk = pl.BlockSpec((bq, HEAD), lambda h, i: (i, h))
    full = pl.BlockSpec((s_len, HEAD), lambda h, i: (0, h))
    shape = jax.ShapeDtypeStruct((s_len, BR_WIDTH), F32)
    return pl.pallas_call(
        body, name=name, grid=(N_HEADS, s_len // bq),
        in_specs=[blk,
                  pl.BlockSpec((s_len, HEAD), lambda h, i: (0, N_HEADS + h)),
                  pl.BlockSpec((s_len, HEAD), lambda h, i: (0, 2 * N_HEADS + h)),
                  blk, blk],
        out_specs=[blk, full, full],
        out_shape=[shape, shape, shape],
        compiler_params=_params(("parallel", "arbitrary")),
    )(qkv, qkv, qkv, o, do)


ROPE_BASE = 10000.0


def rope_tables(positions):
    half = HEAD // 2
    inv_freq = ROPE_BASE ** (-2.0 * jnp.arange(half, dtype=F32) / HEAD)
    ang = positions.astype(F32)[:, None] * inv_freq
    cos, sin = jnp.cos(ang), jnp.sin(ang)
    return jnp.concatenate([cos, cos], 1), jnp.concatenate([-sin, sin], 1)


def ret_decay_tables():
    lg = np.log1p(-np.exp2(-5.0 - np.arange(N_HEADS, dtype=np.float64)))
    ac = (lg[:, None] * (np.arange(RET_CHUNK) + 1.0)[None, :]).astype(np.float32)
    return jnp.asarray(ac[:, None, :, None]), jnp.asarray(ac[:, None, None, :])


def _head_major(s, tm, d=HEAD, heads=N_HEADS, dtype=F32):
    return ((heads, s, d), dtype, (None, tm, d), lambda h, i: (h, i, 0))


def _head_cols(arr, tm, first_col):
    cb = first_col // HEAD
    return (arr, (tm, HEAD), lambda h, i: (i, cb + h))


def _head_rows(arr, tm):
    return (arr, (None, tm, arr.shape[2]), lambda h, i: (h, i, 0))


def rotary_fwd(proj, cos2, sin2):
    s = proj.shape[0]
    tm = _pick(s, (512, 256, 128))

    def fn(qv, kv, c, sn):
        def rot(t):
            return t * c + pltpu.roll(t, HEAD // 2, 1) * sn
        return rot(qv), rot(kv) * (HEAD ** -0.5)

    tab = lambda t: (t, (tm, HEAD), lambda h, i: (i, 0))
    return bmap("rotary_fwd", fn, (N_HEADS, s // tm),
                [_head_cols(proj, tm, COL_RET), _head_cols(proj, tm, COL_RET + BR_WIDTH), tab(cos2), tab(sin2)],
                [_head_major(s, tm), _head_major(s, tm)])


def rotary_bwd(dqr, dkr, cos2, sin2):
    s = dqr.shape[1]
    tm = _pick(s, (512, 256, 128))

    def fn(dq, dk, c, sn):
        def unrot(t):
            return t * c - pltpu.roll(t, HEAD // 2, 1) * sn
        return unrot(dq), unrot(dk) * (HEAD ** -0.5)

    tab = lambda t: (t, (tm, HEAD), lambda h, i: (i, 0))
    out = ((s, BR_WIDTH), BF16, (tm, HEAD), lambda h, i: (i, h))
    return bmap("rotary_bwd", fn, (N_HEADS, s // tm),
                [_head_rows(dqr, tm), _head_rows(dkr, tm), tab(cos2), tab(sin2)], [out, out])


def _group_norm(o):
    mu = jnp.mean(o, axis=-1, keepdims=True)
    oc = o - mu
    r = lax.rsqrt(jnp.mean(oc * oc, axis=-1, keepdims=True) + NORM_EPS)
    return oc * r, r


def gn_gate_fwd(o, gn_w, proj):
    s = o.shape[1]
    tm = _pick(s, (512, 256, 128))

    def fn(ov, wv, gv):
        on, _ = _group_norm(ov)
        return (on * wv * _silu(gv),)

    w3 = gn_w.reshape(N_HEADS, 1, HEAD)
    return bmap("gn_gate_fwd", fn, (N_HEADS, s // tm),
                [_head_rows(o, tm), (w3, (None, 1, HEAD), lambda h, i: (h, 0, 0)),
                 _head_cols(proj, tm, COL_RET + 3 * BR_WIDTH)],
                [((s, BR_WIDTH), BF16, (tm, HEAD), lambda h, i: (i, h))])[0]


def gn_gate_bwd(o, gn_w, proj, dy):
    s = o.shape[1]
    tm = _pick(s, (512, 256, 128))

    def fn(ov, wv, gv, dyv):
        on, r = _group_norm(ov)
        yn = on * wv
        dyn = dyv * _silu(gv)
        dg = dyv * yn * _dsilu(gv)
        don = dyn * wv
        do = r * (don - jnp.mean(don, axis=-1, keepdims=True) - on * jnp.mean(don * on, axis=-1, keepdims=True))
        return do, dg, jnp.sum(dyn * on, axis=0, keepdims=True)

    w3 = gn_w.reshape(N_HEADS, 1, HEAD)
    do, dg, dw = bmap("gn_gate_bwd", fn, (N_HEADS, s // tm),
                      [_head_rows(o, tm), (w3, (None, 1, HEAD), lambda h, i: (h, 0, 0)),
                       _head_cols(proj, tm, COL_RET + 3 * BR_WIDTH), _head_cols(dy, tm, 0)],
                      [_head_major(s, tm), ((s, BR_WIDTH), BF16, (tm, HEAD), lambda h, i: (i, h))],
                      [((N_HEADS, 1, HEAD), (None, 1, HEAD), lambda h, i: (h, 0, 0))])
    return do, dg, dw.reshape(BR_WIDTH)


XBC_WIDTH = 2048


def _shifted(u, back):
    s = u.shape[0]
    n = SSD_CONV - 1
    if back:
        return [jnp.pad(u, ((0, n - k), (0, 0)))[n - k:] for k in range(n)]
    return [jnp.pad(u, ((n - k, 0), (0, 0)))[:s] for k in range(n)]


def _conv(us, w, b):
    return us[0] * w[0:1] + us[1] * w[1:2] + us[2] * w[2:3] + us[3] * w[3:4] + b


def ssd_conv_fwd(us, conv_w, conv_b):
    s = us[0].shape[0]
    tm = _pick(s, (256, 128))

    def fn(u0, u1, u2, u3, w, b):
        return (_silu(_conv((u0, u1, u2, u3), w, b)),)

    return bmap("ssd_conv_fwd", fn, (s // tm,), [rows(u, tm) for u in us] + [par(conv_w), par(conv_b)],
                [row_out(s, XBC_WIDTH, F32, tm)])[0]


def ssd_conv_bwd_a(us, conv_w, conv_b, dact):
    s = dact.shape[0]
    tm = _pick(s, (256, 128))

    def fn(u0, u1, u2, u3, w, b, da):
        dcv = da * _dsilu(_conv((u0, u1, u2, u3), w, b))
        sums = [jnp.sum(dcv * u, axis=0, keepdims=True) for u in (u0, u1, u2, u3)]
        return (dcv, *sums, jnp.sum(dcv, axis=0, keepdims=True))

    res = bmap("ssd_conv_bwd_a", fn, (s // tm,),
               [rows(u, tm) for u in us] + [par(conv_w), par(conv_b), rows(dact, tm)],
               [row_out(s, XBC_WIDTH, F32, tm)], [col_acc(XBC_WIDTH)] * 5)
    return res[0], jnp.concatenate(res[1:5], axis=0), res[5][0]


def ssd_conv_bwd_b(dcs, conv_w):
    s = dcs[0].shape[0]
    tm = _pick(s, (256, 128))

    def fn(d0, d1, d2, d3, w):
        return (d0 * w[0:1] + d1 * w[1:2] + d2 * w[2:3] + d3 * w[3:4],)

    return bmap("ssd_conv_bwd_b", fn, (s // tm,), [rows(d, tm) for d in dcs] + [par(conv_w)],
                [row_out(s, XBC_WIDTH, BF16, tm)])[0]


def _mask_dot_exact(mask_bf16, x):
    hi = x.astype(BF16)
    r1 = x - hi.astype(F32)
    mid = r1.astype(BF16)
    lo = (r1 - mid.astype(F32)).astype(BF16)
    return (jnp.dot(mask_bf16, hi, preferred_element_type=F32) + jnp.dot(mask_bf16, mid, preferred_element_type=F32)
            + jnp.dot(mask_bf16, lo, preferred_element_type=F32))


def ssd_dt_fwd(proj, dt_bias, a_log):
    s = proj.shape[0]
    L = SSD_CHUNK

    def fn(dtr, bias, alog):
        dt = _softplus(dtr + bias)
        a = dt * (-jnp.exp(alog))
        row = lax.broadcasted_iota(jnp.int32, (L, L), 0)
        col = lax.broadcasted_iota(jnp.int32, (L, L), 1)
        return dt, _mask_dot_exact((row >= col).astype(BF16), a)

    return bmap("ssd_dt_fwd", fn, (s // L,), [rows(proj, L, LANES, COL_DT), par(dt_bias), par(a_log)],
                [row_out(s, LANES, F32, L), row_out(s, LANES, F32, L)])


def ssd_state_dot(s_fwd, s_rev, a_tot):
    heads, nc = s_fwd.shape[0], s_fwd.shape[1]
    cb = _pick(nc, (8, 4, 2, 1))

    def fn(sf, sr, at):
        return (jnp.sum(jnp.sum(sf * sr, axis=2, keepdims=True), axis=1, keepdims=True) * jnp.exp(at),)

    st = lambda a: (a, (None, cb) + a.shape[2:], lambda h, i: (h, i, 0, 0))
    return bmap("ssd_state_dot", fn, (heads, nc // cb), [st(s_fwd), st(s_rev), st(a_tot)],
                [((heads, nc, 1, 1), F32, (None, cb, 1, 1), lambda h, i: (h, i, 0, 0))])[0]


def ssd_decay_grad(e_incl, v_excl, g_chunk):
    s = e_incl.shape[0]
    L = SSD_CHUNK

    def fn(ev, vv, gv):
        row = lax.broadcasted_iota(jnp.int32, (L, L), 0)
        col = lax.broadcasted_iota(jnp.int32, (L, L), 1)
        return (_mask_dot_exact((col >= row).astype(BF16), ev) + _mask_dot_exact((col < row).astype(BF16), vv) + gv,)

    return bmap("ssd_decay_grad", fn, (s // L,),
                [rows(e_incl, L), rows(v_excl, L), (g_chunk, (None, 1, LANES), lambda i: (i, 0, 0))],
                [row_out(s, LANES, F32, L)])[0]


def ssd_dt_bwd(da, ddt_x, proj, dt_bias, a_log):
    s = da.shape[0]
    tm = _pick(s, (512, 256, 128))

    def fn(dav, dxv, dtr, bias, alog):
        pre = dtr + bias
        a_coef = -jnp.exp(alog)
        ddtr = (dav * a_coef + dxv) * _sigmoid(pre)
        return (ddtr, jnp.sum(ddtr, axis=0, keepdims=True),
                jnp.sum(dav * _softplus(pre) * a_coef, axis=0, keepdims=True))

    ddtr, dbias, dalog = bmap("ssd_dt_bwd", fn, (s // tm,),
                              [rows(da, tm), rows(ddt_x, tm), rows(proj, tm, LANES, COL_DT), par(dt_bias), par(a_log)],
                              [row_out(s, LANES, BF16, tm)], [col_acc(LANES), col_acc(LANES)])
    return ddtr, dbias[0], dalog[0]


def ssd_xdt(x_h, dt_h):
    s = x_h.shape[1]
    tm = _pick(s, (512, 256, 128))
    return bmap("ssd_xdt", lambda xv, dv: (xv * dv,), (SSD_HEADS, s // tm),
                [_head_rows(x_h, tm), _head_rows(dt_h, tm)], [_head_major(s, tm, SSD_P, SSD_HEADS)])[0]


def ssd_gate_fwd(ytil, act, proj, d_full, norm_w):
    s = ytil.shape[0]
    tm = _pick(s, (256, 128))

    def fn(yt, xv, zv, dv, wv):
        u = (yt + xv * dv) * _silu(zv)
        return (u * lax.rsqrt(jnp.mean(u * u, axis=-1, keepdims=True) + NORM_EPS) * wv,)

    return bmap("ssd_gate_fwd", fn, (s // tm,),
                [rows(ytil, tm), rows(act, tm, BR_WIDTH, 0), rows(proj, tm, BR_WIDTH, COL_Z), par(d_full), par(norm_w)],
                [row_out(s, BR_WIDTH, BF16, tm)])[0]


def ssd_gate_bwd(ytil, act, proj, d_full, norm_w, dout):
    s = ytil.shape[0]
    tm = _pick(s, (256, 128))

    def fn(yt, xv, zv, dv, wv, dov):
        y = yt + xv * dv
        sz = _silu(zv)
        u = y * sz
        r = lax.rsqrt(jnp.mean(u * u, axis=-1, keepdims=True) + NORM_EPS)
        un = u * r
        g = dov * wv
        du = r * (g - un * jnp.mean(g * un, axis=-1, keepdims=True))
        dy = du * sz
        return (dy, du * y * _dsilu(zv), jnp.sum(dov * un, axis=0, keepdims=True),
                jnp.sum(dy * xv, axis=0, keepdims=True))

    dy, dz, dw, dd = bmap("ssd_gate_bwd", fn, (s // tm,),
                          [rows(ytil, tm), rows(act, tm, BR_WIDTH, 0), rows(proj, tm, BR_WIDTH, COL_Z), par(d_full),
                           par(norm_w), rows(dout, tm)],
                          [row_out(s, BR_WIDTH, F32, tm), row_out(s, BR_WIDTH, BF16, tm)],
                          [col_acc(BR_WIDTH), col_acc(BR_WIDTH)])
    return dy, dz, dw[0], dd[0]


def ssd_head_bwd(dy_h, y_parts, dxdt_h, dx_parts, xdt_h, x_h, dt_h, d_h):
    s = dy_h.shape[1]
    tm = _pick(s, (512, 256, 128))

    def fn(dy, yi, yc, dxdt, dxi, dxc, xdt, xv, dt, dskip):
        return (dy * dskip + dxdt * dt,
                jnp.sum(dy * (yi + yc) - dxi * xdt, axis=-1, keepdims=True),
                jnp.sum(dxc * xdt, axis=-1, keepdims=True),
                jnp.sum(dxdt * xv, axis=-1, keepdims=True))

    one = ((SSD_HEADS, s, 1), F32, (None, tm, 1), lambda h, i: (h, i, 0))
    return bmap("ssd_head_bwd", fn, (SSD_HEADS, s // tm),
                [_head_rows(a, tm) for a in (dy_h, *y_parts, dxdt_h, *dx_parts, xdt_h, x_h, dt_h)]
                + [(d_h, (None, 1, 1), lambda h, i: (h, 0, 0))],
                [_head_major(s, tm, SSD_P, SSD_HEADS), one, one, one])


def ssd_group_sum(db_h, dc_h):
    s = db_h.shape[1]
    tm = _pick(s, (512, 256, 128))
    e = SSD_HEADS // SSD_GROUPS

    def fn(*v):
        return v[0] + v[1] + v[2] + v[3], v[4] + v[5] + v[6] + v[7]

    def head(arr, j):
        return (arr, (None, tm, SSD_N), lambda g, i: (g * e + j, i, 0))

    out = ((s, SSD_GROUPS * SSD_N), F32, (tm, SSD_N), lambda g, i: (i, g))
    return bmap("ssd_group_sum", fn, (SSD_GROUPS, s // tm),
                [head(db_h, j) for j in range(e)] + [head(dc_h, j) for j in range(e)], [out, out])


def _gate_ins(proj, b_gate, tm):
    b2 = b_gate.reshape(1, 3 * D_MODEL)
    ins = [rows(proj, tm, D_MODEL, COL_GATE + j * D_MODEL) for j in range(3)]
    ins += [(b2, (1, D_MODEL), (lambda j: lambda i: (0, j))(j)) for j in range(3)]
    return ins


def merge_fwd(proj, b_gate, ps):
    s = proj.shape[0]
    tm = 128

    def fn(g0, g1, g2, b0, b1, b2, p0, p1, p2):
        return (_sigmoid(g0 + b0) * p0 + _sigmoid(g1 + b1) * p1 + _sigmoid(g2 + b2) * p2,)

    return bmap("merge_fwd", fn, (s // tm,), _gate_ins(proj, b_gate, tm) + [rows(p, tm) for p in ps],
                [row_out(s, D_MODEL, BF16, tm)])[0]


def merge_bwd(proj, b_gate, ps, dmerged):
    s = proj.shape[0]
    tm = 128

    def fn(g0, g1, g2, b0, b1, b2, p0, p1, p2, dm):
        dps, dgs = [], []
        for g, b, p in ((g0, b0, p0), (g1, b1, p1), (g2, b2, p2)):
            gate = _sigmoid(g + b)
            dps.append(dm * gate)
            dgs.append(dm * p * gate * (1.0 - gate))
        dgl = jnp.concatenate(dgs, axis=1)
        return (*dps, dgl, jnp.sum(dgl, axis=0, keepdims=True))

    res = bmap("merge_bwd", fn, (s // tm,),
               _gate_ins(proj, b_gate, tm) + [rows(p, tm) for p in ps] + [rows(dmerged, tm)],
               [row_out(s, D_MODEL, BF16, tm)] * 3 + [row_out(s, 3 * D_MODEL, BF16, tm)], [col_acc(3 * D_MODEL)])
    return res[0:3], res[3], res[4][0]


def swiglu_fwd(gu):
    s = gu.shape[0]
    tm = 128
    return bmap("swiglu_fwd", lambda g, u: (_silu(g) * u,), (s // tm,),
                [rows(gu, tm, FFN_HIDDEN, 0), rows(gu, tm, FFN_HIDDEN, FFN_HIDDEN)],
                [row_out(s, FFN_HIDDEN, BF16, tm)])[0]


def swiglu_bwd(gu, dact):
    s = gu.shape[0]
    tm = 128

    def fn(g, u, da):
        return (jnp.concatenate([da * u * _dsilu(g), da * _silu(g)], axis=1),)

    return bmap("swiglu_bwd", fn, (s // tm,),
                [rows(gu, tm, FFN_HIDDEN, 0), rows(gu, tm, FFN_HIDDEN, FFN_HIDDEN), rows(dact, tm)],
                [row_out(s, 2 * FFN_HIDDEN, BF16, tm)])[0]


def loss_head(y, target):
    s, d = y.shape
    tm = _pick(s, (256, 128))

    def fn(yv, tv):
        err = yv - tv
        part = jnp.sum(jnp.mean(err * err, axis=-1, keepdims=True), axis=0, keepdims=True)
        return err * (1.0 / d), 0.5 * part

    dy, loss = bmap("loss_head", fn, (s // tm,), [rows(y, tm), rows(target, tm)], [row_out(s, d, F32, tm)],
                    [((1, 1), (1, 1), lambda i: (0, 0))])
    return dy, loss[0, 0]


def _to_heads(a2d, heads, d):
    return a2d.reshape(a2d.shape[0], heads, d).transpose(1, 0, 2)


def _from_heads(a3d):
    return a3d.transpose(1, 0, 2).reshape(a3d.shape[1], a3d.shape[0] * a3d.shape[2])


def _pad_lanes(v):
    return jnp.pad(v.reshape(1, -1), ((0, 0), (0, LANES - v.shape[-1])))


def layer_forward(x, w, p, tabs):
    s = x.shape[0]
    cos2, sin2, ac_ret, ar_ret = tabs
    sv = {"x": x}
    h = rms_fwd(x, p["norm_mix_pre"], BF16, "rms_mix_pre")
    proj = matmul(h, w["w_in"], name="mm_in")
    sv.update(h=h, proj=proj)

    qr, kr = rotary_fwd(proj, cos2, sin2)
    v_ret = (proj, COL_RET + 2 * BR_WIDTH, N_HEADS)
    o_ret = linear_attention(qr, kr, v_ret, ac_ret, ar_ret, chunk=RET_CHUNK, causal=False, reverse=False,
                             heads=N_HEADS, name="ret_fwd")
    y_ret = gn_gate_fwd(o_ret, p["ret_gn_w"], proj)
    sv.update(qr=qr, kr=kr, o_ret=o_ret, y_ret=y_ret)

    qkv = proj[:, COL_SB:COL_SB + 3 * BR_WIDTH].astype(BF16)
    o_sb = sb_forward(qkv)
    y_sb = o_sb.astype(BF16)
    sv.update(qkv=qkv, o_sb=o_sb, y_sb=y_sb)

    xbc = proj[:, COL_XBC:COL_XBC + XBC_WIDTH]
    us = _shifted(xbc, False) + [xbc]
    conv_b = p["ssd_conv_b"].reshape(1, XBC_WIDTH)
    act = ssd_conv_fwd(us, p["ssd_conv_w"], conv_b)
    dt, acum = ssd_dt_fwd(proj, _pad_lanes(p["ssd_dt_bias"]), _pad_lanes(p["ssd_a_log"]))
    nc = s // SSD_CHUNK
    x_h = _to_heads(act[:, :BR_WIDTH], SSD_HEADS, SSD_P)
    dt_h = dt[:, :SSD_HEADS].T[:, :, None]
    acs = acum[:, :SSD_HEADS].T.reshape(SSD_HEADS, nc, SSD_CHUNK)
    ac_h, ar_h = acs[:, :, :, None], acs[:, :, None, :]
    xdt_h = ssd_xdt(x_h, dt_h)
    b_grp = (act, BR_WIDTH, SSD_GROUPS)
    c_grp = (act, BR_WIDTH + SSD_GROUPS * SSD_N, SSD_GROUPS)
    ytil_h, *y_parts, s_fwd = linear_attention(c_grp, b_grp, xdt_h, ac_h, ar_h, chunk=SSD_CHUNK, causal=True,
                                               reverse=False, heads=SSD_HEADS, name="ssd_fwd", parts_out=True)
    ytil = _from_heads(ytil_h)
    d_full = jnp.repeat(p["ssd_d"], SSD_P).reshape(1, BR_WIDTH)
    norm_w = p["ssd_norm_w"].reshape(1, BR_WIDTH)
    y_ssd = ssd_gate_fwd(ytil, act, proj, d_full, norm_w)
    sv.update(act=act, x_h=x_h, dt_h=dt_h, ac_h=ac_h, ar_h=ar_h, xdt_h=xdt_h, y_parts=y_parts, s_fwd=s_fwd, ytil=ytil,
              y_ssd=y_ssd)

    ys = (y_ret, y_sb, y_ssd)
    ps = [matmul(y, wb, name="mm_branch") for y, wb in zip(ys, w["w_branch"])]
    merged = merge_fwd(proj, p["b_gate"], ps)
    mo = matmul(merged, w["w_out"], name="mm_out")
    x1 = rms_fwd(mo, p["norm_mix_post"], F32, "rms_mix_post", resid=x)
    sv.update(ps=ps, merged=merged, mo=mo, x1=x1)

    h2 = rms_fwd(x1, p["norm_ffn_pre"], BF16, "rms_ffn_pre")
    gu = matmul(h2, w["w_gu"], name="mm_gu")
    actf = swiglu_fwd(gu)
    f = matmul(actf, w["w_down"], name="mm_down")
    x2 = rms_fwd(f, p["norm_ffn_post"], F32, "rms_ffn_post", resid=x1)
    sv.update(h2=h2, gu=gu, actf=actf, f=f)
    return x2, sv


def layer_backward(dx2, sv, w, p, tabs):
    cos2, sin2, ac_ret, ar_ret = tabs
    proj = sv["proj"]
    s = proj.shape[0]
    gs = {}

    df, gs["norm_ffn_post"] = rms_bwd(sv["f"], p["norm_ffn_post"], dx2, "rms_ffn_post_bwd", out_dtype=BF16)
    dactf = matmul(df, w["w_down"], nt=True, name="mm_down_dx")
    gw_down = matmul(sv["actf"].T, df, out_dtype=BF16, name="mm_down_dw")
    dgu = swiglu_bwd(sv["gu"], dactf)
    dh2 = matmul(dgu, w["w_gu"], nt=True, name="mm_gu_dx")
    gw_gu = matmul(sv["h2"].T, dgu, out_dtype=BF16, name="mm_gu_dw")
    dx1, gs["norm_ffn_pre"] = rms_bwd(sv["x1"], p["norm_ffn_pre"], dh2, "rms_ffn_pre_bwd", resid=dx2)

    dmo, gs["norm_mix_post"] = rms_bwd(sv["mo"], p["norm_mix_post"], dx1, "rms_mix_post_bwd", out_dtype=BF16)
    dmerged = matmul(dmo, w["w_out"], nt=True, name="mm_out_dx")
    gw_out = matmul(sv["merged"].T, dmo, out_dtype=BF16, name="mm_out_dw")
    dps, dgl, gs["b_gate"] = merge_bwd(proj, p["b_gate"], sv["ps"], dmerged)
    ys = (sv["y_ret"], sv["y_sb"], sv["y_ssd"])
    dys = [matmul(dp, wb, nt=True, name="mm_branch_dx") for dp, wb in zip(dps, w["w_branch"])]
    gw_branch = [matmul(y.T, dp, out_dtype=BF16, name="mm_branch_dw") for y, dp in zip(ys, dps)]

    do_h, dg, gs["ret_gn_w"] = gn_gate_bwd(sv["o_ret"], p["ret_gn_w"], proj, dys[0])
    v_ret = (proj, COL_RET + 2 * BR_WIDTH, N_HEADS)
    ret = dict(chunk=RET_CHUNK, causal=False, heads=N_HEADS)
    dqr = linear_attention(do_h, v_ret, sv["kr"], ac_ret, ar_ret, reverse=False, name="ret_dq", **ret)
    dkr = linear_attention(v_ret, do_h, sv["qr"], ac_ret, ar_ret, reverse=True, name="ret_dk", **ret)
    dv = linear_attention(sv["kr"], sv["qr"], do_h, ac_ret, ar_ret, reverse=True, name="ret_dv",
                          out_dtype=BF16, out_2d=True, **ret)
    dq, dk = rotary_bwd(dqr, dkr, cos2, sin2)

    dsq, dsk, dsv = sb_backward(sv["qkv"], sv["o_sb"], dys[1])

    act = sv["act"]
    d_full = jnp.repeat(p["ssd_d"], SSD_P).reshape(1, BR_WIDTH)
    norm_w = p["ssd_norm_w"].reshape(1, BR_WIDTH)
    dy, dz, gs["ssd_norm_w"], dd_lanes = ssd_gate_bwd(sv["ytil"], act, proj, d_full, norm_w, dys[2])
    gs["ssd_d"] = dd_lanes.reshape(SSD_HEADS, SSD_P).sum(axis=1)
    dy_h = _to_heads(dy, SSD_HEADS, SSD_P)
    b_grp = (act, BR_WIDTH, SSD_GROUPS)
    c_grp = (act, BR_WIDTH + SSD_GROUPS * SSD_N, SSD_GROUPS)
    ssd = dict(chunk=SSD_CHUNK, causal=True, heads=SSD_HEADS)
    ac_h, ar_h, xdt_h = sv["ac_h"], sv["ar_h"], sv["xdt_h"]
    dxdt_h, *dx_parts, s_rev = linear_attention(b_grp, c_grp, dy_h, ac_h, ar_h, reverse=True, name="ssd_dx",
                                                parts_out=True, **ssd)
    dc_h = linear_attention(dy_h, xdt_h, b_grp, ac_h, ar_h, reverse=False, name="ssd_dc", **ssd)
    db_h = linear_attention(xdt_h, dy_h, c_grp, ac_h, ar_h, reverse=True, name="ssd_db", **ssd)
    d_h = p["ssd_d"].reshape(SSD_HEADS, 1, 1)
    dx_h, e_h, v_h, ddtx_h = ssd_head_bwd(dy_h, sv["y_parts"], dxdt_h, dx_parts, xdt_h, sv["x_h"], sv["dt_h"], d_h)
    lanes = lambda a: jnp.pad(a[:, :, 0].T, ((0, 0), (0, LANES - SSD_HEADS)))
    g_chunk = ssd_state_dot(sv["s_fwd"], s_rev, ac_h[:, :, SSD_CHUNK - 1:, :])
    da = ssd_decay_grad(lanes(e_h), lanes(v_h), lanes(g_chunk[:, :, :, 0])[:, None, :])
    dt_bias, a_log = _pad_lanes(p["ssd_dt_bias"]), _pad_lanes(p["ssd_a_log"])
    ddtr, dbias, dalog = ssd_dt_bwd(da, lanes(ddtx_h), proj, dt_bias, a_log)
    gs["ssd_dt_bias"], gs["ssd_a_log"] = dbias[:SSD_HEADS], dalog[:SSD_HEADS]
    db2, dc2 = ssd_group_sum(db_h, dc_h)
    dact = jnp.concatenate([_from_heads(dx_h), db2, dc2], axis=1)
    xbc = proj[:, COL_XBC:COL_XBC + XBC_WIDTH]
    us = _shifted(xbc, False) + [xbc]
    conv_b = p["ssd_conv_b"].reshape(1, XBC_WIDTH)
    dcv, gs["ssd_conv_w"], gs["ssd_conv_b"] = ssd_conv_bwd_a(us, p["ssd_conv_w"], conv_b, dact)
    dxbc = ssd_conv_bwd_b(_shifted(dcv, True) + [dcv], p["ssd_conv_w"])

    dt_cols = jnp.pad(ddtr, ((0, 0), (0, IN_PAD - COL_DT - LANES)))
    dproj = jnp.concatenate([dq, dk, dv, dg, dsq.astype(BF16), dsk.astype(BF16), dsv.astype(BF16),
                             dz, dxbc, dgl, dt_cols], axis=1)
    dh = matmul(dproj, w["w_in"], nt=True, name="mm_in_dx")
    gw_in = matmul(sv["h"].T, dproj, out_dtype=BF16, name="mm_in_dw")
    dx, gs["norm_mix_pre"] = rms_bwd(sv["x"], p["norm_mix_pre"], dh, "rms_mix_pre_bwd", resid=dx1)
    gw = dict(w_in=gw_in, w_branch=gw_branch, w_out=gw_out, w_gu=gw_gu, w_down=gw_down)
    return dx, gw, gs


def in_cols_pad(w):
    zeros = jnp.zeros(w.shape[:-1] + (IN_PAD - IN_COLS,), w.dtype)
    return jnp.concatenate([w[..., :COL_GATE], w[..., COL_GATE + SSD_HEADS:], w[..., COL_GATE:COL_GATE + SSD_HEADS], zeros],
                           axis=-1)


def in_cols_unpad(g):
    return jnp.concatenate([g[..., :COL_GATE], g[..., COL_DT:COL_DT + SSD_HEADS], g[..., COL_GATE:COL_DT]], axis=-1)


W_IN_SHARD = (D_MODEL, IN_COLS // N_SHARD)
BIG = (("w_branch_ret", (BR_WIDTH, D_MODEL // N_SHARD), True),
       ("w_branch_sb", (BR_WIDTH, D_MODEL // N_SHARD), True),
       ("w_branch_ssd", (BR_WIDTH, D_MODEL // N_SHARD), True),
       ("w_out", (D_MODEL // N_SHARD, D_MODEL), False),
       ("ffn_w_gate", (D_MODEL, FFN_HIDDEN // N_SHARD), True),
       ("ffn_w_up", (D_MODEL, FFN_HIDDEN // N_SHARD), True),
       ("ffn_w_down", (FFN_HIDDEN // N_SHARD, D_MODEL), False))
FLAT_ROWS = sum(r * c for _, (r, c), _ in BIG) // LANES
assert FLAT_ROWS * LANES == sum(r * c for _, (r, c), _ in BIG) and FLAT_ROWS % 32 == 0
HALF_ROWS = FLAT_ROWS // 2


def pack_shard(layer_shards, dtype):
    flat = jnp.concatenate([layer_shards[n].reshape(-1).astype(dtype) for n, _, _ in BIG])
    return flat.reshape(FLAT_ROWS, LANES)


def unpack_shard(flat):
    out, off = {}, 0
    v = flat.reshape(-1)
    for n, (r, c), _ in BIG:
        out[n] = v[off:off + r * c].reshape(r, c)
        off += r * c
    return out


def unpack_gathered(full):
    v = full.reshape(N_SHARD, -1)
    out, off = {}, 0
    for n, (r, c), by_cols in BIG:
        seg = v[:, off:off + r * c].reshape(N_SHARD, r, c)
        out[n] = seg.transpose(1, 0, 2).reshape(r, N_SHARD * c) if by_cols else seg.reshape(N_SHARD * r, c)
        off += r * c
    return out


def pack_for_scatter(whole):
    segs = []
    for n, (r, c), by_cols in BIG:
        g = whole[n].astype(BF16)
        g = g.reshape(r, N_SHARD, c).transpose(1, 0, 2) if by_cols else g.reshape(N_SHARD, r, c)
        segs.append(g.reshape(N_SHARD, r * c))
    return jnp.concatenate(segs, axis=1).reshape(N_SHARD, FLAT_ROWS, LANES)


ANY = pl.BlockSpec(memory_space=pl.ANY)


def _place():
    return lax.axis_index("x"), lax.axis_index("y"), lax.axis_index("c")


def _other_chips(x, y):
    return [(1 - x, y), (x, 1 - y), (1 - x, 1 - y)]


def _half(c, rows):
    return pl.ds(pl.multiple_of(c * (rows // 2), 16), rows // 2)


def _sems(n):
    return [pltpu.SemaphoreType.DMA((n,)), pltpu.SemaphoreType.DMA((n,))]


def all_gather_shards(shards):
    n = len(shards)

    def body(*refs):
        x_refs, o_refs = refs[:n], refs[n:2 * n]
        send_sems, recv_sems, local_sems = refs[2 * n:]
        x, y, c = _place()
        j = 2 * x + y
        sibling = (x, y, 1 - c)
        chips = _other_chips(x, y)

        def copy(k, src, dst, to):
            return pltpu.make_async_remote_copy(src_ref=src, dst_ref=dst, send_sem=send_sems.at[k],
                                                recv_sem=recv_sems.at[k], device_id=to, device_id_type=MESH)

        started, own = [], []
        for a, (x_ref, o_ref) in enumerate(zip(x_refs, o_refs)):
            local = pltpu.make_async_copy(x_ref, o_ref.at[j], local_sems.at[a])
            local.start()
            own.append(local)
            mine = _half(c, x_ref.shape[0])
            for k, (cx, cy) in enumerate(chips):
                cp = copy(6 * a + k, x_ref.at[mine], o_ref.at[j, mine], (cx, cy, c))
                cp.start()
                started.append(cp)
        for a, o_ref in enumerate(o_refs):
            mine = _half(c, o_ref.shape[1])
            for k, (cx, cy) in enumerate(chips):
                landed = o_ref.at[2 * cx + cy, mine]
                copy(6 * a + k, landed, landed, (cx, cy, c)).wait_recv()
                cp = copy(6 * a + 3 + k, landed, landed, sibling)
                cp.start()
                started.append(cp)
        for a, o_ref in enumerate(o_refs):
            other = _half(1 - c, o_ref.shape[1])
            for k, (cx, cy) in enumerate(chips):
                landed = o_ref.at[2 * cx + cy, other]
                copy(6 * a + 3 + k, landed, landed, sibling).wait_recv()
        for cp in started:
            cp.wait_send()
        for local in own:
            local.wait()

    return pl.pallas_call(
        body, name="all_gather_shards", in_specs=[ANY] * n, out_specs=[ANY] * n,
        out_shape=[jax.ShapeDtypeStruct((N_SHARD,) + s.shape, s.dtype) for s in shards],
        scratch_shapes=_sems(6 * n) + [pltpu.SemaphoreType.DMA((n,))],
    )(*shards)


def exchange_siblings(gs):
    n = len(gs)

    def body(*refs):
        g_refs, o_refs = refs[:n], refs[n:2 * n]
        send_sems, recv_sems = refs[2 * n:]
        x, y, c = _place()
        copies = []
        for a, (g_ref, o_ref) in enumerate(zip(g_refs, o_refs)):
            cp = pltpu.make_async_remote_copy(src_ref=g_ref.at[:, _half(1 - c, g_ref.shape[1])], dst_ref=o_ref,
                                              send_sem=send_sems.at[a], recv_sem=recv_sems.at[a],
                                              device_id=(x, y, 1 - c), device_id_type=MESH)
            cp.start()
            copies.append(cp)
        for cp in copies:
            cp.wait()

    return pl.pallas_call(
        body, name="exchange_siblings", in_specs=[ANY] * n, out_specs=[ANY] * n,
        out_shape=[jax.ShapeDtypeStruct((N_SHARD, g.shape[1] // 2, g.shape[2]), g.dtype) for g in gs],
        scratch_shapes=_sems(n),
    )(*gs)


def exchange_chips(ps):
    n = len(ps)

    def body(*refs):
        p_refs, o_refs = refs[:n], refs[n:2 * n]
        send_sems, recv_sems, local_sems = refs[2 * n:]
        x, y, c = _place()
        j = 2 * x + y
        copies = []
        for a, (p_ref, o_ref) in enumerate(zip(p_refs, o_refs)):
            local = pltpu.make_async_copy(p_ref.at[j], o_ref.at[j], local_sems.at[a])
            local.start()
            copies.append(local)
            for k, (cx, cy) in enumerate(_other_chips(x, y)):
                cp = pltpu.make_async_remote_copy(src_ref=p_ref.at[2 * cx + cy], dst_ref=o_ref.at[j],
                                                  send_sem=send_sems.at[3 * a + k], recv_sem=recv_sems.at[3 * a + k],
                                                  device_id=(cx, cy, c), device_id_type=MESH)
                cp.start()
                copies.append(cp)
        for cp in copies:
            cp.wait()

    return pl.pallas_call(
        body, name="exchange_chips", in_specs=[ANY] * n, out_specs=[ANY] * n,
        out_shape=[jax.ShapeDtypeStruct(p.shape, p.dtype) for p in ps],
        scratch_shapes=_sems(3 * n) + [pltpu.SemaphoreType.DMA((n,))],
    )(*ps)


def share_halves(reds):
    n = len(reds)

    def body(*refs):
        r_refs, o_refs = refs[:n], refs[n:2 * n]
        send_sems, recv_sems, local_sems = refs[2 * n:]
        x, y, c = _place()
        copies = []
        for a, (r_ref, o_ref) in enumerate(zip(r_refs, o_refs)):
            mine = o_ref.at[_half(c, o_ref.shape[0])]
            local = pltpu.make_async_copy(r_ref, mine, local_sems.at[a])
            local.start()
            cp = pltpu.make_async_remote_copy(src_ref=r_ref, dst_ref=mine, send_sem=send_sems.at[a],
                                              recv_sem=recv_sems.at[a], device_id=(x, y, 1 - c), device_id_type=MESH)
            cp.start()
            copies += [local, cp]
        for cp in copies:
            cp.wait()

    return pl.pallas_call(
        body, name="share_halves", in_specs=[ANY] * n, out_specs=[ANY] * n,
        out_shape=[jax.ShapeDtypeStruct((2 * r.shape[0], r.shape[1]), r.dtype) for r in reds],
        scratch_shapes=_sems(n) + [pltpu.SemaphoreType.DMA((n,))],
    )(*reds)


def _sum_rows(rows, cols):
    best = 16
    for t in range(16, rows + 1, 16):
        if rows % t == 0 and t * cols * 4 <= 1536 * 1024:
            best = t
    return best


def reduce_scatter(gs):
    c = lax.axis_index("c")
    gots = exchange_siblings(gs)
    pairs = []
    for g, got in zip(gs, gots):
        half, cols = got.shape[1], got.shape[2]
        t = _sum_rows(half, cols)
        own = lax.dynamic_slice_in_dim(g, c * half, half, axis=1)
        blk = lambda a: (a, (None, t, cols), lambda j, i: (j, i, 0))
        pairs.append(bmap("sum_siblings", lambda a, b: (a.astype(F32) + b.astype(F32),), (N_SHARD, half // t),
                          [blk(own), blk(got)], [(got.shape, BF16, (None, t, cols), lambda j, i: (j, i, 0))])[0])
    reds = []
    for parts in exchange_chips(pairs):
        half, cols = parts.shape[1], parts.shape[2]
        t = _sum_rows(half, cols)
        part = lambda j: (parts, (None, t, cols), lambda i: (j, i, 0))
        reds.append(bmap("sum_chips",
                         lambda a, b, c_, d: (((a.astype(F32) + b.astype(F32)) + c_.astype(F32)) + d.astype(F32),),
                         (half // t,), [part(j) for j in range(N_SHARD)],
                         [((half, cols), F32, (t, cols), lambda i: (i, 0))])[0])
    return share_halves(reds)


def all_reduce_small(buf):
    r = buf.shape[0]

    def body(x_ref, o_ref, land_ref, send_sems, recv_sems):
        x, y, c = _place()
        me = 4 * x + 2 * y + c
        land_ref[me] = x_ref[...]
        copies = []
        for d in range(1, 8):
            peer = tuple(1 - v if (d >> s) & 1 else v for v, s in ((x, 2), (y, 1), (c, 0)))
            cp = pltpu.make_async_remote_copy(src_ref=x_ref, dst_ref=land_ref.at[me], send_sem=send_sems.at[d - 1],
                                              recv_sem=recv_sems.at[d - 1], device_id=peer, device_id_type=MESH)
            cp.start()
            copies.append(cp)
        for cp in copies:
            cp.wait()
        acc = land_ref[0]
        for s in range(1, 8):
            acc = acc + land_ref[s]
        o_ref[...] = acc

    vmem = pl.BlockSpec(memory_space=pltpu.VMEM)
    return pl.pallas_call(
        body, name="all_reduce_small", in_specs=[vmem], out_specs=vmem,
        out_shape=jax.ShapeDtypeStruct((r, LANES), F32),
        scratch_shapes=[pltpu.VMEM((8, r, LANES), F32), pltpu.SemaphoreType.DMA((7,)), pltpu.SemaphoreType.DMA((7,))],
    )(buf)


def adamw(w, g, m, v, name):
    r, c = w.shape
    tm = r
    for cand in (512, 256, 128, 64, 32, 16, 8):
        if r % cand == 0 and cand * c * 4 * 14 <= 24 * 1024 * 1024:
            tm = cand
            break
    bc1 = 1.0 - ADAM_B1 ** ADAM_STEP
    bc2 = 1.0 - ADAM_B2 ** ADAM_STEP

    def fn(wv, gv, mv, vv):
        m2 = ADAM_B1 * mv + (1.0 - ADAM_B1) * gv
        v2 = ADAM_B2 * vv + (1.0 - ADAM_B2) * (gv * gv)
        delta = -ADAM_LR * ((m2 / bc1) / (jnp.sqrt(v2 / bc2) + ADAM_EPS) + ADAM_WD * wv)
        return delta, m2, v2

    return bmap(name, fn, (r // tm,), [rows(a, tm) for a in (w, g, m, v)], [row_out(r, c, F32, tm)] * 3)


WEIGHTS = ("norm_mix_pre", "norm_mix_post", "norm_ffn_pre", "norm_ffn_post", "w_in", "b_gate", "ret_gn_w", "ssd_conv_w",
           "ssd_conv_b", "ssd_dt_bias", "ssd_a_log", "ssd_d", "ssd_norm_w", "w_branch_ret", "w_branch_sb", "w_branch_ssd",
           "w_out", "ffn_w_gate", "ffn_w_up", "ffn_w_down")
BIG_NAMES = tuple(n for n, _, _ in BIG)
SHARDED = ("w_in",) + BIG_NAMES
SMALL = tuple(n for n in WEIGHTS if n not in SHARDED and n != "ssd_conv_w")
CONV_SHARD = XBC_WIDTH // N_SHARD


def _pack_small(parts):
    flat = jnp.concatenate([p.reshape(-1).astype(F32) for p in parts])
    rows_ = -(-flat.shape[0] // (8 * LANES)) * 8
    return jnp.pad(flat, (0, rows_ * LANES - flat.shape[0])).reshape(rows_, LANES)


def _unpack_small(buf, shapes):
    v, out, off = buf.reshape(-1), [], 0
    for shp in shapes:
        n = int(np.prod(shp))
        out.append(v[off:off + n].reshape(shp))
        off += n
    return out


def kernel(x, positions, norm_mix_pre, norm_mix_post, norm_ffn_pre, norm_ffn_post, w_in, b_gate, ret_gn_w, ssd_conv_w, ssd_conv_b, ssd_dt_bias, ssd_a_log, ssd_d, ssd_norm_w, w_branch_ret, w_branch_sb, w_branch_ssd, w_out, ffn_w_gate, ffn_w_up, ffn_w_down, loss_target, m_norm_mix_pre, m_norm_mix_post, m_norm_ffn_pre, m_norm_ffn_post, m_w_in, m_b_gate, m_ret_gn_w, m_ssd_conv_w, m_ssd_conv_b, m_ssd_dt_bias, m_ssd_a_log, m_ssd_d, m_ssd_norm_w, m_w_branch_ret, m_w_branch_sb, m_w_branch_ssd, m_w_out, m_ffn_w_gate, m_ffn_w_up, m_ffn_w_down, v_norm_mix_pre, v_norm_mix_post, v_norm_ffn_pre, v_norm_ffn_post, v_w_in, v_b_gate, v_ret_gn_w, v_ssd_conv_w, v_ssd_conv_b, v_ssd_dt_bias, v_ssd_a_log, v_ssd_d, v_ssd_norm_w, v_w_branch_ret, v_w_branch_sb, v_w_branch_ssd, v_w_out, v_ffn_w_gate, v_ffn_w_up, v_ffn_w_down):
    given = dict(locals())
    wts = {n: given[n] for n in WEIGHTS}
    mom = {n: given["m_" + n] for n in WEIGHTS}
    var = {n: given["v_" + n] for n in WEIGHTS}
    xi, yi, ci = lax.axis_index("x"), lax.axis_index("y"), lax.axis_index("c")
    shard_id = 2 * xi + yi

    conv_mine = jnp.where(ci == 0, ssd_conv_w, 0.0)
    conv_slots = lax.dynamic_update_slice_in_dim(jnp.zeros((DEPTH, SSD_CONV, XBC_WIDTH), F32), conv_mine,
                                                 shard_id * CONV_SHARD, axis=2)
    conv_buf = _pack_small([conv_slots])
    conv_whole = _unpack_small(all_reduce_small(conv_buf), [(DEPTH, SSD_CONV, XBC_WIDTH)])[0]

    tabs = rope_tables(positions[0]) + ret_decay_tables()
    layer_w, layer_p = [], []
    for l in range(DEPTH):
        flat, w_in_all = all_gather_shards([pack_shard({n: wts[n][l] for n in BIG_NAMES}, BF16), w_in[l].astype(BF16)])
        full = unpack_gathered(flat)
        layer_w.append(dict(
            w_in=in_cols_pad(w_in_all.transpose(1, 0, 2).reshape(D_MODEL, IN_COLS)),
            w_branch=[full["w_branch_ret"], full["w_branch_sb"], full["w_branch_ssd"]],
            w_out=full["w_out"],
            w_gu=jnp.concatenate([full["ffn_w_gate"], full["ffn_w_up"]], axis=1),
            w_down=full["ffn_w_down"]))
        p = {n: wts[n][l] for n in SMALL}
        p["ssd_conv_w"] = conv_whole[l]
        layer_p.append(p)

    act = x[0]
    saved = []
    for l in range(DEPTH):
        act, sv = layer_forward(act, layer_w[l], layer_p[l], tabs)
        saved.append(sv)
    dact, loss_part = loss_head(act, loss_target[0])

    big_grads = [None] * DEPTH
    small_grads = [None] * DEPTH
    for l in reversed(range(DEPTH)):
        dact, gw, small_grads[l] = layer_backward(dact, saved[l], layer_w[l], layer_p[l], tabs)
        gu = gw["w_gu"]
        whole = dict(w_branch_ret=gw["w_branch"][0], w_branch_sb=gw["w_branch"][1],
                     w_branch_ssd=gw["w_branch"][2], w_out=gw["w_out"], ffn_w_gate=gu[:, :FFN_HIDDEN],
                     ffn_w_up=gu[:, FFN_HIDDEN:], ffn_w_down=gw["w_down"])
        g_in = in_cols_unpad(gw["w_in"]).reshape(D_MODEL, N_SHARD, IN_COLS // N_SHARD).transpose(1, 0, 2)
        red_flat, red_in = reduce_scatter([pack_for_scatter(whole), g_in])
        big_grads[l] = unpack_shard(red_flat)
        big_grads[l]["w_in"] = red_in

    small_names = SMALL + ("ssd_conv_w",)
    small_parts = [jnp.stack([small_grads[l][n] for l in range(DEPTH)]) for n in small_names]
    small_shapes = [p.shape for p in small_parts]
    summed = all_reduce_small(_pack_small([loss_part.reshape(1)] + small_parts))
    loss, *small_sum = _unpack_small(summed, [(1,)] + small_shapes)
    grads = dict(zip(small_names, small_sum))
    grads["ssd_conv_w"] = lax.dynamic_slice_in_dim(grads["ssd_conv_w"], shard_id * CONV_SHARD, CONV_SHARD, axis=2)
    for n in SHARDED:
        grads[n] = jnp.stack([big_grads[l][n] for l in range(DEPTH)])

    delta, new_m, new_v = {}, {}, {}
    for n in WEIGHTS:
        shape = wts[n].shape
        two_d = (-1, shape[-1]) if len(shape) == 3 else (8, -1) if shape[-1] * shape[0] % (8 * LANES) == 0 else shape
        args = [a.reshape(two_d) for a in (wts[n], grads[n], mom[n], var[n])]
        d, m2, v2 = adamw(*args, name="adamw_" + n)
        delta[n], new_m[n], new_v[n] = d.reshape(shape), m2.reshape(shape), v2.reshape(shape)

    return (loss.reshape(()), dact[None], *[grads[n] for n in WEIGHTS], *[delta[n] for n in WEIGHTS],
            *[new_m[n] for n in WEIGHTS], *[new_v[n] for n in WEIGHTS])
```

```python
import functools
import math

import numpy as np
import jax
import jax.numpy as jnp
from jax import lax
from jax.experimental import pallas as pl
from jax.experimental.pallas import tpu as pltpu

F32 = jnp.float32
BF16 = jnp.bfloat16
MESH = pl.DeviceIdType.MESH

D_MODEL = 2048
DEPTH = 4
RET_CHUNK = 64
NORM_EPS = 1e-6
HEAD = 128
N_HEADS = 8
BR_WIDTH = 1024
SSD_P = 64
SSD_HEADS = 16
SSD_GROUPS = 4
SSD_N = 128
SSD_CHUNK = 128
SSD_CONV = 4
FFN_HIDDEN = 5632
IN_COLS = 16400
IN_PAD = 16896
COL_RET, COL_SB, COL_Z, COL_XBC, COL_GATE, COL_DT = 0, 4096, 7168, 8192, 10240, 16384
N_SHARD = 4
LANES = 128
VMEM_LIMIT = 48 * 1024 * 1024

ADAM_LR, ADAM_B1, ADAM_B2, ADAM_EPS, ADAM_WD, ADAM_STEP = 0.001, 0.9, 0.999, 1e-08, 0.01, 10


def _pick(n, prefs):
    for p in prefs:
        if n % p == 0:
            return p
    return n


def _params(sem=None):
    kw = dict(vmem_limit_bytes=VMEM_LIMIT)
    if sem is not None:
        kw["dimension_semantics"] = sem
    return pltpu.CompilerParams(**kw)


MATMUL_VMEM = 36 * 1024 * 1024


def _matmul_tiles(m, n, k, out_bytes):
    def options(dim, prefs):
        opts = [p for p in prefs if dim % p == 0]
        return opts or [dim]

    best, best_vol = None, -1
    for tm in options(m, (1408, 1024, 512, 256, 128)):
        for tn in options(n, (1536, 1408, 1024, 512, 256, 128)):
            for tk in options(k, (2048, 1536, 1408, 1024, 512, 256, 128)):
                need = 2 * 2 * (tm * tk + tk * tn) + tm * tn * (4 + 2 * out_bytes)
                vol = tm * tn * tk
                if need <= MATMUL_VMEM and vol > best_vol:
                    best, best_vol = (tm, tn, tk), vol
    return best


def matmul(a, b, *, nt=False, out_dtype=F32, name="matmul"):
    m, k = a.shape
    n = b.shape[0] if nt else b.shape[1]
    tm, tn, tk = _matmul_tiles(m, n, k, jnp.dtype(out_dtype).itemsize)
    nk = k // tk

    def body(a_ref, b_ref, o_ref, acc_ref):
        kk = pl.program_id(2)
        if nt:
            part = lax.dot_general(a_ref[...], b_ref[...], (((1,), (1,)), ((), ())), preferred_element_type=F32)
        else:
            part = jnp.dot(a_ref[...], b_ref[...], preferred_element_type=F32)

        @pl.when(kk == 0)
        def _():
            acc_ref[...] = part

        @pl.when(kk > 0)
        def _():
            acc_ref[...] += part

        @pl.when(kk == nk - 1)
        def _():
            o_ref[...] = acc_ref[...].astype(o_ref.dtype)

    b_spec = pl.BlockSpec((tn, tk), lambda i, j, kk: (j, kk)) if nt else pl.BlockSpec((tk, tn), lambda i, j, kk: (kk, j))
    return pl.pallas_call(
        body, name=name, grid=(m // tm, n // tn, nk),
        in_specs=[pl.BlockSpec((tm, tk), lambda i, j, kk: (i, kk)), b_spec],
        out_specs=pl.BlockSpec((tm, tn), lambda i, j, kk: (i, j)),
        out_shape=jax.ShapeDtypeStruct((m, n), out_dtype),
        scratch_shapes=[pltpu.VMEM((tm, tn), F32)],
        compiler_params=_params(("parallel", "parallel", "arbitrary")),
    )(a, b)


def bmap(name, fn, grid, ins, outs, accs=()):
    n_in, n_out = len(ins), len(outs)
    last = len(grid) - 1

    def store_acc(ref, val, first):
        @pl.when(first)
        def _():
            ref[...] = val

        @pl.when(jnp.logical_not(first))
        def _():
            ref[...] += val

    def body(*refs):
        vals = fn(*[r[...] for r in refs[:n_in]])
        if not isinstance(vals, (tuple, list)):
            vals = (vals,)
        for r, v in zip(refs[n_in:n_in + n_out], vals[:n_out]):
            r[...] = v.astype(r.dtype)
        first = pl.program_id(last) == 0
        for r, v in zip(refs[n_in + n_out:], vals[n_out:]):
            store_acc(r, v.astype(F32), first)

    sem = ("parallel",) * last + (("arbitrary",) if accs else ("parallel",))
    res = pl.pallas_call(
        body, name=name, grid=grid,
        in_specs=[pl.BlockSpec(bs, im) for (_, bs, im) in ins],
        out_specs=[pl.BlockSpec(bs, im) for (_, _, bs, im) in outs] + [pl.BlockSpec(bs, im) for (_, bs, im) in accs],
        out_shape=[jax.ShapeDtypeStruct(s, dt) for (s, dt, _, _) in outs] + [jax.ShapeDtypeStruct(s, F32) for (s, _, _) in accs],
        compiler_params=_params(sem),
    )(*[a for (a, _, _) in ins])
    return res


def rows(a, tm, width=None, col=0):
    width = a.shape[1] if width is None else width
    assert col % width == 0
    cb = col // width
    return (a, (tm, width), lambda i: (i, cb))


def par(a):
    return (a, a.shape, lambda *_: (0,) * a.ndim)


def row_out(s, width, dtype, tm):
    return ((s, width), dtype, (tm, width), lambda i: (i, 0))


def col_acc(width):
    return ((1, width), (1, width), lambda i: (0, 0))


def _sigmoid(x):
    return 1.0 / (1.0 + jnp.exp(-x))


def _silu(x):
    return x * _sigmoid(x)


def _dsilu(x):
    s = _sigmoid(x)
    return s * (1.0 + x * (1.0 - s))


def _softplus(x):
    return jnp.maximum(x, 0.0) + jnp.log(1.0 + jnp.exp(-jnp.abs(x)))


def _split_bf16(x):
    hi = x.astype(BF16)
    lo = (x - hi.astype(F32)).astype(BF16)
    return hi, lo


def _dot_exact_mask(x, mask_bf16):
    hi, lo = _split_bf16(x)
    return (jnp.dot(hi, mask_bf16, preferred_element_type=F32)
            + jnp.dot(lo, mask_bf16, preferred_element_type=F32))


def rms_fwd(x, w, out_dtype, name, resid=None):
    s, d = x.shape
    tm = _pick(s, (256, 128))

    def fn(xv, wv, *rest):
        r = lax.rsqrt(jnp.mean(xv * xv, axis=-1, keepdims=True) + NORM_EPS)
        y = xv * r * wv
        if rest:
            y = y + rest[0]
        return (y,)

    ins = [rows(x, tm), par(w.reshape(1, d))]
    if resid is not None:
        ins.append(rows(resid, tm))
    return bmap(name, fn, (s // tm,), ins, [row_out(s, d, out_dtype, tm)])[0]


def rms_bwd(x, w, dy, name, resid=None, out_dtype=F32):
    s, d = x.shape
    tm = _pick(s, (256, 128))

    def fn(xv, wv, dyv, *rest):
        dyv = dyv.astype(F32)
        r = lax.rsqrt(jnp.mean(xv * xv, axis=-1, keepdims=True) + NORM_EPS)
        xn = xv * r
        g = dyv * wv
        dx = r * (g - xn * jnp.mean(g * xn, axis=-1, keepdims=True))
        if rest:
            dx = dx + rest[0]
        return dx, jnp.sum(dyv * xn, axis=0, keepdims=True)

    ins = [rows(x, tm), par(w.reshape(1, d)), rows(dy, tm)]
    if resid is not None:
        ins.append(rows(resid, tm))
    dx, dw = bmap(name, fn, (s // tm,), ins, [row_out(s, d, out_dtype, tm)], [col_acc(d)])
    return dx, dw[0]


def linear_attention(q, k, v, ac, ar, *, chunk, causal, reverse, heads, name, out_dtype=F32, out_2d=False,
                     parts_out=False):
    def dims(op):
        return (op[0].shape[0], HEAD) if isinstance(op, tuple) else (op.shape[1], op.shape[2])

    s_len, dk = dims(q)
    dv = dims(v)[1]
    L = chunk
    cpb = _pick(s_len // L, (8, 4, 2, 1))
    nb = s_len // (L * cpb)
    a_chunks = ac.shape[1]

    def blk(i):
        return nb - 1 - i if reverse else i

    def seq_spec(op):
        if isinstance(op, tuple):
            _, first_col, nh = op
            rep, cb = heads // nh, first_col // HEAD
            return pl.BlockSpec((L * cpb, HEAD), lambda h, i: (blk(i), cb + h // rep))
        rep = heads // op.shape[0]
        return pl.BlockSpec((None, L * cpb, op.shape[2]), lambda h, i: (h // rep, blk(i), 0))

    def a_spec(arr, shape):
        rep = heads // arr.shape[0]
        if a_chunks == 1:
            return pl.BlockSpec((None, 1) + shape, lambda h, i: (h // rep, 0, 0, 0))
        return pl.BlockSpec((None, cpb) + shape, lambda h, i: (h // rep, blk(i), 0, 0))

    def body(q_ref, k_ref, v_ref, ac_ref, ar_ref, *rest):
        o_refs, state_ref = rest[:-1], rest[-1]

        @pl.when(pl.program_id(1) == 0)
        def _():
            state_ref[...] = jnp.zeros_like(state_ref)

        row = lax.broadcasted_iota(jnp.int32, (L, L), 0)
        col = lax.broadcasted_iota(jnp.int32, (L, L), 1)

        def one(step, carry):
            c = cpb - 1 - step if reverse else step
            ai = 0 if a_chunks == 1 else c
            sl = pl.ds(pl.multiple_of(c * L, L), L)
            qc = q_ref[sl, :].astype(F32)
            kc = k_ref[sl, :].astype(F32)
            vf = v_ref[sl, :].astype(F32)
            vc = vf.astype(BF16)
            a_col = ac_ref[ai]
            a_row = ar_ref[ai]
            a_last = ac_ref[ai, pl.ds(L - 1, 1), :]
            seg = a_col - a_row
            if not causal:
                dec = jnp.exp(-jnp.abs(seg))
            elif reverse:
                dec = jnp.where(col > row, jnp.exp(jnp.minimum(-seg, 0.0)), 0.0)
            else:
                dec = jnp.where(row > col, jnp.exp(jnp.minimum(seg, 0.0)), 0.0)
            if reverse:
                qd, kd = jnp.exp(a_last - a_col), jnp.exp(a_col)
            else:
                qd, kd = jnp.exp(a_col), jnp.exp(a_last - a_col)
            sc = lax.dot_general(qc.astype(BF16), kc.astype(BF16), (((1,), (1,)), ((), ())),
                                 preferred_element_type=F32) * dec
            state = state_ref[...]
            intra = jnp.dot(sc.astype(BF16), vc, preferred_element_type=F32)
            cross = jnp.dot((qc * qd).astype(BF16), state.astype(BF16), preferred_element_type=F32)
            out = intra + cross
            if parts_out:
                o_refs[1][sl, :] = cross
                o_refs[2][c] = state
            if causal:
                out = out + jnp.sum(qc * kc, axis=-1, keepdims=True) * vf
            o_refs[0][sl, :] = out.astype(o_refs[0].dtype)
            upd = lax.dot_general((kc * kd).astype(BF16), vc, (((0,), (0,)), ((), ())),
                                  preferred_element_type=F32)
            state_ref[...] = state * jnp.exp(a_last) + upd
            return carry

        lax.fori_loop(0, cpb, one, 0)

    if out_2d:
        out_spec = pl.BlockSpec((L * cpb, dv), lambda h, i: (blk(i), h))
        out_shape = jax.ShapeDtypeStruct((s_len, heads * dv), out_dtype)
    else:
        out_spec = pl.BlockSpec((None, L * cpb, dv), lambda h, i: (h, blk(i), 0))
        out_shape = jax.ShapeDtypeStruct((heads, s_len, dv), out_dtype)
    out_specs, out_shapes = [out_spec], [out_shape]
    if parts_out:
        assert not out_2d and out_dtype == F32
        out_specs += [out_spec, pl.BlockSpec((None, cpb, dk, dv), lambda h, i: (h, blk(i), 0, 0))]
        out_shapes += [out_shape, jax.ShapeDtypeStruct((heads, s_len // L, dk, dv), F32)]
    arrs = [op[0] if isinstance(op, tuple) else op for op in (q, k, v)]
    res = pl.pallas_call(
        body, name=name, grid=(heads, nb),
        in_specs=[seq_spec(q), seq_spec(k), seq_spec(v), a_spec(ac, (L, 1)), a_spec(ar, (1, L))],
        out_specs=out_specs, out_shape=out_shapes,
        scratch_shapes=[pltpu.VMEM((dk, dv), F32)],
        compiler_params=_params(("parallel", "arbitrary")),
    )(*arrs, ac, ar)
    return res if parts_out else res[0]


SB_KEYS = 128


def _sb_tile(q, kb, scale, valid, tri):
    z = lax.dot_general(q, kb, (((1,), (1,)), ((), ())), preferred_element_type=F32) * scale
    sp = _softplus(z)
    lk = -sp if valid is None else jnp.where(valid, -sp, 0.0)
    later_in = _dot_exact_mask(lk, tri)
    return z, lk, (z - sp) + later_in


def _sb_loops(i, ratio, step, init):
    carry = lax.fori_loop(0, ratio, lambda n, c: step((i + 1) * ratio - 1 - n, True, c), init)
    return lax.fori_loop(0, i * ratio, lambda n, c: step(i * ratio - 1 - n, False, c), carry)


def _sb_masks(i, bq):
    bk = SB_KEYS
    row = lax.broadcasted_iota(jnp.int32, (bk, bk), 0)
    col = lax.broadcasted_iota(jnp.int32, (bk, bk), 1)
    qpos = i * bq + lax.broadcasted_iota(jnp.int32, (bq, bk), 0)
    kcol = lax.broadcasted_iota(jnp.int32, (bq, bk), 1)
    return row, col, lambda kbi: (kbi * bk + kcol) < qpos


def sb_forward(qkv, name="sb_fwd"):
    s_len = qkv.shape[0]
    bk = SB_KEYS
    bq = _pick(s_len, (512, 256, 128))
    scale = HEAD ** -0.5

    def body(q_ref, k_ref, v_ref, o_ref):
        i = pl.program_id(1)
        q = q_ref[...]
        row, col, valid_of = _sb_masks(i, bq)
        tri_gt = (row > col).astype(BF16)

        def step(kbi, masked, carry):
            acc, a_run = carry
            sl = pl.ds(pl.multiple_of(kbi * bk, bk), bk)
            valid = valid_of(kbi) if masked else None
            _, lk, expo = _sb_tile(q, k_ref[sl, :], scale, valid, tri_gt)
            w = jnp.exp(expo + a_run)
            if masked:
                w = jnp.where(valid, w, 0.0)
            acc = acc + jnp.dot(w.astype(BF16), v_ref[sl, :], preferred_element_type=F32)
            return acc, a_run + jnp.sum(lk, axis=-1, keepdims=True)

        acc, _ = _sb_loops(i, bq // bk, step, (jnp.zeros((bq, HEAD), F32), jnp.zeros((bq, 1), F32)))
        o_ref[...] = acc

    return pl.pallas_call(
        body, name=name, grid=(N_HEADS, s_len // bq),
        in_specs=[pl.BlockSpec((bq, HEAD), lambda h, i: (i, h)),
                  pl.BlockSpec((s_len, HEAD), lambda h, i: (0, N_HEADS + h)),
                  pl.BlockSpec((s_len, HEAD), lambda h, i: (0, 2 * N_HEADS + h))],
        out_specs=pl.BlockSpec((bq, HEAD), lambda h, i: (i, h)),
        out_shape=jax.ShapeDtypeStruct((s_len, BR_WIDTH), F32),
        compiler_params=_params(("parallel", "arbitrary")),
    )(qkv, qkv, qkv)


def sb_backward(qkv, o, do, name="sb_bwd"):
    s_len = qkv.shape[0]
    bk = SB_KEYS
    bq = _pick(s_len, (512, 256, 128))
    scale = HEAD ** -0.5

    def body(q_ref, k_ref, v_ref, o_ref, do_ref, dq_ref, dk_ref, dv_ref):
        i = pl.program_id(1)

        @pl.when(i == 0)
        def _():
            dk_ref[...] = jnp.zeros_like(dk_ref)
            dv_ref[...] = jnp.zeros_like(dv_ref)

        q = q_ref[...]
        dob = do_ref[...].astype(BF16)
        delta = jnp.sum(dob.astype(F32) * o_ref[...], axis=-1, keepdims=True)
        row, col, valid_of = _sb_masks(i, bq)
        tri_gt = (row > col).astype(BF16)
        tri_ge = (row >= col).astype(BF16)

        def step(kbi, masked, carry):
            dq, a_run, e_run = carry
            sl = pl.ds(pl.multiple_of(kbi * bk, bk), bk)
            valid = valid_of(kbi) if masked else None
            kb = k_ref[sl, :]
            vb = v_ref[sl, :]
            z, lk, expo = _sb_tile(q, kb, scale, valid, tri_gt)
            w = jnp.exp(expo + a_run)
            if masked:
                w = jnp.where(valid, w, 0.0)
            wb = w.astype(BF16)
            beta = _sigmoid(z)
            dw = lax.dot_general(dob, vb, (((1,), (1,)), ((), ())), preferred_element_type=F32)
            e = dw * wb.astype(F32)
            suffix = e_run + _dot_exact_mask(e, tri_ge)
            dz = (e * (1.0 - beta) - beta * (delta - suffix)) * scale
            if masked:
                dz = jnp.where(valid, dz, 0.0)
            dzb = dz.astype(BF16)
            dq = dq + jnp.dot(dzb, kb, preferred_element_type=F32)
            dk_ref[sl, :] += lax.dot_general(dzb, q, (((0,), (0,)), ((), ())), preferred_element_type=F32)
            dv_ref[sl, :] += lax.dot_general(wb, dob, (((0,), (0,)), ((), ())),
                                             preferred_element_type=F32)
            return (dq, a_run + jnp.sum(lk, axis=-1, keepdims=True),
                    e_run + jnp.sum(e, axis=-1, keepdims=True))

        zero = jnp.zeros((bq, 1), F32)
        dq, _, _ = _sb_loops(i, bq // bk, step, (jnp.zeros((bq, HEAD), F32), zero, zero))
        dq_ref[...] = dq

    blk = pl.BlockSpec((bq, HEAD), lambda h, i: (i, h))
    full = pl.BlockSpec((s_len, HEAD), lambda h, i: (0, h))
    shape = jax.ShapeDtypeStruct((s_len, BR_WIDTH), F32)
    return pl.pallas_call(
        body, name=name, grid=(N_HEADS, s_len // bq),
        in_specs=[blk,
                  pl.BlockSpec((s_len, HEAD), lambda h, i: (0, N_HEADS + h)),
                  pl.BlockSpec((s_len, HEAD), lambda h, i: (0, 2 * N_HEADS + h)),
                  blk, blk],
        out_specs=[blk, full, full],
        out_shape=[shape, shape, shape],
        compiler_params=_params(("parallel", "arbitrary")),
    )(qkv, qkv, qkv, o, do)


ROPE_BASE = 10000.0


def rope_tables(positions):
    half = HEAD // 2
    inv_freq = ROPE_BASE ** (-2.0 * jnp.arange(half, dtype=F32) / HEAD)
    ang = positions.astype(F32)[:, None] * inv_freq
    cos, sin = jnp.cos(ang), jnp.sin(ang)
    return jnp.concatenate([cos, cos], 1), jnp.concatenate([-sin, sin], 1)


def ret_decay_tables():
    lg = np.log1p(-np.exp2(-5.0 - np.arange(N_HEADS, dtype=np.float64)))
    ac = (lg[:, None] * (np.arange(RET_CHUNK) + 1.0)[None, :]).astype(np.float32)
    return jnp.asarray(ac[:, None, :, None]), jnp.asarray(ac[:, None, None, :])


def _head_major(s, tm, d=HEAD, heads=N_HEADS, dtype=F32):
    return ((heads, s, d), dtype, (None, tm, d), lambda h, i: (h, i, 0))


def _head_cols(arr, tm, first_col):
    cb = first_col // HEAD
    return (arr, (tm, HEAD), lambda h, i: (i, cb + h))


def _head_rows(arr, tm):
    return (arr, (None, tm, arr.shape[2]), lambda h, i: (h, i, 0))


def rotary_fwd(proj, cos2, sin2):
    s = proj.shape[0]
    tm = _pick(s, (512, 256, 128))

    def fn(qv, kv, c, sn):
        def rot(t):
            return t * c + pltpu.roll(t, HEAD // 2, 1) * sn
        return rot(qv), rot(kv) * (HEAD ** -0.5)

    tab = lambda t: (t, (tm, HEAD), lambda h, i: (i, 0))
    return bmap("rotary_fwd", fn, (N_HEADS, s // tm),
                [_head_cols(proj, tm, COL_RET), _head_cols(proj, tm, COL_RET + BR_WIDTH), tab(cos2), tab(sin2)],
                [_head_major(s, tm), _head_major(s, tm)])


def rotary_bwd(dqr, dkr, cos2, sin2):
    s = dqr.shape[1]
    tm = _pick(s, (512, 256, 128))

    def fn(dq, dk, c, sn):
        def unrot(t):
            return t * c - pltpu.roll(t, HEAD // 2, 1) * sn
        return unrot(dq), unrot(dk) * (HEAD ** -0.5)

    tab = lambda t: (t, (tm, HEAD), lambda h, i: (i, 0))
    out = ((s, BR_WIDTH), BF16, (tm, HEAD), lambda h, i: (i, h))
    return bmap("rotary_bwd", fn, (N_HEADS, s // tm),
                [_head_rows(dqr, tm), _head_rows(dkr, tm), tab(cos2), tab(sin2)], [out, out])


def _group_norm(o):
    mu = jnp.mean(o, axis=-1, keepdims=True)
    oc = o - mu
    r = lax.rsqrt(jnp.mean(oc * oc, axis=-1, keepdims=True) + NORM_EPS)
    return oc * r, r


def gn_gate_fwd(o, gn_w, proj):
    s = o.shape[1]
    tm = _pick(s, (512, 256, 128))

    def fn(ov, wv, gv):
        on, _ = _group_norm(ov)
        return (on * wv * _silu(gv),)

    w3 = gn_w.reshape(N_HEADS, 1, HEAD)
    return bmap("gn_gate_fwd", fn, (N_HEADS, s // tm),
                [_head_rows(o, tm), (w3, (None, 1, HEAD), lambda h, i: (h, 0, 0)),
                 _head_cols(proj, tm, COL_RET + 3 * BR_WIDTH)],
                [((s, BR_WIDTH), BF16, (tm, HEAD), lambda h, i: (i, h))])[0]


def gn_gate_bwd(o, gn_w, proj, dy):
    s = o.shape[1]
    tm = _pick(s, (512, 256, 128))

    def fn(ov, wv, gv, dyv):
        on, r = _group_norm(ov)
        yn = on * wv
        dyn = dyv * _silu(gv)
        dg = dyv * yn * _dsilu(gv)
        don = dyn * wv
        do = r * (don - jnp.mean(don, axis=-1, keepdims=True) - on * jnp.mean(don * on, axis=-1, keepdims=True))
        return do, dg, jnp.sum(dyn * on, axis=0, keepdims=True)

    w3 = gn_w.reshape(N_HEADS, 1, HEAD)
    do, dg, dw = bmap("gn_gate_bwd", fn, (N_HEADS, s // tm),
                      [_head_rows(o, tm), (w3, (None, 1, HEAD), lambda h, i: (h, 0, 0)),
                       _head_cols(proj, tm, COL_RET + 3 * BR_WIDTH), _head_cols(dy, tm, 0)],
                      [_head_major(s, tm), ((s, BR_WIDTH), BF16, (tm, HEAD), lambda h, i: (i, h))],
                      [((N_HEADS, 1, HEAD), (None, 1, HEAD), lambda h, i: (h, 0, 0))])
    return do, dg, dw.reshape(BR_WIDTH)


XBC_WIDTH = 2048


def _shifted(u, back):
    s = u.shape[0]
    n = SSD_CONV - 1
    if back:
        return [jnp.pad(u, ((0, n - k), (0, 0)))[n - k:] for k in range(n)]
    return [jnp.pad(u, ((n - k, 0), (0, 0)))[:s] for k in range(n)]


def _conv(us, w, b):
    return us[0] * w[0:1] + us[1] * w[1:2] + us[2] * w[2:3] + us[3] * w[3:4] + b


def ssd_conv_fwd(us, conv_w, conv_b):
    s = us[0].shape[0]
    tm = _pick(s, (256, 128))

    def fn(u0, u1, u2, u3, w, b):
        return (_silu(_conv((u0, u1, u2, u3), w, b)),)

    return bmap("ssd_conv_fwd", fn, (s // tm,), [rows(u, tm) for u in us] + [par(conv_w), par(conv_b)],
                [row_out(s, XBC_WIDTH, F32, tm)])[0]


def ssd_conv_bwd_a(us, conv_w, conv_b, dact):
    s = dact.shape[0]
    tm = _pick(s, (256, 128))

    def fn(u0, u1, u2, u3, w, b, da):
        dcv = da * _dsilu(_conv((u0, u1, u2, u3), w, b))
        sums = [jnp.sum(dcv * u, axis=0, keepdims=True) for u in (u0, u1, u2, u3)]
        return (dcv, *sums, jnp.sum(dcv, axis=0, keepdims=True))

    res = bmap("ssd_conv_bwd_a", fn, (s // tm,),
               [rows(u, tm) for u in us] + [par(conv_w), par(conv_b), rows(dact, tm)],
               [row_out(s, XBC_WIDTH, F32, tm)], [col_acc(XBC_WIDTH)] * 5)
    return res[0], jnp.concatenate(res[1:5], axis=0), res[5][0]


def ssd_conv_bwd_b(dcs, conv_w):
    s = dcs[0].shape[0]
    tm = _pick(s, (256, 128))

    def fn(d0, d1, d2, d3, w):
        return (d0 * w[0:1] + d1 * w[1:2] + d2 * w[2:3] + d3 * w[3:4],)

    return bmap("ssd_conv_bwd_b", fn, (s // tm,), [rows(d, tm) for d in dcs] + [par(conv_w)],
                [row_out(s, XBC_WIDTH, BF16, tm)])[0]


def _mask_dot_exact(mask_bf16, x):
    hi = x.astype(BF16)
    r1 = x - hi.astype(F32)
    mid = r1.astype(BF16)
    lo = (r1 - mid.astype(F32)).astype(BF16)
    return (jnp.dot(mask_bf16, hi, preferred_element_type=F32) + jnp.dot(mask_bf16, mid, preferred_element_type=F32)
            + jnp.dot(mask_bf16, lo, preferred_element_type=F32))


def ssd_dt_fwd(proj, dt_bias, a_log):
    s = proj.shape[0]
    L = SSD_CHUNK

    def fn(dtr, bias, alog):
        dt = _softplus(dtr + bias)
        a = dt * (-jnp.exp(alog))
        row = lax.broadcasted_iota(jnp.int32, (L, L), 0)
        col = lax.broadcasted_iota(jnp.int32, (L, L), 1)
        return dt, _mask_dot_exact((row >= col).astype(BF16), a)

    return bmap("ssd_dt_fwd", fn, (s // L,), [rows(proj, L, LANES, COL_DT), par(dt_bias), par(a_log)],
                [row_out(s, LANES, F32, L), row_out(s, LANES, F32, L)])


def ssd_state_dot(s_fwd, s_rev, a_tot):
    heads, nc = s_fwd.shape[0], s_fwd.shape[1]
    cb = _pick(nc, (8, 4, 2, 1))

    def fn(sf, sr, at):
        return (jnp.sum(jnp.sum(sf * sr, axis=2, keepdims=True), axis=1, keepdims=True) * jnp.exp(at),)

    st = lambda a: (a, (None, cb) + a.shape[2:], lambda h, i: (h, i, 0, 0))
    return bmap("ssd_state_dot", fn, (heads, nc // cb), [st(s_fwd), st(s_rev), st(a_tot)],
                [((heads, nc, 1, 1), F32, (None, cb, 1, 1), lambda h, i: (h, i, 0, 0))])[0]


def ssd_decay_grad(e_incl, v_excl, g_chunk, own):
    s = e_incl.shape[0]
    L = SSD_CHUNK

    def fn(ev, vv, gv, ov):
        row = lax.broadcasted_iota(jnp.int32, (L, L), 0)
        col = lax.broadcasted_iota(jnp.int32, (L, L), 1)
        return (_mask_dot_exact((col >= row).astype(BF16), ev) + _mask_dot_exact((col < row).astype(BF16), vv)
                + gv + ov,)

    return bmap("ssd_decay_grad", fn, (s // L,),
                [rows(e_incl, L), rows(v_excl, L), (g_chunk, (None, 1, LANES), lambda i: (i, 0, 0)), rows(own, L)],
                [row_out(s, LANES, F32, L)])[0]


def ssd_decay_grad_own(c_grp, b_grp, dy_h, xdt_h, ac, ar):
    s_len = dy_h.shape[1]
    L = SSD_CHUNK
    nc = s_len // L
    cpb = _pick(nc, (8, 4, 2, 1))
    rep = SSD_HEADS // SSD_GROUPS

    def grp_spec(op):
        cb = op[1] // HEAD
        return pl.BlockSpec((L * cpb, HEAD), lambda h, i: (i, cb + h // rep))

    def body(c_ref, b_ref, dy_ref, x_ref, ac_ref, ar_ref, o_ref):
        row = lax.broadcasted_iota(jnp.int32, (L, L), 0)
        col = lax.broadcasted_iota(jnp.int32, (L, L), 1)
        earlier = (row < col).astype(BF16)

        def one(c, carry):
            sl = pl.ds(pl.multiple_of(c * L, L), L)
            seg = ac_ref[c] - ar_ref[c]
            dec = jnp.where(row > col, jnp.exp(jnp.minimum(seg, 0.0)), 0.0)
            nt = (((1,), (1,)), ((), ()))
            sc = lax.dot_general(c_ref[sl, :].astype(BF16), b_ref[sl, :].astype(BF16), nt, preferred_element_type=F32)
            dx = lax.dot_general(dy_ref[sl, :].astype(BF16), x_ref[sl, :].astype(BF16), nt, preferred_element_type=F32)
            before_r = _dot_exact_mask(sc * dec * dx, earlier)
            o_ref[c] = jnp.sum(jnp.where(row >= col, before_r, 0.0), axis=0, keepdims=True)
            return carry

        lax.fori_loop(0, cpb, one, 0)

    seq = lambda a: pl.BlockSpec((None, L * cpb, a.shape[2]), lambda h, i: (h, i, 0))
    dec_spec = lambda shape: pl.BlockSpec((None, cpb) + shape, lambda h, i: (h, i, 0, 0))
    return pl.pallas_call(
        body, name="ssd_decay_grad_own", grid=(SSD_HEADS, nc // cpb),
        in_specs=[grp_spec(c_grp), grp_spec(b_grp), seq(dy_h), seq(xdt_h), dec_spec((L, 1)), dec_spec((1, L))],
        out_specs=pl.BlockSpec((None, cpb, 1, L), lambda h, i: (h, i, 0, 0)),
        out_shape=jax.ShapeDtypeStruct((SSD_HEADS, nc, 1, L), F32),
        compiler_params=_params(("parallel", "parallel")),
    )(c_grp[0], b_grp[0], dy_h, xdt_h, ac, ar)


def ssd_dt_bwd(da, ddt_x, proj, dt_bias, a_log):
    s = da.shape[0]
    tm = _pick(s, (512, 256, 128))

    def fn(dav, dxv, dtr, bias, alog):
        pre = dtr + bias
        a_coef = -jnp.exp(alog)
        ddtr = (dav * a_coef + dxv) * _sigmoid(pre)
        return (ddtr, jnp.sum(ddtr, axis=0, keepdims=True),
                jnp.sum(dav * _softplus(pre) * a_coef, axis=0, keepdims=True))

    ddtr, dbias, dalog = bmap("ssd_dt_bwd", fn, (s // tm,),
                              [rows(da, tm), rows(ddt_x, tm), rows(proj, tm, LANES, COL_DT), par(dt_bias), par(a_log)],
                              [row_out(s, LANES, BF16, tm)], [col_acc(LANES), col_acc(LANES)])
    return ddtr, dbias[0], dalog[0]


def ssd_xdt(x_h, dt_h):
    s = x_h.shape[1]
    tm = _pick(s, (512, 256, 128))
    return bmap("ssd_xdt", lambda xv, dv: (xv * dv,), (SSD_HEADS, s // tm),
                [_head_rows(x_h, tm), _head_rows(dt_h, tm)], [_head_major(s, tm, SSD_P, SSD_HEADS)])[0]


def ssd_gate_fwd(ytil, act, proj, d_full, norm_w):
    s = ytil.shape[0]
    tm = _pick(s, (256, 128))

    def fn(yt, xv, zv, dv, wv):
        u = (yt + xv * dv) * _silu(zv)
        return (u * lax.rsqrt(jnp.mean(u * u, axis=-1, keepdims=True) + NORM_EPS) * wv,)

    return bmap("ssd_gate_fwd", fn, (s // tm,),
                [rows(ytil, tm), rows(act, tm, BR_WIDTH, 0), rows(proj, tm, BR_WIDTH, COL_Z), par(d_full), par(norm_w)],
                [row_out(s, BR_WIDTH, BF16, tm)])[0]


def ssd_gate_bwd(ytil, act, proj, d_full, norm_w, dout):
    s = ytil.shape[0]
    tm = _pick(s, (256, 128))

    def fn(yt, xv, zv, dv, wv, dov):
        y = yt + xv * dv
        sz = _silu(zv)
        u = y * sz
        r = lax.rsqrt(jnp.mean(u * u, axis=-1, keepdims=True) + NORM_EPS)
        un = u * r
        g = dov * wv
        du = r * (g - un * jnp.mean(g * un, axis=-1, keepdims=True))
        dy = du * sz
        return (dy, du * y * _dsilu(zv), jnp.sum(dov * un, axis=0, keepdims=True),
                jnp.sum(dy * xv, axis=0, keepdims=True))

    dy, dz, dw, dd = bmap("ssd_gate_bwd", fn, (s // tm,),
                          [rows(ytil, tm), rows(act, tm, BR_WIDTH, 0), rows(proj, tm, BR_WIDTH, COL_Z), par(d_full),
                           par(norm_w), rows(dout, tm)],
                          [row_out(s, BR_WIDTH, F32, tm), row_out(s, BR_WIDTH, BF16, tm)],
                          [col_acc(BR_WIDTH), col_acc(BR_WIDTH)])
    return dy, dz, dw[0], dd[0]


def ssd_head_bwd(dy_h, y_cross, dxdt_h, dx_cross, xdt_h, x_h, dt_h, d_h):
    s = dy_h.shape[1]
    tm = _pick(s, (512, 256, 128))

    def fn(dy, yc, dxdt, dxc, xdt, xv, dt, dskip):
        return (dy * dskip + dxdt * dt,
                jnp.sum(dy * yc, axis=-1, keepdims=True),
                jnp.sum(dxc * xdt, axis=-1, keepdims=True),
                jnp.sum(dxdt * xv, axis=-1, keepdims=True))

    one = ((SSD_HEADS, s, 1), F32, (None, tm, 1), lambda h, i: (h, i, 0))
    return bmap("ssd_head_bwd", fn, (SSD_HEADS, s // tm),
                [_head_rows(a, tm) for a in (dy_h, y_cross, dxdt_h, dx_cross, xdt_h, x_h, dt_h)]
                + [(d_h, (None, 1, 1), lambda h, i: (h, 0, 0))],
                [_head_major(s, tm, SSD_P, SSD_HEADS), one, one, one])


def ssd_group_sum(db_h, dc_h):
    s = db_h.shape[1]
    tm = _pick(s, (512, 256, 128))
    e = SSD_HEADS // SSD_GROUPS

    def fn(*v):
        return v[0] + v[1] + v[2] + v[3], v[4] + v[5] + v[6] + v[7]

    def head(arr, j):
        return (arr, (None, tm, SSD_N), lambda g, i: (g * e + j, i, 0))

    out = ((s, SSD_GROUPS * SSD_N), F32, (tm, SSD_N), lambda g, i: (i, g))
    return bmap("ssd_group_sum", fn, (SSD_GROUPS, s // tm),
                [head(db_h, j) for j in range(e)] + [head(dc_h, j) for j in range(e)], [out, out])


def _gate_ins(proj, b_gate, tm):
    b2 = b_gate.reshape(1, 3 * D_MODEL)
    ins = [rows(proj, tm, D_MODEL, COL_GATE + j * D_MODEL) for j in range(3)]
    ins += [(b2, (1, D_MODEL), (lambda j: lambda i: (0, j))(j)) for j in range(3)]
    return ins


def merge_fwd(proj, b_gate, ps):
    s = proj.shape[0]
    tm = 128

    def fn(g0, g1, g2, b0, b1, b2, p0, p1, p2):
        return (_sigmoid(g0 + b0) * p0 + _sigmoid(g1 + b1) * p1 + _sigmoid(g2 + b2) * p2,)

    return bmap("merge_fwd", fn, (s // tm,), _gate_ins(proj, b_gate, tm) + [rows(p, tm) for p in ps],
                [row_out(s, D_MODEL, BF16, tm)])[0]


def merge_bwd(proj, b_gate, ps, dmerged):
    s = proj.shape[0]
    tm = 128

    def fn(g0, g1, g2, b0, b1, b2, p0, p1, p2, dm):
        dps, dgs = [], []
        for g, b, p in ((g0, b0, p0), (g1, b1, p1), (g2, b2, p2)):
            gate = _sigmoid(g + b)
            dps.append(dm * gate)
            dgs.append(dm * p * gate * (1.0 - gate))
        dgl = jnp.concatenate(dgs, axis=1)
        return (*dps, dgl, jnp.sum(dgl, axis=0, keepdims=True))

    res = bmap("merge_bwd", fn, (s // tm,),
               _gate_ins(proj, b_gate, tm) + [rows(p, tm) for p in ps] + [rows(dmerged, tm)],
               [row_out(s, D_MODEL, BF16, tm)] * 3 + [row_out(s, 3 * D_MODEL, BF16, tm)], [col_acc(3 * D_MODEL)])
    return res[0:3], res[3], res[4][0]


def swiglu_fwd(gu):
    s = gu.shape[0]
    tm = 128
    return bmap("swiglu_fwd", lambda g, u: (_silu(g) * u,), (s // tm,),
                [rows(gu, tm, FFN_HIDDEN, 0), rows(gu, tm, FFN_HIDDEN, FFN_HIDDEN)],
                [row_out(s, FFN_HIDDEN, BF16, tm)])[0]


def swiglu_bwd(gu, dact):
    s = gu.shape[0]
    tm = 128

    def fn(g, u, da):
        return (jnp.concatenate([da * u * _dsilu(g), da * _silu(g)], axis=1),)

    return bmap("swiglu_bwd", fn, (s // tm,),
                [rows(gu, tm, FFN_HIDDEN, 0), rows(gu, tm, FFN_HIDDEN, FFN_HIDDEN), rows(dact, tm)],
                [row_out(s, 2 * FFN_HIDDEN, BF16, tm)])[0]


def loss_head(y, target):
    s, d = y.shape
    tm = _pick(s, (256, 128))

    def fn(yv, tv):
        err = yv - tv
        part = jnp.sum(jnp.mean(err * err, axis=-1, keepdims=True), axis=0, keepdims=True)
        return err * (1.0 / d), 0.5 * part

    dy, loss = bmap("loss_head", fn, (s // tm,), [rows(y, tm), rows(target, tm)], [row_out(s, d, F32, tm)],
                    [((1, 1), (1, 1), lambda i: (0, 0))])
    return dy, loss[0, 0]


def _to_heads(a2d, heads, d):
    return a2d.reshape(a2d.shape[0], heads, d).transpose(1, 0, 2)


def _from_heads(a3d):
    return a3d.transpose(1, 0, 2).reshape(a3d.shape[1], a3d.shape[0] * a3d.shape[2])


def _pad_lanes(v):
    return jnp.pad(v.reshape(1, -1), ((0, 0), (0, LANES - v.shape[-1])))


def layer_forward(x, w, p, tabs):
    s = x.shape[0]
    cos2, sin2, ac_ret, ar_ret = tabs
    sv = {"x": x}
    h = rms_fwd(x, p["norm_mix_pre"], BF16, "rms_mix_pre")
    proj = matmul(h, w["w_in"], name="mm_in")
    sv.update(h=h, proj=proj)

    qr, kr = rotary_fwd(proj, cos2, sin2)
    v_ret = (proj, COL_RET + 2 * BR_WIDTH, N_HEADS)
    o_ret = linear_attention(qr, kr, v_ret, ac_ret, ar_ret, chunk=RET_CHUNK, causal=False, reverse=False,
                             heads=N_HEADS, name="ret_fwd")
    y_ret = gn_gate_fwd(o_ret, p["ret_gn_w"], proj)
    sv.update(qr=qr, kr=kr, o_ret=o_ret, y_ret=y_ret)

    qkv = proj[:, COL_SB:COL_SB + 3 * BR_WIDTH].astype(BF16)
    o_sb = sb_forward(qkv)
    y_sb = o_sb.astype(BF16)
    sv.update(qkv=qkv, o_sb=o_sb, y_sb=y_sb)

    xbc = proj[:, COL_XBC:COL_XBC + XBC_WIDTH]
    us = _shifted(xbc, False) + [xbc]
    conv_b = p["ssd_conv_b"].reshape(1, XBC_WIDTH)
    act = ssd_conv_fwd(us, p["ssd_conv_w"], conv_b)
    dt, acum = ssd_dt_fwd(proj, _pad_lanes(p["ssd_dt_bias"]), _pad_lanes(p["ssd_a_log"]))
    nc = s // SSD_CHUNK
    x_h = _to_heads(act[:, :BR_WIDTH], SSD_HEADS, SSD_P)
    dt_h = dt[:, :SSD_HEADS].T[:, :, None]
    acs = acum[:, :SSD_HEADS].T.reshape(SSD_HEADS, nc, SSD_CHUNK)
    ac_h, ar_h = acs[:, :, :, None], acs[:, :, None, :]
    xdt_h = ssd_xdt(x_h, dt_h)
    b_grp = (act, BR_WIDTH, SSD_GROUPS)
    c_grp = (act, BR_WIDTH + SSD_GROUPS * SSD_N, SSD_GROUPS)
    ytil_h, y_cross, s_fwd = linear_attention(c_grp, b_grp, xdt_h, ac_h, ar_h, chunk=SSD_CHUNK, causal=True,
                                              reverse=False, heads=SSD_HEADS, name="ssd_fwd", parts_out=True)
    ytil = _from_heads(ytil_h)
    d_full = jnp.repeat(p["ssd_d"], SSD_P).reshape(1, BR_WIDTH)
    norm_w = p["ssd_norm_w"].reshape(1, BR_WIDTH)
    y_ssd = ssd_gate_fwd(ytil, act, proj, d_full, norm_w)
    sv.update(act=act, x_h=x_h, dt_h=dt_h, ac_h=ac_h, ar_h=ar_h, xdt_h=xdt_h, y_cross=y_cross, s_fwd=s_fwd, ytil=ytil,
              y_ssd=y_ssd)

    ys = (y_ret, y_sb, y_ssd)
    ps = [matmul(y, wb, name="mm_branch") for y, wb in zip(ys, w["w_branch"])]
    merged = merge_fwd(proj, p["b_gate"], ps)
    mo = matmul(merged, w["w_out"], name="mm_out")
    x1 = rms_fwd(mo, p["norm_mix_post"], F32, "rms_mix_post", resid=x)
    sv.update(ps=ps, merged=merged, mo=mo, x1=x1)

    h2 = rms_fwd(x1, p["norm_ffn_pre"], BF16, "rms_ffn_pre")
    gu = matmul(h2, w["w_gu"], name="mm_gu")
    actf = swiglu_fwd(gu)
    f = matmul(actf, w["w_down"], name="mm_down")
    x2 = rms_fwd(f, p["norm_ffn_post"], F32, "rms_ffn_post", resid=x1)
    sv.update(h2=h2, gu=gu, actf=actf, f=f)
    return x2, sv


def layer_backward(dx2, sv, w, p, tabs):
    cos2, sin2, ac_ret, ar_ret = tabs
    proj = sv["proj"]
    s = proj.shape[0]
    gs = {}

    df, gs["norm_ffn_post"] = rms_bwd(sv["f"], p["norm_ffn_post"], dx2, "rms_ffn_post_bwd", out_dtype=BF16)
    dactf = matmul(df, w["w_down"], nt=True, name="mm_down_dx")
    gw_down = matmul(sv["actf"].T, df, out_dtype=BF16, name="mm_down_dw")
    dgu = swiglu_bwd(sv["gu"], dactf)
    dh2 = matmul(dgu, w["w_gu"], nt=True, name="mm_gu_dx")
    gw_gu = matmul(sv["h2"].T, dgu, out_dtype=BF16, name="mm_gu_dw")
    dx1, gs["norm_ffn_pre"] = rms_bwd(sv["x1"], p["norm_ffn_pre"], dh2, "rms_ffn_pre_bwd", resid=dx2)

    dmo, gs["norm_mix_post"] = rms_bwd(sv["mo"], p["norm_mix_post"], dx1, "rms_mix_post_bwd", out_dtype=BF16)
    dmerged = matmul(dmo, w["w_out"], nt=True, name="mm_out_dx")
    gw_out = matmul(sv["merged"].T, dmo, out_dtype=BF16, name="mm_out_dw")
    dps, dgl, gs["b_gate"] = merge_bwd(proj, p["b_gate"], sv["ps"], dmerged)
    ys = (sv["y_ret"], sv["y_sb"], sv["y_ssd"])
    dys = [matmul(dp, wb, nt=True, name="mm_branch_dx") for dp, wb in zip(dps, w["w_branch"])]
    gw_branch = [matmul(y.T, dp, out_dtype=BF16, name="mm_branch_dw") for y, dp in zip(ys, dps)]

    do_h, dg, gs["ret_gn_w"] = gn_gate_bwd(sv["o_ret"], p["ret_gn_w"], proj, dys[0])
    v_ret = (proj, COL_RET + 2 * BR_WIDTH, N_HEADS)
    ret = dict(chunk=RET_CHUNK, causal=False, heads=N_HEADS)
    dqr = linear_attention(do_h, v_ret, sv["kr"], ac_ret, ar_ret, reverse=False, name="ret_dq", **ret)
    dkr = linear_attention(v_ret, do_h, sv["qr"], ac_ret, ar_ret, reverse=True, name="ret_dk", **ret)
    dv = linear_attention(sv["kr"], sv["qr"], do_h, ac_ret, ar_ret, reverse=True, name="ret_dv",
                          out_dtype=BF16, out_2d=True, **ret)
    dq, dk = rotary_bwd(dqr, dkr, cos2, sin2)

    dsq, dsk, dsv = sb_backward(sv["qkv"], sv["o_sb"], dys[1])

    act = sv["act"]
    d_full = jnp.repeat(p["ssd_d"], SSD_P).reshape(1, BR_WIDTH)
    norm_w = p["ssd_norm_w"].reshape(1, BR_WIDTH)
    dy, dz, gs["ssd_norm_w"], dd_lanes = ssd_gate_bwd(sv["ytil"], act, proj, d_full, norm_w, dys[2])
    gs["ssd_d"] = dd_lanes.reshape(SSD_HEADS, SSD_P).sum(axis=1)
    dy_h = _to_heads(dy, SSD_HEADS, SSD_P)
    b_grp = (act, BR_WIDTH, SSD_GROUPS)
    c_grp = (act, BR_WIDTH + SSD_GROUPS * SSD_N, SSD_GROUPS)
    ssd = dict(chunk=SSD_CHUNK, causal=True, heads=SSD_HEADS)
    ac_h, ar_h, xdt_h = sv["ac_h"], sv["ar_h"], sv["xdt_h"]
    dxdt_h, dx_cross, s_rev = linear_attention(b_grp, c_grp, dy_h, ac_h, ar_h, reverse=True, name="ssd_dx",
                                               parts_out=True, **ssd)
    dc_h = linear_attention(dy_h, xdt_h, b_grp, ac_h, ar_h, reverse=False, name="ssd_dc", **ssd)
    db_h = linear_attention(xdt_h, dy_h, c_grp, ac_h, ar_h, reverse=True, name="ssd_db", **ssd)
    d_h = p["ssd_d"].reshape(SSD_HEADS, 1, 1)
    dx_h, e_h, v_h, ddtx_h = ssd_head_bwd(dy_h, sv["y_cross"], dxdt_h, dx_cross, xdt_h, sv["x_h"], sv["dt_h"], d_h)
    lanes = lambda a: jnp.pad(a[:, :, 0].T, ((0, 0), (0, LANES - SSD_HEADS)))
    g_chunk = ssd_state_dot(sv["s_fwd"], s_rev, ac_h[:, :, SSD_CHUNK - 1:, :])
    own = ssd_decay_grad_own(c_grp, b_grp, dy_h, xdt_h, ac_h, ar_h).reshape(SSD_HEADS, s, 1)
    da = ssd_decay_grad(lanes(e_h), lanes(v_h), lanes(g_chunk[:, :, :, 0])[:, None, :], lanes(own))
    dt_bias, a_log = _pad_lanes(p["ssd_dt_bias"]), _pad_lanes(p["ssd_a_log"])
    ddtr, dbias, dalog = ssd_dt_bwd(da, lanes(ddtx_h), proj, dt_bias, a_log)
    gs["ssd_dt_bias"], gs["ssd_a_log"] = dbias[:SSD_HEADS], dalog[:SSD_HEADS]
    db2, dc2 = ssd_group_sum(db_h, dc_h)
    dact = jnp.concatenate([_from_heads(dx_h), db2, dc2], axis=1)
    xbc = proj[:, COL_XBC:COL_XBC + XBC_WIDTH]
    us = _shifted(xbc, False) + [xbc]
    conv_b = p["ssd_conv_b"].reshape(1, XBC_WIDTH)
    dcv, gs["ssd_conv_w"], gs["ssd_conv_b"] = ssd_conv_bwd_a(us, p["ssd_conv_w"], conv_b, dact)
    dxbc = ssd_conv_bwd_b(_shifted(dcv, True) + [dcv], p["ssd_conv_w"])

    dt_cols = jnp.pad(ddtr, ((0, 0), (0, IN_PAD - COL_DT - LANES)))
    dproj = jnp.concatenate([dq, dk, dv, dg, dsq.astype(BF16), dsk.astype(BF16), dsv.astype(BF16),
                             dz, dxbc, dgl, dt_cols], axis=1)
    dh = matmul(dproj, w["w_in"], nt=True, name="mm_in_dx")
    gw_in = matmul(sv["h"].T, dproj, out_dtype=BF16, name="mm_in_dw")
    dx, gs["norm_mix_pre"] = rms_bwd(sv["x"], p["norm_mix_pre"], dh, "rms_mix_pre_bwd", resid=dx1)
    gw = dict(w_in=gw_in, w_branch=gw_branch, w_out=gw_out, w_gu=gw_gu, w_down=gw_down)
    return dx, gw, gs


def in_cols_pad(w):
    zeros = jnp.zeros(w.shape[:-1] + (IN_PAD - IN_COLS,), w.dtype)
    return jnp.concatenate([w[..., :COL_GATE], w[..., COL_GATE + SSD_HEADS:], w[..., COL_GATE:COL_GATE + SSD_HEADS], zeros],
                           axis=-1)


def in_cols_unpad(g):
    return jnp.concatenate([g[..., :COL_GATE], g[..., COL_DT:COL_DT + SSD_HEADS], g[..., COL_GATE:COL_DT]], axis=-1)


W_IN_SHARD = (D_MODEL, IN_COLS // N_SHARD)
BIG = (("w_branch_ret", (BR_WIDTH, D_MODEL // N_SHARD), True),
       ("w_branch_sb", (BR_WIDTH, D_MODEL // N_SHARD), True),
       ("w_branch_ssd", (BR_WIDTH, D_MODEL // N_SHARD), True),
       ("w_out", (D_MODEL // N_SHARD, D_MODEL), False),
       ("ffn_w_gate", (D_MODEL, FFN_HIDDEN // N_SHARD), True),
       ("ffn_w_up", (D_MODEL, FFN_HIDDEN // N_SHARD), True),
       ("ffn_w_down", (FFN_HIDDEN // N_SHARD, D_MODEL), False))
FLAT_ROWS = sum(r * c for _, (r, c), _ in BIG) // LANES
assert FLAT_ROWS * LANES == sum(r * c for _, (r, c), _ in BIG) and FLAT_ROWS % 32 == 0
HALF_ROWS = FLAT_ROWS // 2


def pack_shard(layer_shards, dtype):
    flat = jnp.concatenate([layer_shards[n].reshape(-1).astype(dtype) for n, _, _ in BIG])
    return flat.reshape(FLAT_ROWS, LANES)


def unpack_shard(flat):
    out, off = {}, 0
    v = flat.reshape(-1)
    for n, (r, c), _ in BIG:
        out[n] = v[off:off + r * c].reshape(r, c)
        off += r * c
    return out


def unpack_gathered(full):
    v = full.reshape(N_SHARD, -1)
    out, off = {}, 0
    for n, (r, c), by_cols in BIG:
        seg = v[:, off:off + r * c].reshape(N_SHARD, r, c)
        out[n] = seg.transpose(1, 0, 2).reshape(r, N_SHARD * c) if by_cols else seg.reshape(N_SHARD * r, c)
        off += r * c
    return out


def pack_for_scatter(whole):
    segs = []
    for n, (r, c), by_cols in BIG:
        g = whole[n].astype(BF16)
        g = g.reshape(r, N_SHARD, c).transpose(1, 0, 2) if by_cols else g.reshape(N_SHARD, r, c)
        segs.append(g.reshape(N_SHARD, r * c))
    return jnp.concatenate(segs, axis=1).reshape(N_SHARD, FLAT_ROWS, LANES)


ANY = pl.BlockSpec(memory_space=pl.ANY)


def _place():
    return lax.axis_index("x"), lax.axis_index("y"), lax.axis_index("c")


def _other_chips(x, y):
    return [(1 - x, y), (x, 1 - y), (1 - x, 1 - y)]


def _half(c, rows):
    return pl.ds(pl.multiple_of(c * (rows // 2), 16), rows // 2)


def _sems(n):
    return [pltpu.SemaphoreType.DMA((n,)), pltpu.SemaphoreType.DMA((n,))]


def all_gather_shards(shards):
    n = len(shards)

    def body(*refs):
        x_refs, o_refs = refs[:n], refs[n:2 * n]
        send_sems, recv_sems, local_sems = refs[2 * n:]
        x, y, c = _place()
        j = 2 * x + y
        sibling = (x, y, 1 - c)
        chips = _other_chips(x, y)

        def copy(k, src, dst, to):
            return pltpu.make_async_remote_copy(src_ref=src, dst_ref=dst, send_sem=send_sems.at[k],
                                                recv_sem=recv_sems.at[k], device_id=to, device_id_type=MESH)

        started, own = [], []
        for a, (x_ref, o_ref) in enumerate(zip(x_refs, o_refs)):
            local = pltpu.make_async_copy(x_ref, o_ref.at[j], local_sems.at[a])
            local.start()
            own.append(local)
            mine = _half(c, x_ref.shape[0])
            for k, (cx, cy) in enumerate(chips):
                cp = copy(6 * a + k, x_ref.at[mine], o_ref.at[j, mine], (cx, cy, c))
                cp.start()
                started.append(cp)
        for a, o_ref in enumerate(o_refs):
            mine = _half(c, o_ref.shape[1])
            for k, (cx, cy) in enumerate(chips):
                landed = o_ref.at[2 * cx + cy, mine]
                copy(6 * a + k, landed, landed, (cx, cy, c)).wait_recv()
                cp = copy(6 * a + 3 + k, landed, landed, sibling)
                cp.start()
                started.append(cp)
        for a, o_ref in enumerate(o_refs):
            other = _half(1 - c, o_ref.shape[1])
            for k, (cx, cy) in enumerate(chips):
                landed = o_ref.at[2 * cx + cy, other]
                copy(6 * a + 3 + k, landed, landed, sibling).wait_recv()
        for cp in started:
            cp.wait_send()
        for local in own:
            local.wait()

    return pl.pallas_call(
        body, name="all_gather_shards", in_specs=[ANY] * n, out_specs=[ANY] * n,
        out_shape=[jax.ShapeDtypeStruct((N_SHARD,) + s.shape, s.dtype) for s in shards],
        scratch_shapes=_sems(6 * n) + [pltpu.SemaphoreType.DMA((n,))],
    )(*shards)


def exchange_siblings(gs):
    n = len(gs)

    def body(*refs):
        g_refs, o_refs = refs[:n], refs[n:2 * n]
        send_sems, recv_sems = refs[2 * n:]
        x, y, c = _place()
        copies = []
        for a, (g_ref, o_ref) in enumerate(zip(g_refs, o_refs)):
            cp = pltpu.make_async_remote_copy(src_ref=g_ref.at[:, _half(1 - c, g_ref.shape[1])], dst_ref=o_ref,
                                              send_sem=send_sems.at[a], recv_sem=recv_sems.at[a],
                                              device_id=(x, y, 1 - c), device_id_type=MESH)
            cp.start()
            copies.append(cp)
        for cp in copies:
            cp.wait()

    return pl.pallas_call(
        body, name="exchange_siblings", in_specs=[ANY] * n, out_specs=[ANY] * n,
        out_shape=[jax.ShapeDtypeStruct((N_SHARD, g.shape[1] // 2, g.shape[2]), g.dtype) for g in gs],
        scratch_shapes=_sems(n),
    )(*gs)


def exchange_chips(ps):
    n = len(ps)

    def body(*refs):
        p_refs, o_refs = refs[:n], refs[n:2 * n]
        send_sems, recv_sems, local_sems = refs[2 * n:]
        x, y, c = _place()
        j = 2 * x + y
        copies = []
        for a, (p_ref, o_ref) in enumerate(zip(p_refs, o_refs)):
            local = pltpu.make_async_copy(p_ref.at[j], o_ref.at[j], local_sems.at[a])
            local.start()
            copies.append(local)
            for k, (cx, cy) in enumerate(_other_chips(x, y)):
                cp = pltpu.make_async_remote_copy(src_ref=p_ref.at[2 * cx + cy], dst_ref=o_ref.at[j],
                                                  send_sem=send_sems.at[3 * a + k], recv_sem=recv_sems.at[3 * a + k],
                                                  device_id=(cx, cy, c), device_id_type=MESH)
                cp.start()
                copies.append(cp)
        for cp in copies:
            cp.wait()

    return pl.pallas_call(
        body, name="exchange_chips", in_specs=[ANY] * n, out_specs=[ANY] * n,
        out_shape=[jax.ShapeDtypeStruct(p.shape, p.dtype) for p in ps],
        scratch_shapes=_sems(3 * n) + [pltpu.SemaphoreType.DMA((n,))],
    )(*ps)


def swap_halves(reds):
    n = len(reds)

    def body(*refs):
        r_refs, o_refs = refs[:n], refs[n:2 * n]
        send_sems, recv_sems = refs[2 * n:]
        x, y, c = _place()
        copies = []
        for a, (r_ref, o_ref) in enumerate(zip(r_refs, o_refs)):
            cp = pltpu.make_async_remote_copy(src_ref=r_ref, dst_ref=o_ref, send_sem=send_sems.at[a],
                                              recv_sem=recv_sems.at[a], device_id=(x, y, 1 - c), device_id_type=MESH)
            cp.start()
            copies.append(cp)
        for cp in copies:
            cp.wait()

    return pl.pallas_call(
        body, name="swap_halves", in_specs=[ANY] * n, out_specs=[ANY] * n,
        out_shape=[jax.ShapeDtypeStruct(r.shape, r.dtype) for r in reds],
        scratch_shapes=_sems(n),
    )(*reds)


def _sum_rows(rows, cols):
    best = 16
    for t in range(16, rows + 1, 16):
        if rows % t == 0 and t * cols * 4 <= 1536 * 1024:
            best = t
    return best


def reduce_scatter(gs):
    c = lax.axis_index("c")
    gots = exchange_siblings(gs)
    pairs = []
    for g, got in zip(gs, gots):
        half, cols = got.shape[1], got.shape[2]
        t = _sum_rows(half, cols)
        own = lax.dynamic_slice_in_dim(g, c * half, half, axis=1)
        blk = lambda a: (a, (None, t, cols), lambda j, i: (j, i, 0))
        pairs.append(bmap("sum_siblings", lambda a, b: (a.astype(F32) + b.astype(F32),), (N_SHARD, half // t),
                          [blk(own), blk(got)], [(got.shape, BF16, (None, t, cols), lambda j, i: (j, i, 0))])[0])
    reds = []
    for parts in exchange_chips(pairs):
        half, cols = parts.shape[1], parts.shape[2]
        t = _sum_rows(half, cols)
        part = lambda j: (parts, (None, t, cols), lambda i: (j, i, 0))
        reds.append(bmap("sum_chips",
                         lambda a, b, c_, d: (((a.astype(F32) + b.astype(F32)) + c_.astype(F32)) + d.astype(F32),),
                         (half // t,), [part(j) for j in range(N_SHARD)],
                         [((half, cols), F32, (t, cols), lambda i: (i, 0))])[0])
    out = []
    for mine, theirs in zip(reds, swap_halves(reds)):
        both = jnp.where(c == 0, jnp.stack([mine, theirs]), jnp.stack([theirs, mine]))
        out.append(both.reshape(2 * mine.shape[0], mine.shape[1]))
    return out


def all_reduce_small(buf):
    r = buf.shape[0]

    def body(x_ref, o_ref, land_ref, send_sems, recv_sems):
        x, y, c = _place()
        me = 4 * x + 2 * y + c
        land_ref[me] = x_ref[...]
        copies = []
        for d in range(1, 8):
            peer = tuple(1 - v if (d >> s) & 1 else v for v, s in ((x, 2), (y, 1), (c, 0)))
            cp = pltpu.make_async_remote_copy(src_ref=x_ref, dst_ref=land_ref.at[me], send_sem=send_sems.at[d - 1],
                                              recv_sem=recv_sems.at[d - 1], device_id=peer, device_id_type=MESH)
            cp.start()
            copies.append(cp)
        for cp in copies:
            cp.wait()
        acc = land_ref[0]
        for s in range(1, 8):
            acc = acc + land_ref[s]
        o_ref[...] = acc

    vmem = pl.BlockSpec(memory_space=pltpu.VMEM)
    return pl.pallas_call(
        body, name="all_reduce_small", in_specs=[vmem], out_specs=vmem,
        out_shape=jax.ShapeDtypeStruct((r, LANES), F32),
        scratch_shapes=[pltpu.VMEM((8, r, LANES), F32), pltpu.SemaphoreType.DMA((7,)), pltpu.SemaphoreType.DMA((7,))],
    )(buf)


def adamw(w, g, m, v, name):
    r, c = w.shape
    tm = r
    for cand in (512, 256, 128, 64, 32, 16, 8):
        if r % cand == 0 and cand * c * 4 * 14 <= 24 * 1024 * 1024:
            tm = cand
            break
    bc1 = 1.0 - ADAM_B1 ** ADAM_STEP
    bc2 = 1.0 - ADAM_B2 ** ADAM_STEP

    def fn(wv, gv, mv, vv):
        m2 = ADAM_B1 * mv + (1.0 - ADAM_B1) * gv
        v2 = ADAM_B2 * vv + (1.0 - ADAM_B2) * (gv * gv)
        delta = -ADAM_LR * ((m2 / bc1) / (jnp.sqrt(v2 / bc2) + ADAM_EPS) + ADAM_WD * wv)
        return delta, m2, v2

    return bmap(name, fn, (r // tm,), [rows(a, tm) for a in (w, g, m, v)], [row_out(r, c, F32, tm)] * 3)


WEIGHTS = ("norm_mix_pre", "norm_mix_post", "norm_ffn_pre", "norm_ffn_post", "w_in", "b_gate", "ret_gn_w", "ssd_conv_w",
           "ssd_conv_b", "ssd_dt_bias", "ssd_a_log", "ssd_d", "ssd_norm_w", "w_branch_ret", "w_branch_sb", "w_branch_ssd",
           "w_out", "ffn_w_gate", "ffn_w_up", "ffn_w_down")
BIG_NAMES = tuple(n for n, _, _ in BIG)
SHARDED = ("w_in",) + BIG_NAMES
SMALL = tuple(n for n in WEIGHTS if n not in SHARDED and n != "ssd_conv_w")
CONV_SHARD = XBC_WIDTH // N_SHARD


def _pack_small(parts):
    flat = jnp.concatenate([p.reshape(-1).astype(F32) for p in parts])
    rows_ = -(-flat.shape[0] // (8 * LANES)) * 8
    return jnp.pad(flat, (0, rows_ * LANES - flat.shape[0])).reshape(rows_, LANES)


def _unpack_small(buf, shapes):
    v, out, off = buf.reshape(-1), [], 0
    for shp in shapes:
        n = int(np.prod(shp))
        out.append(v[off:off + n].reshape(shp))
        off += n
    return out


def kernel(x, positions, norm_mix_pre, norm_mix_post, norm_ffn_pre, norm_ffn_post, w_in, b_gate, ret_gn_w, ssd_conv_w, ssd_conv_b, ssd_dt_bias, ssd_a_log, ssd_d, ssd_norm_w, w_branch_ret, w_branch_sb, w_branch_ssd, w_out, ffn_w_gate, ffn_w_up, ffn_w_down, loss_target, m_norm_mix_pre, m_norm_mix_post, m_norm_ffn_pre, m_norm_ffn_post, m_w_in, m_b_gate, m_ret_gn_w, m_ssd_conv_w, m_ssd_conv_b, m_ssd_dt_bias, m_ssd_a_log, m_ssd_d, m_ssd_norm_w, m_w_branch_ret, m_w_branch_sb, m_w_branch_ssd, m_w_out, m_ffn_w_gate, m_ffn_w_up, m_ffn_w_down, v_norm_mix_pre, v_norm_mix_post, v_norm_ffn_pre, v_norm_ffn_post, v_w_in, v_b_gate, v_ret_gn_w, v_ssd_conv_w, v_ssd_conv_b, v_ssd_dt_bias, v_ssd_a_log, v_ssd_d, v_ssd_norm_w, v_w_branch_ret, v_w_branch_sb, v_w_branch_ssd, v_w_out, v_ffn_w_gate, v_ffn_w_up, v_ffn_w_down):
    given = dict(locals())
    wts = {n: given[n] for n in WEIGHTS}
    mom = {n: given["m_" + n] for n in WEIGHTS}
    var = {n: given["v_" + n] for n in WEIGHTS}
    xi, yi, ci = lax.axis_index("x"), lax.axis_index("y"), lax.axis_index("c")
    shard_id = 2 * xi + yi

    conv_mine = jnp.where(ci == 0, ssd_conv_w, 0.0)
    conv_slots = lax.dynamic_update_slice_in_dim(jnp.zeros((DEPTH, SSD_CONV, XBC_WIDTH), F32), conv_mine,
                                                 shard_id * CONV_SHARD, axis=2)
    conv_buf = _pack_small([conv_slots])
    conv_whole = _unpack_small(all_reduce_small(conv_buf), [(DEPTH, SSD_CONV, XBC_WIDTH)])[0]

    tabs = rope_tables(positions[0]) + ret_decay_tables()
    layer_w, layer_p = [], []
    for l in range(DEPTH):
        flat, w_in_all = all_gather_shards([pack_shard({n: wts[n][l] for n in BIG_NAMES}, BF16), w_in[l].astype(BF16)])
        full = unpack_gathered(flat)
        layer_w.append(dict(
            w_in=in_cols_pad(w_in_all.transpose(1, 0, 2).reshape(D_MODEL, IN_COLS)),
            w_branch=[full["w_branch_ret"], full["w_branch_sb"], full["w_branch_ssd"]],
            w_out=full["w_out"],
            w_gu=jnp.concatenate([full["ffn_w_gate"], full["ffn_w_up"]], axis=1),
            w_down=full["ffn_w_down"]))
        p = {n: wts[n][l] for n in SMALL}
        p["ssd_conv_w"] = conv_whole[l]
        layer_p.append(p)

    act = x[0]
    saved = []
    for l in range(DEPTH):
        act, sv = layer_forward(act, layer_w[l], layer_p[l], tabs)
        saved.append(sv)
    dact, loss_part = loss_head(act, loss_target[0])

    big_grads = [None] * DEPTH
    small_grads = [None] * DEPTH
    for l in reversed(range(DEPTH)):
        dact, gw, small_grads[l] = layer_backward(dact, saved[l], layer_w[l], layer_p[l], tabs)
        gu = gw["w_gu"]
        whole = dict(w_branch_ret=gw["w_branch"][0], w_branch_sb=gw["w_branch"][1],
                     w_branch_ssd=gw["w_branch"][2], w_out=gw["w_out"], ffn_w_gate=gu[:, :FFN_HIDDEN],
                     ffn_w_up=gu[:, FFN_HIDDEN:], ffn_w_down=gw["w_down"])
        g_in = in_cols_unpad(gw["w_in"]).reshape(D_MODEL, N_SHARD, IN_COLS // N_SHARD).transpose(1, 0, 2)
        red_flat, red_in = reduce_scatter([pack_for_scatter(whole), g_in])
        big_grads[l] = unpack_shard(red_flat)
        big_grads[l]["w_in"] = red_in

    small_names = SMALL + ("ssd_conv_w",)
    small_parts = [jnp.stack([small_grads[l][n] for l in range(DEPTH)]) for n in small_names]
    small_shapes = [p.shape for p in small_parts]
    summed = all_reduce_small(_pack_small([loss_part.reshape(1)] + small_parts))
    loss, *small_sum = _unpack_small(summed, [(1,)] + small_shapes)
    grads = dict(zip(small_names, small_sum))
    grads["ssd_conv_w"] = lax.dynamic_slice_in_dim(grads["ssd_conv_w"], shard_id * CONV_SHARD, CONV_SHARD, axis=2)
    for n in SHARDED:
        grads[n] = jnp.stack([big_grads[l][n] for l in range(DEPTH)])

    delta, new_m, new_v = {}, {}, {}
    for n in WEIGHTS:
        shape = wts[n].shape
        two_d = (-1, shape[-1]) if len(shape) == 3 else (8, -1) if shape[-1] * shape[0] % (8 * LANES) == 0 else shape
        args = [a.reshape(two_d) for a in (wts[n], grads[n], mom[n], var[n])]
        d, m2, v2 = adamw(*args, name="adamw_" + n)
        delta[n], new_m[n], new_v[n] = d.reshape(shape), m2.reshape(shape), v2.reshape(shape)

    return (loss.reshape(()), dact[None], *[grads[n] for n in WEIGHTS], *[delta[n] for n in WEIGHTS],
            *[new_m[n] for n in WEIGHTS], *[new_v[n] for n in WEIGHTS])
```

```python
import functools
import math

import numpy as np
import jax
import jax.numpy as jnp
from jax import lax
from jax.experimental import pallas as pl
from jax.experimental.pallas import tpu as pltpu

F32 = jnp.float32
BF16 = jnp.bfloat16
MESH = pl.DeviceIdType.MESH

D_MODEL = 2048
DEPTH = 4
RET_CHUNK = 64
NORM_EPS = 1e-6
HEAD = 128
N_HEADS = 8
BR_WIDTH = 1024
SSD_P = 64
SSD_HEADS = 16
SSD_GROUPS = 4
SSD_N = 128
SSD_CHUNK = 256
SSD_CONV = 4
FFN_HIDDEN = 5632
IN_COLS = 16400
IN_PAD = 16896
COL_RET, COL_SB, COL_Z, COL_XBC, COL_GATE, COL_DT = 0, 4096, 7168, 8192, 10240, 16384
N_SHARD = 4
LANES = 128
VMEM_LIMIT = 48 * 1024 * 1024

ADAM_LR, ADAM_B1, ADAM_B2, ADAM_EPS, ADAM_WD, ADAM_STEP = 0.001, 0.9, 0.999, 1e-08, 0.01, 10


def _pick(n, prefs):
    for p in prefs:
        if n % p == 0:
            return p
    return n


def _params(sem=None):
    kw = dict(vmem_limit_bytes=VMEM_LIMIT)
    if sem is not None:
        kw["dimension_semantics"] = sem
    return pltpu.CompilerParams(**kw)


MATMUL_VMEM = 36 * 1024 * 1024


def _matmul_tiles(m, n, k, out_bytes):
    def options(dim, prefs):
        opts = [p for p in prefs if dim % p == 0]
        return opts or [dim]

    best, best_vol = None, -1
    for tm in options(m, (1408, 1024, 512, 256, 128)):
        for tn in options(n, (1536, 1408, 1024, 512, 256, 128)):
            for tk in options(k, (2048, 1536, 1408, 1024, 512, 256, 128)):
                need = 2 * 2 * (tm * tk + tk * tn) + tm * tn * (4 + 2 * out_bytes)
                vol = tm * tn * tk
                if need <= MATMUL_VMEM and vol > best_vol:
                    best, best_vol = (tm, tn, tk), vol
    return best


def matmul(a, b, *, nt=False, out_dtype=F32, name="matmul"):
    m, k = a.shape
    n = b.shape[0] if nt else b.shape[1]
    tm, tn, tk = _matmul_tiles(m, n, k, jnp.dtype(out_dtype).itemsize)
    nk = k // tk

    def body(a_ref, b_ref, o_ref, acc_ref):
        kk = pl.program_id(2)
        if nt:
            part = lax.dot_general(a_ref[...], b_ref[...], (((1,), (1,)), ((), ())), preferred_element_type=F32)
        else:
            part = jnp.dot(a_ref[...], b_ref[...], preferred_element_type=F32)

        @pl.when(kk == 0)
        def _():
            acc_ref[...] = part

        @pl.when(kk > 0)
        def _():
            acc_ref[...] += part

        @pl.when(kk == nk - 1)
        def _():
            o_ref[...] = acc_ref[...].astype(o_ref.dtype)

    b_spec = pl.BlockSpec((tn, tk), lambda i, j, kk: (j, kk)) if nt else pl.BlockSpec((tk, tn), lambda i, j, kk: (kk, j))
    return pl.pallas_call(
        body, name=name, grid=(m // tm, n // tn, nk),
        in_specs=[pl.BlockSpec((tm, tk), lambda i, j, kk: (i, kk)), b_spec],
        out_specs=pl.BlockSpec((tm, tn), lambda i, j, kk: (i, j)),
        out_shape=jax.ShapeDtypeStruct((m, n), out_dtype),
        scratch_shapes=[pltpu.VMEM((tm, tn), F32)],
        compiler_params=_params(("parallel", "parallel", "arbitrary")),
    )(a, b)


def bmap(name, fn, grid, ins, outs, accs=()):
    n_in, n_out = len(ins), len(outs)
    last = len(grid) - 1

    def store_acc(ref, val, first):
        @pl.when(first)
        def _():
            ref[...] = val

        @pl.when(jnp.logical_not(first))
        def _():
            ref[...] += val

    def body(*refs):
        vals = fn(*[r[...] for r in refs[:n_in]])
        if not isinstance(vals, (tuple, list)):
            vals = (vals,)
        for r, v in zip(refs[n_in:n_in + n_out], vals[:n_out]):
            r[...] = v.astype(r.dtype)
        first = pl.program_id(last) == 0
        for r, v in zip(refs[n_in + n_out:], vals[n_out:]):
            store_acc(r, v.astype(F32), first)

    sem = ("parallel",) * last + (("arbitrary",) if accs else ("parallel",))
    res = pl.pallas_call(
        body, name=name, grid=grid,
        in_specs=[pl.BlockSpec(bs, im) for (_, bs, im) in ins],
        out_specs=[pl.BlockSpec(bs, im) for (_, _, bs, im) in outs] + [pl.BlockSpec(bs, im) for (_, bs, im) in accs],
        out_shape=[jax.ShapeDtypeStruct(s, dt) for (s, dt, _, _) in outs] + [jax.ShapeDtypeStruct(s, F32) for (s, _, _) in accs],
        compiler_params=_params(sem),
    )(*[a for (a, _, _) in ins])
    return res


def rows(a, tm, width=None, col=0):
    width = a.shape[1] if width is None else width
    assert col % width == 0
    cb = col // width
    return (a, (tm, width), lambda i: (i, cb))


def par(a):
    return (a, a.shape, lambda *_: (0,) * a.ndim)


def row_out(s, width, dtype, tm):
    return ((s, width), dtype, (tm, width), lambda i: (i, 0))


def col_acc(width):
    return ((1, width), (1, width), lambda i: (0, 0))


def _sigmoid(x):
    return 1.0 / (1.0 + jnp.exp(-x))


def _silu(x):
    return x * _sigmoid(x)


def _dsilu(x):
    s = _sigmoid(x)
    return s * (1.0 + x * (1.0 - s))


def _softplus(x):
    return jnp.maximum(x, 0.0) + jnp.log(1.0 + jnp.exp(-jnp.abs(x)))


def _split_bf16(x):
    hi = x.astype(BF16)
    lo = (x - hi.astype(F32)).astype(BF16)
    return hi, lo


def _dot_exact_mask(x, mask_bf16):
    hi, lo = _split_bf16(x)
    return (jnp.dot(hi, mask_bf16, preferred_element_type=F32)
            + jnp.dot(lo, mask_bf16, preferred_element_type=F32))


def rms_fwd(x, w, out_dtype, name, resid=None):
    s, d = x.shape
    tm = _pick(s, (256, 128))

    def fn(xv, wv, *rest):
        r = lax.rsqrt(jnp.mean(xv * xv, axis=-1, keepdims=True) + NORM_EPS)
        y = xv * r * wv
        if rest:
            y = y + rest[0]
        return (y,)

    ins = [rows(x, tm), par(w.reshape(1, d))]
    if resid is not None:
        ins.append(rows(resid, tm))
    return bmap(name, fn, (s // tm,), ins, [row_out(s, d, out_dtype, tm)])[0]


def rms_bwd(x, w, dy, name, resid=None, out_dtype=F32):
    s, d = x.shape
    tm = _pick(s, (256, 128))

    def fn(xv, wv, dyv, *rest):
        dyv = dyv.astype(F32)
        r = lax.rsqrt(jnp.mean(xv * xv, axis=-1, keepdims=True) + NORM_EPS)
        xn = xv * r
        g = dyv * wv
        dx = r * (g - xn * jnp.mean(g * xn, axis=-1, keepdims=True))
        if rest:
            dx = dx + rest[0]
        return dx, jnp.sum(dyv * xn, axis=0, keepdims=True)

    ins = [rows(x, tm), par(w.reshape(1, d)), rows(dy, tm)]
    if resid is not None:
        ins.append(rows(resid, tm))
    dx, dw = bmap(name, fn, (s // tm,), ins, [row_out(s, d, out_dtype, tm)], [col_acc(d)])
    return dx, dw[0]


def linear_attention(q, k, v, ac, ar, *, chunk, causal, reverse, heads, name, out_dtype=F32, out_2d=False,
                     parts_out=False, sub_chunk=None):
    def dims(op):
        return (op[0].shape[0], HEAD) if isinstance(op, tuple) else (op.shape[1], op.shape[2])

    s_len, dk = dims(q)
    dv = dims(v)[1]
    L = chunk
    cpb = _pick(s_len // L, (8, 4, 2, 1))
    nb = s_len // (L * cpb)
    a_chunks = ac.shape[1]

    def blk(i):
        return nb - 1 - i if reverse else i

    def seq_spec(op):
        if isinstance(op, tuple):
            _, first_col, nh = op
            rep, cb = heads // nh, first_col // HEAD
            return pl.BlockSpec((L * cpb, HEAD), lambda h, i: (blk(i), cb + h // rep))
        rep = heads // op.shape[0]
        return pl.BlockSpec((None, L * cpb, op.shape[2]), lambda h, i: (h // rep, blk(i), 0))

    def a_spec(arr, shape):
        rep = heads // arr.shape[0]
        if a_chunks == 1:
            return pl.BlockSpec((None, 1) + shape, lambda h, i: (h // rep, 0, 0, 0))
        return pl.BlockSpec((None, cpb) + shape, lambda h, i: (h // rep, blk(i), 0, 0))

    def body(q_ref, k_ref, v_ref, ac_ref, ar_ref, *rest):
        o_refs, state_ref = rest[:-1], rest[-1]

        @pl.when(pl.program_id(1) == 0)
        def _():
            state_ref[...] = jnp.zeros_like(state_ref)

        row = lax.broadcasted_iota(jnp.int32, (L, L), 0)
        col = lax.broadcasted_iota(jnp.int32, (L, L), 1)

        def decays(ai):
            a_col = ac_ref[ai]
            a_row = ar_ref[ai]
            a_last = ac_ref[ai, pl.ds(L - 1, 1), :]
            seg = a_col - a_row
            if not causal:
                dec = jnp.exp(-jnp.abs(seg))
                if sub_chunk is not None:
                    shift = sub_chunk.bit_length() - 1
                    rs, cs = jnp.right_shift(row, shift), jnp.right_shift(col, shift)
                    dec = jnp.where(cs >= rs if reverse else cs <= rs, dec, 0.0)
            elif reverse:
                dec = jnp.where(col > row, jnp.exp(jnp.minimum(-seg, 0.0)), 0.0)
            else:
                dec = jnp.where(row > col, jnp.exp(jnp.minimum(seg, 0.0)), 0.0)
            if reverse:
                qd, kd = jnp.exp(a_last - a_col), jnp.exp(a_col)
            else:
                qd, kd = jnp.exp(a_col), jnp.exp(a_last - a_col)
            return dec, qd, kd, jnp.exp(a_last)

        fixed = decays(0) if a_chunks == 1 else None

        def one(step, carry):
            c = cpb - 1 - step if reverse else step
            sl = pl.ds(pl.multiple_of(c * L, L), L)
            qc = q_ref[sl, :].astype(F32)
            kc = k_ref[sl, :].astype(F32)
            vf = v_ref[sl, :].astype(F32)
            vc = vf.astype(BF16)
            dec, qd, kd, chunk_decay = fixed if a_chunks == 1 else decays(c)
            sc = lax.dot_general(qc.astype(BF16), kc.astype(BF16), (((1,), (1,)), ((), ())),
                                 preferred_element_type=F32) * dec
            state = state_ref[...]
            intra = jnp.dot(sc.astype(BF16), vc, preferred_element_type=F32)
            cross = jnp.dot((qc * qd).astype(BF16), state.astype(BF16), preferred_element_type=F32)
            out = intra + cross
            if parts_out:
                o_refs[1][sl, :] = cross
                o_refs[2][c] = state
            if causal:
                out = out + jnp.sum(qc * kc, axis=-1, keepdims=True) * vf
            o_refs[0][sl, :] = out.astype(o_refs[0].dtype)
            upd = lax.dot_general((kc * kd).astype(BF16), vc, (((0,), (0,)), ((), ())),
                                  preferred_element_type=F32)
            state_ref[...] = state * chunk_decay + upd
            return carry

        lax.fori_loop(0, cpb, one, 0)

    if out_2d:
        out_spec = pl.BlockSpec((L * cpb, dv), lambda h, i: (blk(i), h))
        out_shape = jax.ShapeDtypeStruct((s_len, heads * dv), out_dtype)
    else:
        out_spec = pl.BlockSpec((None, L * cpb, dv), lambda h, i: (h, blk(i), 0))
        out_shape = jax.ShapeDtypeStruct((heads, s_len, dv), out_dtype)
    out_specs, out_shapes = [out_spec], [out_shape]
    if parts_out:
        assert not out_2d and out_dtype == F32
        out_specs += [out_spec, pl.BlockSpec((None, cpb, dk, dv), lambda h, i: (h, blk(i), 0, 0))]
        out_shapes += [out_shape, jax.ShapeDtypeStruct((heads, s_len // L, dk, dv), F32)]
    arrs = [op[0] if isinstance(op, tuple) else op for op in (q, k, v)]
    res = pl.pallas_call(
        body, name=name, grid=(heads, nb),
        in_specs=[seq_spec(q), seq_spec(k), seq_spec(v), a_spec(ac, (L, 1)), a_spec(ar, (1, L))],
        out_specs=out_specs, out_shape=out_shapes,
        scratch_shapes=[pltpu.VMEM((dk, dv), F32)],
        compiler_params=_params(("parallel", "arbitrary")),
    )(*arrs, ac, ar)
    return res if parts_out else res[0]


SB_KEYS = 256


def _sb_tile(q, kb, scale, valid, tri):
    z = lax.dot_general(q, kb, (((1,), (1,)), ((), ())), preferred_element_type=F32) * scale
    sp = _softplus(z)
    lk = -sp if valid is None else jnp.where(valid, -sp, 0.0)
    later_in = _dot_exact_mask(lk, tri)
    return z, lk, (z - sp) + later_in


def _sb_loops(i, ratio, step, init):
    carry = lax.fori_loop(0, ratio, lambda n, c: step((i + 1) * ratio - 1 - n, True, c), init)
    return lax.fori_loop(0, i * ratio, lambda n, c: step(i * ratio - 1 - n, False, c), carry)


def _sb_masks(i, bq):
    bk = SB_KEYS
    row = lax.broadcasted_iota(jnp.int32, (bk, bk), 0)
    col = lax.broadcasted_iota(jnp.int32, (bk, bk), 1)
    qpos = i * bq + lax.broadcasted_iota(jnp.int32, (bq, bk), 0)
    kcol = lax.broadcasted_iota(jnp.int32, (bq, bk), 1)
    return row, col, lambda kbi: (kbi * bk + kcol) < qpos


def sb_forward(qkv, name="sb_fwd"):
    s_len = qkv.shape[0]
    bk = SB_KEYS
    bq = _pick(s_len, (512, 256))
    scale = HEAD ** -0.5

    def body(q_ref, k_ref, v_ref, o_ref):
        i = pl.program_id(1)
        q = q_ref[...]
        row, col, valid_of = _sb_masks(i, bq)
        tri_gt = (row > col).astype(BF16)

        def step(kbi, masked, carry):
            acc, a_run = carry
            sl = pl.ds(pl.multiple_of(kbi * bk, bk), bk)
            valid = valid_of(kbi) if masked else None
            _, lk, expo = _sb_tile(q, k_ref[sl, :], scale, valid, tri_gt)
            w = jnp.exp(expo + a_run)
            if masked:
                w = jnp.where(valid, w, 0.0)
            acc = acc + jnp.dot(w.astype(BF16), v_ref[sl, :], preferred_element_type=F32)
            return acc, a_run + jnp.sum(lk, axis=-1, keepdims=True)

        acc, _ = _sb_loops(i, bq // bk, step, (jnp.zeros((bq, HEAD), F32), jnp.zeros((bq, 1), F32)))
        o_ref[...] = acc

    return pl.pallas_call(
        body, name=name, grid=(N_HEADS, s_len // bq),
        in_specs=[pl.BlockSpec((bq, HEAD), lambda h, i: (i, h)),
                  pl.BlockSpec((s_len, HEAD), lambda h, i: (0, N_HEADS + h)),
                  pl.BlockSpec((s_len, HEAD), lambda h, i: (0, 2 * N_HEADS + h))],
        out_specs=pl.BlockSpec((bq, HEAD), lambda h, i: (i, h)),
        out_shape=jax.ShapeDtypeStruct((s_len, BR_WIDTH), F32),
        compiler_params=_params(("parallel", "arbitrary")),
    )(qkv, qkv, qkv)


def sb_backward(qkv, o, do, name="sb_bwd"):
    s_len = qkv.shape[0]
    bk = SB_KEYS
    bq = _pick(s_len, (512, 256))
    scale = HEAD ** -0.5

    def body(q_ref, k_ref, v_ref, o_ref, do_ref, dq_ref, dk_ref, dv_ref):
        i = pl.program_id(1)

        @pl.when(i == 0)
        def _():
            dk_ref[...] = jnp.zeros_like(dk_ref)
            dv_ref[...] = jnp.zeros_like(dv_ref)

        q = q_ref[...]
        dob = do_ref[...].astype(BF16)
        delta = jnp.sum(dob.astype(F32) * o_ref[...], axis=-1, keepdims=True)
        row, col, valid_of = _sb_masks(i, bq)
        tri_gt = (row > col).astype(BF16)
        tri_ge = (row >= col).astype(BF16)

        def step(kbi, masked, carry):
            dq, a_run, e_run = carry
            sl = pl.ds(pl.multiple_of(kbi * bk, bk), bk)
            valid = valid_of(kbi) if masked else None
            kb = k_ref[sl, :]
            vb = v_ref[sl, :]
            z, lk, expo = _sb_tile(q, kb, scale, valid, tri_gt)
            w = jnp.exp(expo + a_run)
            if masked:
                w = jnp.where(valid, w, 0.0)
            wb = w.astype(BF16)
            beta = _sigmoid(z)
            dw = lax.dot_general(dob, vb, (((1,), (1,)), ((), ())), preferred_element_type=F32)
            e = dw * wb.astype(F32)
            suffix = e_run + _dot_exact_mask(e, tri_ge)
            dz = (e * (1.0 - beta) - beta * (delta - suffix)) * scale
            if masked:
                dz = jnp.where(valid, dz, 0.0)
            dzb = dz.astype(BF16)
            dq = dq + jnp.dot(dzb, kb, preferred_element_type=F32)
            dk_ref[sl, :] += lax.dot_general(dzb, q, (((0,), (0,)), ((), ())), preferred_element_type=F32)
            dv_ref[sl, :] += lax.dot_general(wb, dob, (((0,), (0,)), ((), ())),
                                             preferred_element_type=F32)
            return (dq, a_run + jnp.sum(lk, axis=-1, keepdims=True),
                    e_run + jnp.sum(e, axis=-1, keepdims=True))

        zero = jnp.zeros((bq, 1), F32)
        dq, _, _ = _sb_loops(i, bq // bk, step, (jnp.zeros((bq, HEAD), F32), zero, zero))
        dq_ref[...] = dq

    blk = pl.BlockSpec((bq, HEAD), lambda h, i: (i, h))
    full = pl.BlockSpec((s_len, HEAD), lambda h, i: (0, h))
    shape = jax.ShapeDtypeStruct((s_len, BR_WIDTH), F32)
    return pl.pallas_call(
        body, name=name, grid=(N_HEADS, s_len // bq),
        in_specs=[blk,
                  pl.BlockSpec((s_len, HEAD), lambda h, i: (0, N_HEADS + h)),
                  pl.BlockSpec((s_len, HEAD), lambda h, i: (0, 2 * N_HEADS + h)),
                  blk, blk],
        out_specs=[blk, full, full],
        out_shape=[shape, shape, shape],
        compiler_params=_params(("parallel", "arbitrary")),
    )(qkv, qkv, qkv, o, do)


ROPE_BASE = 10000.0


def rope_tables(positions):
    half = HEAD // 2
    inv_freq = ROPE_BASE ** (-2.0 * jnp.arange(half, dtype=F32) / HEAD)
    ang = positions.astype(F32)[:, None] * inv_freq
    cos, sin = jnp.cos(ang), jnp.sin(ang)
    return jnp.concatenate([cos, cos], 1), jnp.concatenate([-sin, sin], 1)


def ret_decay_tables(s_len):
    block = _pick(s_len, (512, 256, 128, RET_CHUNK))
    lg = np.log1p(-np.exp2(-5.0 - np.arange(N_HEADS, dtype=np.float64)))
    ac = (lg[:, None] * (np.arange(block) + 1.0)[None, :]).astype(np.float32)
    return jnp.asarray(ac[:, None, :, None]), jnp.asarray(ac[:, None, None, :])


def _head_major(s, tm, d=HEAD, heads=N_HEADS, dtype=F32):
    return ((heads, s, d), dtype, (None, tm, d), lambda h, i: (h, i, 0))


def _head_cols(arr, tm, first_col):
    cb = first_col // HEAD
    return (arr, (tm, HEAD), lambda h, i: (i, cb + h))


def _head_rows(arr, tm):
    return (arr, (None, tm, arr.shape[2]), lambda h, i: (h, i, 0))


def rotary_fwd(proj, cos2, sin2):
    s = proj.shape[0]
    tm = _pick(s, (512, 256, 128))

    def fn(qv, kv, c, sn):
        def rot(t):
            return t * c + pltpu.roll(t, HEAD // 2, 1) * sn
        return rot(qv), rot(kv) * (HEAD ** -0.5)

    tab = lambda t: (t, (tm, HEAD), lambda h, i: (i, 0))
    return bmap("rotary_fwd", fn, (N_HEADS, s // tm),
                [_head_cols(proj, tm, COL_RET), _head_cols(proj, tm, COL_RET + BR_WIDTH), tab(cos2), tab(sin2)],
                [_head_major(s, tm), _head_major(s, tm)])


def rotary_bwd(dqr, dkr, cos2, sin2):
    s = dqr.shape[1]
    tm = _pick(s, (512, 256, 128))

    def fn(dq, dk, c, sn):
        def unrot(t):
            return t * c - pltpu.roll(t, HEAD // 2, 1) * sn
        return unrot(dq), unrot(dk) * (HEAD ** -0.5)

    tab = lambda t: (t, (tm, HEAD), lambda h, i: (i, 0))
    out = ((s, BR_WIDTH), BF16, (tm, HEAD), lambda h, i: (i, h))
    return bmap("rotary_bwd", fn, (N_HEADS, s // tm),
                [_head_rows(dqr, tm), _head_rows(dkr, tm), tab(cos2), tab(sin2)], [out, out])


def _group_norm(o):
    mu = jnp.mean(o, axis=-1, keepdims=True)
    oc = o - mu
    r = lax.rsqrt(jnp.mean(oc * oc, axis=-1, keepdims=True) + NORM_EPS)
    return oc * r, r


def gn_gate_fwd(o, gn_w, proj):
    s = o.shape[1]
    tm = _pick(s, (512, 256, 128))

    def fn(ov, wv, gv):
        on, _ = _group_norm(ov)
        return (on * wv * _silu(gv),)

    w3 = gn_w.reshape(N_HEADS, 1, HEAD)
    return bmap("gn_gate_fwd", fn, (N_HEADS, s // tm),
                [_head_rows(o, tm), (w3, (None, 1, HEAD), lambda h, i: (h, 0, 0)),
                 _head_cols(proj, tm, COL_RET + 3 * BR_WIDTH)],
                [((s, BR_WIDTH), BF16, (tm, HEAD), lambda h, i: (i, h))])[0]


def gn_gate_bwd(o, gn_w, proj, dy):
    s = o.shape[1]
    tm = _pick(s, (512, 256, 128))

    def fn(ov, wv, gv, dyv):
        on, r = _group_norm(ov)
        yn = on * wv
        dyn = dyv * _silu(gv)
        dg = dyv * yn * _dsilu(gv)
        don = dyn * wv
        do = r * (don - jnp.mean(don, axis=-1, keepdims=True) - on * jnp.mean(don * on, axis=-1, keepdims=True))
        return do, dg, jnp.sum(dyn * on, axis=0, keepdims=True)

    w3 = gn_w.reshape(N_HEADS, 1, HEAD)
    do, dg, dw = bmap("gn_gate_bwd", fn, (N_HEADS, s // tm),
                      [_head_rows(o, tm), (w3, (None, 1, HEAD), lambda h, i: (h, 0, 0)),
                       _head_cols(proj, tm, COL_RET + 3 * BR_WIDTH), _head_cols(dy, tm, 0)],
                      [_head_major(s, tm), ((s, BR_WIDTH), BF16, (tm, HEAD), lambda h, i: (i, h))],
                      [((N_HEADS, 1, HEAD), (None, 1, HEAD), lambda h, i: (h, 0, 0))])
    return do, dg, dw.reshape(BR_WIDTH)


XBC_WIDTH = 2048


def _shifted(u, back):
    s = u.shape[0]
    n = SSD_CONV - 1
    if back:
        return [jnp.pad(u, ((0, n - k), (0, 0)))[n - k:] for k in range(n)]
    return [jnp.pad(u, ((n - k, 0), (0, 0)))[:s] for k in range(n)]


def _conv(us, w, b):
    return us[0] * w[0:1] + us[1] * w[1:2] + us[2] * w[2:3] + us[3] * w[3:4] + b


def ssd_conv_fwd(us, conv_w, conv_b):
    s = us[0].shape[0]
    tm = _pick(s, (256, 128))

    def fn(u0, u1, u2, u3, w, b):
        return (_silu(_conv((u0, u1, u2, u3), w, b)),)

    return bmap("ssd_conv_fwd", fn, (s // tm,), [rows(u, tm) for u in us] + [par(conv_w), par(conv_b)],
                [row_out(s, XBC_WIDTH, F32, tm)])[0]


def ssd_conv_bwd_a(us, conv_w, conv_b, dact):
    s = dact.shape[0]
    tm = _pick(s, (256, 128))

    def fn(u0, u1, u2, u3, w, b, da):
        dcv = da * _dsilu(_conv((u0, u1, u2, u3), w, b))
        sums = [jnp.sum(dcv * u, axis=0, keepdims=True) for u in (u0, u1, u2, u3)]
        return (dcv, *sums, jnp.sum(dcv, axis=0, keepdims=True))

    res = bmap("ssd_conv_bwd_a", fn, (s // tm,),
               [rows(u, tm) for u in us] + [par(conv_w), par(conv_b), rows(dact, tm)],
               [row_out(s, XBC_WIDTH, F32, tm)], [col_acc(XBC_WIDTH)] * 5)
    return res[0], jnp.concatenate(res[1:5], axis=0), res[5][0]


def ssd_conv_bwd_b(dcs, conv_w):
    s = dcs[0].shape[0]
    tm = _pick(s, (256, 128))

    def fn(d0, d1, d2, d3, w):
        return (d0 * w[0:1] + d1 * w[1:2] + d2 * w[2:3] + d3 * w[3:4],)

    return bmap("ssd_conv_bwd_b", fn, (s // tm,), [rows(d, tm) for d in dcs] + [par(conv_w)],
                [row_out(s, XBC_WIDTH, BF16, tm)])[0]


def _mask_dot_exact(mask_bf16, x):
    hi = x.astype(BF16)
    r1 = x - hi.astype(F32)
    mid = r1.astype(BF16)
    lo = (r1 - mid.astype(F32)).astype(BF16)
    return (jnp.dot(mask_bf16, hi, preferred_element_type=F32) + jnp.dot(mask_bf16, mid, preferred_element_type=F32)
            + jnp.dot(mask_bf16, lo, preferred_element_type=F32))


def ssd_dt_fwd(proj, dt_bias, a_log):
    s = proj.shape[0]
    L = SSD_CHUNK

    def fn(dtr, bias, alog):
        dt = _softplus(dtr + bias)
        a = dt * (-jnp.exp(alog))
        row = lax.broadcasted_iota(jnp.int32, (L, L), 0)
        col = lax.broadcasted_iota(jnp.int32, (L, L), 1)
        return dt, _mask_dot_exact((row >= col).astype(BF16), a)

    return bmap("ssd_dt_fwd", fn, (s // L,), [rows(proj, L, LANES, COL_DT), par(dt_bias), par(a_log)],
                [row_out(s, LANES, F32, L), row_out(s, LANES, F32, L)])


def ssd_state_dot(s_fwd, s_rev, a_tot):
    heads, nc = s_fwd.shape[0], s_fwd.shape[1]
    cb = _pick(nc, (8, 4, 2, 1))

    def fn(sf, sr, at):
        return (jnp.sum(jnp.sum(sf * sr, axis=2, keepdims=True), axis=1, keepdims=True) * jnp.exp(at),)

    st = lambda a: (a, (None, cb) + a.shape[2:], lambda h, i: (h, i, 0, 0))
    return bmap("ssd_state_dot", fn, (heads, nc // cb), [st(s_fwd), st(s_rev), st(a_tot)],
                [((heads, nc, 1, 1), F32, (None, cb, 1, 1), lambda h, i: (h, i, 0, 0))])[0]


def ssd_decay_grad(e_incl, v_excl, g_chunk, own):
    s = e_incl.shape[0]
    L = SSD_CHUNK

    def fn(ev, vv, gv, ov):
        row = lax.broadcasted_iota(jnp.int32, (L, L), 0)
        col = lax.broadcasted_iota(jnp.int32, (L, L), 1)
        return (_mask_dot_exact((col >= row).astype(BF16), ev) + _mask_dot_exact((col < row).astype(BF16), vv)
                + gv + ov,)

    return bmap("ssd_decay_grad", fn, (s // L,),
                [rows(e_incl, L), rows(v_excl, L), (g_chunk, (None, 1, LANES), lambda i: (i, 0, 0)), rows(own, L)],
                [row_out(s, LANES, F32, L)])[0]


def ssd_decay_grad_own(c_grp, b_grp, dy_h, xdt_h, ac, ar):
    s_len = dy_h.shape[1]
    L = SSD_CHUNK
    nc = s_len // L
    cpb = _pick(nc, (8, 4, 2, 1))
    rep = SSD_HEADS // SSD_GROUPS

    def grp_spec(op):
        cb = op[1] // HEAD
        return pl.BlockSpec((L * cpb, HEAD), lambda h, i: (i, cb + h // rep))

    def body(c_ref, b_ref, dy_ref, x_ref, ac_ref, ar_ref, o_ref):
        row = lax.broadcasted_iota(jnp.int32, (L, L), 0)
        col = lax.broadcasted_iota(jnp.int32, (L, L), 1)
        earlier = (row < col).astype(BF16)

        def one(c, carry):
            sl = pl.ds(pl.multiple_of(c * L, L), L)
            seg = ac_ref[c] - ar_ref[c]
            dec = jnp.where(row > col, jnp.exp(jnp.minimum(seg, 0.0)), 0.0)
            nt = (((1,), (1,)), ((), ()))
            sc = lax.dot_general(c_ref[sl, :].astype(BF16), b_ref[sl, :].astype(BF16), nt, preferred_element_type=F32)
            dx = lax.dot_general(dy_ref[sl, :].astype(BF16), x_ref[sl, :].astype(BF16), nt, preferred_element_type=F32)
            before_r = _dot_exact_mask(sc * dec * dx, earlier)
            o_ref[c] = jnp.sum(jnp.where(row >= col, before_r, 0.0), axis=0, keepdims=True)
            return carry

        lax.fori_loop(0, cpb, one, 0)

    seq = lambda a: pl.BlockSpec((None, L * cpb, a.shape[2]), lambda h, i: (h, i, 0))
    dec_spec = lambda shape: pl.BlockSpec((None, cpb) + shape, lambda h, i: (h, i, 0, 0))
    return pl.pallas_call(
        body, name="ssd_decay_grad_own", grid=(SSD_HEADS, nc // cpb),
        in_specs=[grp_spec(c_grp), grp_spec(b_grp), seq(dy_h), seq(xdt_h), dec_spec((L, 1)), dec_spec((1, L))],
        out_specs=pl.BlockSpec((None, cpb, 1, L), lambda h, i: (h, i, 0, 0)),
        out_shape=jax.ShapeDtypeStruct((SSD_HEADS, nc, 1, L), F32),
        compiler_params=_params(("parallel", "parallel")),
    )(c_grp[0], b_grp[0], dy_h, xdt_h, ac, ar)


def ssd_dt_bwd(da, ddt_x, proj, dt_bias, a_log):
    s = da.shape[0]
    tm = _pick(s, (512, 256, 128))

    def fn(dav, dxv, dtr, bias, alog):
        pre = dtr + bias
        a_coef = -jnp.exp(alog)
        ddtr = (dav * a_coef + dxv) * _sigmoid(pre)
        return (ddtr, jnp.sum(ddtr, axis=0, keepdims=True),
                jnp.sum(dav * _softplus(pre) * a_coef, axis=0, keepdims=True))

    ddtr, dbias, dalog = bmap("ssd_dt_bwd", fn, (s // tm,),
                              [rows(da, tm), rows(ddt_x, tm), rows(proj, tm, LANES, COL_DT), par(dt_bias), par(a_log)],
                              [row_out(s, LANES, BF16, tm)], [col_acc(LANES), col_acc(LANES)])
    return ddtr, dbias[0], dalog[0]


def ssd_xdt(x_h, dt_h):
    s = x_h.shape[1]
    tm = _pick(s, (512, 256, 128))
    return bmap("ssd_xdt", lambda xv, dv: (xv * dv,), (SSD_HEADS, s // tm),
                [_head_rows(x_h, tm), _head_rows(dt_h, tm)], [_head_major(s, tm, SSD_P, SSD_HEADS)])[0]


def ssd_gate_fwd(ytil, act, proj, d_full, norm_w):
    s = ytil.shape[0]
    tm = _pick(s, (256, 128))

    def fn(yt, xv, zv, dv, wv):
        u = (yt + xv * dv) * _silu(zv)
        return (u * lax.rsqrt(jnp.mean(u * u, axis=-1, keepdims=True) + NORM_EPS) * wv,)

    return bmap("ssd_gate_fwd", fn, (s // tm,),
                [rows(ytil, tm), rows(act, tm, BR_WIDTH, 0), rows(proj, tm, BR_WIDTH, COL_Z), par(d_full), par(norm_w)],
                [row_out(s, BR_WIDTH, BF16, tm)])[0]


def ssd_gate_bwd(ytil, act, proj, d_full, norm_w, dout):
    s = ytil.shape[0]
    tm = _pick(s, (256, 128))

    def fn(yt, xv, zv, dv, wv, dov):
        y = yt + xv * dv
        sz = _silu(zv)
        u = y * sz
        r = lax.rsqrt(jnp.mean(u * u, axis=-1, keepdims=True) + NORM_EPS)
        un = u * r
        g = dov * wv
        du = r * (g - un * jnp.mean(g * un, axis=-1, keepdims=True))
        dy = du * sz
        return (dy, du * y * _dsilu(zv), jnp.sum(dov * un, axis=0, keepdims=True),
                jnp.sum(dy * xv, axis=0, keepdims=True))

    dy, dz, dw, dd = bmap("ssd_gate_bwd", fn, (s // tm,),
                          [rows(ytil, tm), rows(act, tm, BR_WIDTH, 0), rows(proj, tm, BR_WIDTH, COL_Z), par(d_full),
                           par(norm_w), rows(dout, tm)],
                          [row_out(s, BR_WIDTH, F32, tm), row_out(s, BR_WIDTH, BF16, tm)],
                          [col_acc(BR_WIDTH), col_acc(BR_WIDTH)])
    return dy, dz, dw[0], dd[0]


def ssd_head_bwd(dy_h, y_cross, dxdt_h, dx_cross, xdt_h, x_h, dt_h, d_h):
    s = dy_h.shape[1]
    tm = _pick(s, (512, 256, 128))

    def fn(dy, yc, dxdt, dxc, xdt, xv, dt, dskip):
        return (dy * dskip + dxdt * dt,
                jnp.sum(dy * yc, axis=-1, keepdims=True),
                jnp.sum(dxc * xdt, axis=-1, keepdims=True),
                jnp.sum(dxdt * xv, axis=-1, keepdims=True))

    one = ((SSD_HEADS, s, 1), F32, (None, tm, 1), lambda h, i: (h, i, 0))
    return bmap("ssd_head_bwd", fn, (SSD_HEADS, s // tm),
                [_head_rows(a, tm) for a in (dy_h, y_cross, dxdt_h, dx_cross, xdt_h, x_h, dt_h)]
                + [(d_h, (None, 1, 1), lambda h, i: (h, 0, 0))],
                [_head_major(s, tm, SSD_P, SSD_HEADS), one, one, one])


def ssd_group_sum(db_h, dc_h):
    s = db_h.shape[1]
    tm = _pick(s, (512, 256, 128))
    e = SSD_HEADS // SSD_GROUPS

    def fn(*v):
        return v[0] + v[1] + v[2] + v[3], v[4] + v[5] + v[6] + v[7]

    def head(arr, j):
        return (arr, (None, tm, SSD_N), lambda g, i: (g * e + j, i, 0))

    out = ((s, SSD_GROUPS * SSD_N), F32, (tm, SSD_N), lambda g, i: (i, g))
    return bmap("ssd_group_sum", fn, (SSD_GROUPS, s // tm),
                [head(db_h, j) for j in range(e)] + [head(dc_h, j) for j in range(e)], [out, out])


def _gate_ins(proj, b_gate, tm):
    b2 = b_gate.reshape(1, 3 * D_MODEL)
    ins = [rows(proj, tm, D_MODEL, COL_GATE + j * D_MODEL) for j in range(3)]
    ins += [(b2, (1, D_MODEL), (lambda j: lambda i: (0, j))(j)) for j in range(3)]
    return ins


def merge_fwd(proj, b_gate, ps):
    s = proj.shape[0]
    tm = 128

    def fn(g0, g1, g2, b0, b1, b2, p0, p1, p2):
        return (_sigmoid(g0 + b0) * p0 + _sigmoid(g1 + b1) * p1 + _sigmoid(g2 + b2) * p2,)

    return bmap("merge_fwd", fn, (s // tm,), _gate_ins(proj, b_gate, tm) + [rows(p, tm) for p in ps],
                [row_out(s, D_MODEL, BF16, tm)])[0]


def merge_bwd(proj, b_gate, ps, dmerged):
    s = proj.shape[0]
    tm = 128

    def fn(g0, g1, g2, b0, b1, b2, p0, p1, p2, dm):
        dps, dgs = [], []
        for g, b, p in ((g0, b0, p0), (g1, b1, p1), (g2, b2, p2)):
            gate = _sigmoid(g + b)
            dps.append(dm * gate)
            dgs.append(dm * p * gate * (1.0 - gate))
        dgl = jnp.concatenate(dgs, axis=1)
        return (*dps, dgl, jnp.sum(dgl, axis=0, keepdims=True))

    res = bmap("merge_bwd", fn, (s // tm,),
               _gate_ins(proj, b_gate, tm) + [rows(p, tm) for p in ps] + [rows(dmerged, tm)],
               [row_out(s, D_MODEL, BF16, tm)] * 3 + [row_out(s, 3 * D_MODEL, BF16, tm)], [col_acc(3 * D_MODEL)])
    return res[0:3], res[3], res[4][0]


def swiglu_fwd(gu):
    s = gu.shape[0]
    tm = 128
    return bmap("swiglu_fwd", lambda g, u: (_silu(g) * u,), (s // tm,),
                [rows(gu, tm, FFN_HIDDEN, 0), rows(gu, tm, FFN_HIDDEN, FFN_HIDDEN)],
                [row_out(s, FFN_HIDDEN, BF16, tm)])[0]


def swiglu_bwd(gu, dact):
    s = gu.shape[0]
    tm = 128

    def fn(g, u, da):
        return (jnp.concatenate([da * u * _dsilu(g), da * _silu(g)], axis=1),)

    return bmap("swiglu_bwd", fn, (s // tm,),
                [rows(gu, tm, FFN_HIDDEN, 0), rows(gu, tm, FFN_HIDDEN, FFN_HIDDEN), rows(dact, tm)],
                [row_out(s, 2 * FFN_HIDDEN, BF16, tm)])[0]


def loss_head(y, target):
    s, d = y.shape
    tm = _pick(s, (256, 128))

    def fn(yv, tv):
        err = yv - tv
        part = jnp.sum(jnp.mean(err * err, axis=-1, keepdims=True), axis=0, keepdims=True)
        return err * (1.0 / d), 0.5 * part

    dy, loss = bmap("loss_head", fn, (s // tm,), [rows(y, tm), rows(target, tm)], [row_out(s, d, F32, tm)],
                    [((1, 1), (1, 1), lambda i: (0, 0))])
    return dy, loss[0, 0]


def _to_heads(a2d, heads, d):
    return a2d.reshape(a2d.shape[0], heads, d).transpose(1, 0, 2)


def _from_heads(a3d):
    return a3d.transpose(1, 0, 2).reshape(a3d.shape[1], a3d.shape[0] * a3d.shape[2])


def _pad_lanes(v):
    return jnp.pad(v.reshape(1, -1), ((0, 0), (0, LANES - v.shape[-1])))


def layer_forward(x, w, p, tabs):
    s = x.shape[0]
    cos2, sin2, ac_ret, ar_ret = tabs
    sv = {"x": x}
    h = rms_fwd(x, p["norm_mix_pre"], BF16, "rms_mix_pre")
    proj = matmul(h, w["w_in"], name="mm_in")
    sv.update(h=h, proj=proj)

    qr, kr = rotary_fwd(proj, cos2, sin2)
    v_ret = (proj, COL_RET + 2 * BR_WIDTH, N_HEADS)
    o_ret = linear_attention(qr, kr, v_ret, ac_ret, ar_ret, chunk=ac_ret.shape[2], sub_chunk=RET_CHUNK, causal=False,
                             reverse=False, heads=N_HEADS, name="ret_fwd")
    y_ret = gn_gate_fwd(o_ret, p["ret_gn_w"], proj)
    sv.update(qr=qr, kr=kr, o_ret=o_ret, y_ret=y_ret)

    qkv = proj[:, COL_SB:COL_SB + 3 * BR_WIDTH].astype(BF16)
    o_sb = sb_forward(qkv)
    y_sb = o_sb.astype(BF16)
    sv.update(qkv=qkv, o_sb=o_sb, y_sb=y_sb)

    xbc = proj[:, COL_XBC:COL_XBC + XBC_WIDTH]
    us = _shifted(xbc, False) + [xbc]
    conv_b = p["ssd_conv_b"].reshape(1, XBC_WIDTH)
    act = ssd_conv_fwd(us, p["ssd_conv_w"], conv_b)
    dt, acum = ssd_dt_fwd(proj, _pad_lanes(p["ssd_dt_bias"]), _pad_lanes(p["ssd_a_log"]))
    nc = s // SSD_CHUNK
    x_h = _to_heads(act[:, :BR_WIDTH], SSD_HEADS, SSD_P)
    dt_h = dt[:, :SSD_HEADS].T[:, :, None]
    acs = acum[:, :SSD_HEADS].T.reshape(SSD_HEADS, nc, SSD_CHUNK)
    ac_h, ar_h = acs[:, :, :, None], acs[:, :, None, :]
    xdt_h = ssd_xdt(x_h, dt_h)
    b_grp = (act, BR_WIDTH, SSD_GROUPS)
    c_grp = (act, BR_WIDTH + SSD_GROUPS * SSD_N, SSD_GROUPS)
    ytil_h, y_cross, s_fwd = linear_attention(c_grp, b_grp, xdt_h, ac_h, ar_h, chunk=SSD_CHUNK, causal=True,
                                              reverse=False, heads=SSD_HEADS, name="ssd_fwd", parts_out=True)
    ytil = _from_heads(ytil_h)
    d_full = jnp.repeat(p["ssd_d"], SSD_P).reshape(1, BR_WIDTH)
    norm_w = p["ssd_norm_w"].reshape(1, BR_WIDTH)
    y_ssd = ssd_gate_fwd(ytil, act, proj, d_full, norm_w)
    sv.update(act=act, x_h=x_h, dt_h=dt_h, ac_h=ac_h, ar_h=ar_h, xdt_h=xdt_h, y_cross=y_cross, s_fwd=s_fwd, ytil=ytil,
              y_ssd=y_ssd)

    ys = (y_ret, y_sb, y_ssd)
    ps = [matmul(y, wb, name="mm_branch") for y, wb in zip(ys, w["w_branch"])]
    merged = merge_fwd(proj, p["b_gate"], ps)
    mo = matmul(merged, w["w_out"], name="mm_out")
    x1 = rms_fwd(mo, p["norm_mix_post"], F32, "rms_mix_post", resid=x)
    sv.update(ps=ps, merged=merged, mo=mo, x1=x1)

    h2 = rms_fwd(x1, p["norm_ffn_pre"], BF16, "rms_ffn_pre")
    gu = matmul(h2, w["w_gu"], name="mm_gu")
    actf = swiglu_fwd(gu)
    f = matmul(actf, w["w_down"], name="mm_down")
    x2 = rms_fwd(f, p["norm_ffn_post"], F32, "rms_ffn_post", resid=x1)
    sv.update(h2=h2, gu=gu, actf=actf, f=f)
    return x2, sv


def layer_backward(dx2, sv, w, p, tabs):
    cos2, sin2, ac_ret, ar_ret = tabs
    proj = sv["proj"]
    s = proj.shape[0]
    gs = {}

    df, gs["norm_ffn_post"] = rms_bwd(sv["f"], p["norm_ffn_post"], dx2, "rms_ffn_post_bwd", out_dtype=BF16)
    dactf = matmul(df, w["w_down"], nt=True, name="mm_down_dx")
    gw_down = matmul(sv["actf"].T, df, out_dtype=BF16, name="mm_down_dw")
    dgu = swiglu_bwd(sv["gu"], dactf)
    dh2 = matmul(dgu, w["w_gu"], nt=True, name="mm_gu_dx")
    gw_gu = matmul(sv["h2"].T, dgu, out_dtype=BF16, name="mm_gu_dw")
    dx1, gs["norm_ffn_pre"] = rms_bwd(sv["x1"], p["norm_ffn_pre"], dh2, "rms_ffn_pre_bwd", resid=dx2)

    dmo, gs["norm_mix_post"] = rms_bwd(sv["mo"], p["norm_mix_post"], dx1, "rms_mix_post_bwd", out_dtype=BF16)
    dmerged = matmul(dmo, w["w_out"], nt=True, name="mm_out_dx")
    gw_out = matmul(sv["merged"].T, dmo, out_dtype=BF16, name="mm_out_dw")
    dps, dgl, gs["b_gate"] = merge_bwd(proj, p["b_gate"], sv["ps"], dmerged)
    ys = (sv["y_ret"], sv["y_sb"], sv["y_ssd"])
    dys = [matmul(dp, wb, nt=True, name="mm_branch_dx") for dp, wb in zip(dps, w["w_branch"])]
    gw_branch = [matmul(y.T, dp, out_dtype=BF16, name="mm_branch_dw") for y, dp in zip(ys, dps)]

    do_h, dg, gs["ret_gn_w"] = gn_gate_bwd(sv["o_ret"], p["ret_gn_w"], proj, dys[0])
    v_ret = (proj, COL_RET + 2 * BR_WIDTH, N_HEADS)
    ret = dict(chunk=ac_ret.shape[2], sub_chunk=RET_CHUNK, causal=False, heads=N_HEADS)
    dqr = linear_attention(do_h, v_ret, sv["kr"], ac_ret, ar_ret, reverse=False, name="ret_dq", **ret)
    dkr = linear_attention(v_ret, do_h, sv["qr"], ac_ret, ar_ret, reverse=True, name="ret_dk", **ret)
    dv = linear_attention(sv["kr"], sv["qr"], do_h, ac_ret, ar_ret, reverse=True, name="ret_dv",
                          out_dtype=BF16, out_2d=True, **ret)
    dq, dk = rotary_bwd(dqr, dkr, cos2, sin2)

    dsq, dsk, dsv = sb_backward(sv["qkv"], sv["o_sb"], dys[1])

    act = sv["act"]
    d_full = jnp.repeat(p["ssd_d"], SSD_P).reshape(1, BR_WIDTH)
    norm_w = p["ssd_norm_w"].reshape(1, BR_WIDTH)
    dy, dz, gs["ssd_norm_w"], dd_lanes = ssd_gate_bwd(sv["ytil"], act, proj, d_full, norm_w, dys[2])
    gs["ssd_d"] = dd_lanes.reshape(SSD_HEADS, SSD_P).sum(axis=1)
    dy_h = _to_heads(dy, SSD_HEADS, SSD_P)
    b_grp = (act, BR_WIDTH, SSD_GROUPS)
    c_grp = (act, BR_WIDTH + SSD_GROUPS * SSD_N, SSD_GROUPS)
    ssd = dict(chunk=SSD_CHUNK, causal=True, heads=SSD_HEADS)
    ac_h, ar_h, xdt_h = sv["ac_h"], sv["ar_h"], sv["xdt_h"]
    dxdt_h, dx_cross, s_rev = linear_attention(b_grp, c_grp, dy_h, ac_h, ar_h, reverse=True, name="ssd_dx",
                                               parts_out=True, **ssd)
    dc_h = linear_attention(dy_h, xdt_h, b_grp, ac_h, ar_h, reverse=False, name="ssd_dc", **ssd)
    db_h = linear_attention(xdt_h, dy_h, c_grp, ac_h, ar_h, reverse=True, name="ssd_db", **ssd)
    d_h = p["ssd_d"].reshape(SSD_HEADS, 1, 1)
    dx_h, e_h, v_h, ddtx_h = ssd_head_bwd(dy_h, sv["y_cross"], dxdt_h, dx_cross, xdt_h, sv["x_h"], sv["dt_h"], d_h)
    lanes = lambda a: jnp.pad(a[:, :, 0].T, ((0, 0), (0, LANES - SSD_HEADS)))
    g_chunk = ssd_state_dot(sv["s_fwd"], s_rev, ac_h[:, :, SSD_CHUNK - 1:, :])
    own = ssd_decay_grad_own(c_grp, b_grp, dy_h, xdt_h, ac_h, ar_h).reshape(SSD_HEADS, s, 1)
    da = ssd_decay_grad(lanes(e_h), lanes(v_h), lanes(g_chunk[:, :, :, 0])[:, None, :], lanes(own))
    dt_bias, a_log = _pad_lanes(p["ssd_dt_bias"]), _pad_lanes(p["ssd_a_log"])
    ddtr, dbias, dalog = ssd_dt_bwd(da, lanes(ddtx_h), proj, dt_bias, a_log)
    gs["ssd_dt_bias"], gs["ssd_a_log"] = dbias[:SSD_HEADS], dalog[:SSD_HEADS]
    db2, dc2 = ssd_group_sum(db_h, dc_h)
    dact = jnp.concatenate([_from_heads(dx_h), db2, dc2], axis=1)
    xbc = proj[:, COL_XBC:COL_XBC + XBC_WIDTH]
    us = _shifted(xbc, False) + [xbc]
    conv_b = p["ssd_conv_b"].reshape(1, XBC_WIDTH)
    dcv, gs["ssd_conv_w"], gs["ssd_conv_b"] = ssd_conv_bwd_a(us, p["ssd_conv_w"], conv_b, dact)
    dxbc = ssd_conv_bwd_b(_shifted(dcv, True) + [dcv], p["ssd_conv_w"])

    dt_cols = jnp.pad(ddtr, ((0, 0), (0, IN_PAD - COL_DT - LANES)))
    dproj = jnp.concatenate([dq, dk, dv, dg, dsq.astype(BF16), dsk.astype(BF16), dsv.astype(BF16),
                             dz, dxbc, dgl, dt_cols], axis=1)
    dh = matmul(dproj, w["w_in"], nt=True, name="mm_in_dx")
    gw_in = matmul(sv["h"].T, dproj, out_dtype=BF16, name="mm_in_dw")
    dx, gs["norm_mix_pre"] = rms_bwd(sv["x"], p["norm_mix_pre"], dh, "rms_mix_pre_bwd", resid=dx1)
    gw = dict(w_in=gw_in, w_branch=gw_branch, w_out=gw_out, w_gu=gw_gu, w_down=gw_down)
    return dx, gw, gs


def in_cols_pad(w):
    zeros = jnp.zeros(w.shape[:-1] + (IN_PAD - IN_COLS,), w.dtype)
    return jnp.concatenate([w[..., :COL_GATE], w[..., COL_GATE + SSD_HEADS:], w[..., COL_GATE:COL_GATE + SSD_HEADS], zeros],
                           axis=-1)


def in_cols_unpad(g):
    return jnp.concatenate([g[..., :COL_GATE], g[..., COL_DT:COL_DT + SSD_HEADS], g[..., COL_GATE:COL_DT]], axis=-1)


W_IN_SHARD = (D_MODEL, IN_COLS // N_SHARD)
BIG = (("w_branch_ret", (BR_WIDTH, D_MODEL // N_SHARD), True),
       ("w_branch_sb", (BR_WIDTH, D_MODEL // N_SHARD), True),
       ("w_branch_ssd", (BR_WIDTH, D_MODEL // N_SHARD), True),
       ("w_out", (D_MODEL // N_SHARD, D_MODEL), False),
       ("ffn_w_gate", (D_MODEL, FFN_HIDDEN // N_SHARD), True),
       ("ffn_w_up", (D_MODEL, FFN_HIDDEN // N_SHARD), True),
       ("ffn_w_down", (FFN_HIDDEN // N_SHARD, D_MODEL), False))
FLAT_ROWS = sum(r * c for _, (r, c), _ in BIG) // LANES
assert FLAT_ROWS * LANES == sum(r * c for _, (r, c), _ in BIG) and FLAT_ROWS % 32 == 0
HALF_ROWS = FLAT_ROWS // 2


def pack_shard(layer_shards, dtype):
    flat = jnp.concatenate([layer_shards[n].reshape(-1).astype(dtype) for n, _, _ in BIG])
    return flat.reshape(FLAT_ROWS, LANES)


def unpack_shard(flat):
    out, off = {}, 0
    v = flat.reshape(-1)
    for n, (r, c), _ in BIG:
        out[n] = v[off:off + r * c].reshape(r, c)
        off += r * c
    return out


def unpack_gathered(full):
    v = full.reshape(N_SHARD, -1)
    out, off = {}, 0
    for n, (r, c), by_cols in BIG:
        seg = v[:, off:off + r * c].reshape(N_SHARD, r, c)
        out[n] = seg.transpose(1, 0, 2).reshape(r, N_SHARD * c) if by_cols else seg.reshape(N_SHARD * r, c)
        off += r * c
    return out


def pack_for_scatter(whole):
    segs = []
    for n, (r, c), by_cols in BIG:
        g = whole[n].astype(BF16)
        g = g.reshape(r, N_SHARD, c).transpose(1, 0, 2) if by_cols else g.reshape(N_SHARD, r, c)
        segs.append(g.reshape(N_SHARD, r * c))
    return jnp.concatenate(segs, axis=1).reshape(N_SHARD, FLAT_ROWS, LANES)


ANY = pl.BlockSpec(memory_space=pl.ANY)


def _place():
    return lax.axis_index("x"), lax.axis_index("y"), lax.axis_index("c")


def _other_chips(x, y):
    return [(1 - x, y), (x, 1 - y), (1 - x, 1 - y)]


def _half(c, rows):
    return pl.ds(pl.multiple_of(c * (rows // 2), 16), rows // 2)


def _sems(n):
    return [pltpu.SemaphoreType.DMA((n,)), pltpu.SemaphoreType.DMA((n,))]


def all_gather_shards(shards):
    n = len(shards)

    def body(*refs):
        x_refs, o_refs = refs[:n], refs[n:2 * n]
        send_sems, recv_sems, local_sems = refs[2 * n:]
        x, y, c = _place()
        j = 2 * x + y
        sibling = (x, y, 1 - c)
        chips = _other_chips(x, y)

        def copy(k, src, dst, to):
            return pltpu.make_async_remote_copy(src_ref=src, dst_ref=dst, send_sem=send_sems.at[k],
                                                recv_sem=recv_sems.at[k], device_id=to, device_id_type=MESH)

        started, own = [], []
        for a, (x_ref, o_ref) in enumerate(zip(x_refs, o_refs)):
            local = pltpu.make_async_copy(x_ref, o_ref.at[j], local_sems.at[a])
            local.start()
            own.append(local)
            mine = _half(c, x_ref.shape[0])
            for k, (cx, cy) in enumerate(chips):
                cp = copy(6 * a + k, x_ref.at[mine], o_ref.at[j, mine], (cx, cy, c))
                cp.start()
                started.append(cp)
        for a, o_ref in enumerate(o_refs):
            mine = _half(c, o_ref.shape[1])
            for k, (cx, cy) in enumerate(chips):
                landed = o_ref.at[2 * cx + cy, mine]
                copy(6 * a + k, landed, landed, (cx, cy, c)).wait_recv()
                cp = copy(6 * a + 3 + k, landed, landed, sibling)
                cp.start()
                started.append(cp)
        for a, o_ref in enumerate(o_refs):
            other = _half(1 - c, o_ref.shape[1])
            for k, (cx, cy) in enumerate(chips):
                landed = o_ref.at[2 * cx + cy, other]
                copy(6 * a + 3 + k, landed, landed, sibling).wait_recv()
        for cp in started:
            cp.wait_send()
        for local in own:
            local.wait()

    return pl.pallas_call(
        body, name="all_gather_shards", in_specs=[ANY] * n, out_specs=[ANY] * n,
        out_shape=[jax.ShapeDtypeStruct((N_SHARD,) + s.shape, s.dtype) for s in shards],
        scratch_shapes=_sems(6 * n) + [pltpu.SemaphoreType.DMA((n,))],
    )(*shards)


def exchange_siblings(gs):
    n = len(gs)

    def body(*refs):
        g_refs, o_refs = refs[:n], refs[n:2 * n]
        send_sems, recv_sems = refs[2 * n:]
        x, y, c = _place()
        copies = []
        for a, (g_ref, o_ref) in enumerate(zip(g_refs, o_refs)):
            cp = pltpu.make_async_remote_copy(src_ref=g_ref.at[:, _half(1 - c, g_ref.shape[1])], dst_ref=o_ref,
                                              send_sem=send_sems.at[a], recv_sem=recv_sems.at[a],
                                              device_id=(x, y, 1 - c), device_id_type=MESH)
            cp.start()
            copies.append(cp)
        for cp in copies:
            cp.wait()

    return pl.pallas_call(
        body, name="exchange_siblings", in_specs=[ANY] * n, out_specs=[ANY] * n,
        out_shape=[jax.ShapeDtypeStruct((N_SHARD, g.shape[1] // 2, g.shape[2]), g.dtype) for g in gs],
        scratch_shapes=_sems(n),
    )(*gs)


def exchange_chips(ps):
    n = len(ps)

    def body(*refs):
        p_refs, o_refs = refs[:n], refs[n:2 * n]
        send_sems, recv_sems, local_sems = refs[2 * n:]
        x, y, c = _place()
        j = 2 * x + y
        copies = []
        for a, (p_ref, o_ref) in enumerate(zip(p_refs, o_refs)):
            local = pltpu.make_async_copy(p_ref.at[j], o_ref.at[j], local_sems.at[a])
            local.start()
            copies.append(local)
            for k, (cx, cy) in enumerate(_other_chips(x, y)):
                cp = pltpu.make_async_remote_copy(src_ref=p_ref.at[2 * cx + cy], dst_ref=o_ref.at[j],
                                                  send_sem=send_sems.at[3 * a + k], recv_sem=recv_sems.at[3 * a + k],
                                                  device_id=(cx, cy, c), device_id_type=MESH)
                cp.start()
                copies.append(cp)
        for cp in copies:
            cp.wait()

    return pl.pallas_call(
        body, name="exchange_chips", in_specs=[ANY] * n, out_specs=[ANY] * n,
        out_shape=[jax.ShapeDtypeStruct(p.shape, p.dtype) for p in ps],
        scratch_shapes=_sems(3 * n) + [pltpu.SemaphoreType.DMA((n,))],
    )(*ps)


def swap_halves(reds):
    n = len(reds)

    def body(*refs):
        r_refs, o_refs = refs[:n], refs[n:2 * n]
        send_sems, recv_sems = refs[2 * n:]
        x, y, c = _place()
        copies = []
        for a, (r_ref, o_ref) in enumerate(zip(r_refs, o_refs)):
            cp = pltpu.make_async_remote_copy(src_ref=r_ref, dst_ref=o_ref, send_sem=send_sems.at[a],
                                              recv_sem=recv_sems.at[a], device_id=(x, y, 1 - c), device_id_type=MESH)
            cp.start()
            copies.append(cp)
        for cp in copies:
            cp.wait()

    return pl.pallas_call(
        body, name="swap_halves", in_specs=[ANY] * n, out_specs=[ANY] * n,
        out_shape=[jax.ShapeDtypeStruct(r.shape, r.dtype) for r in reds],
        scratch_shapes=_sems(n),
    )(*reds)


def _sum_rows(rows, cols):
    best = 16
    for t in range(16, rows + 1, 16):
        if rows % t == 0 and t * cols * 4 <= 1536 * 1024:
            best = t
    return best


def reduce_scatter(gs):
    c = lax.axis_index("c")
    gots = exchange_siblings(gs)
    pairs = []
    for g, got in zip(gs, gots):
        half, cols = got.shape[1], got.shape[2]
        t = _sum_rows(half, cols)
        own = lax.dynamic_slice_in_dim(g, c * half, half, axis=1)
        blk = lambda a: (a, (None, t, cols), lambda j, i: (j, i, 0))
        pairs.append(bmap("sum_siblings", lambda a, b: (a.astype(F32) + b.astype(F32),), (N_SHARD, half // t),
                          [blk(own), blk(got)], [(got.shape, BF16, (None, t, cols), lambda j, i: (j, i, 0))])[0])
    reds = []
    for parts in exchange_chips(pairs):
        half, cols = parts.shape[1], parts.shape[2]
        t = _sum_rows(half, cols)
        part = lambda j: (parts, (None, t, cols), lambda i: (j, i, 0))
        reds.append(bmap("sum_chips",
                         lambda a, b, c_, d: (((a.astype(F32) + b.astype(F32)) + c_.astype(F32)) + d.astype(F32),),
                         (half // t,), [part(j) for j in range(N_SHARD)],
                         [((half, cols), F32, (t, cols), lambda i: (i, 0))])[0])
    out = []
    for mine, theirs in zip(reds, swap_halves(reds)):
        both = jnp.where(c == 0, jnp.stack([mine, theirs]), jnp.stack([theirs, mine]))
        out.append(both.reshape(2 * mine.shape[0], mine.shape[1]))
    return out


def all_reduce_small(buf):
    r = buf.shape[0]

    def body(x_ref, o_ref, land_ref, send_sems, recv_sems):
        x, y, c = _place()
        me = 4 * x + 2 * y + c
        land_ref[me] = x_ref[...]
        copies = []
        for d in range(1, 8):
            peer = tuple(1 - v if (d >> s) & 1 else v for v, s in ((x, 2), (y, 1), (c, 0)))
            cp = pltpu.make_async_remote_copy(src_ref=x_ref, dst_ref=land_ref.at[me], send_sem=send_sems.at[d - 1],
                                              recv_sem=recv_sems.at[d - 1], device_id=peer, device_id_type=MESH)
            cp.start()
            copies.append(cp)
        for cp in copies:
            cp.wait()
        acc = land_ref[0]
        for s in range(1, 8):
            acc = acc + land_ref[s]
        o_ref[...] = acc

    vmem = pl.BlockSpec(memory_space=pltpu.VMEM)
    return pl.pallas_call(
        body, name="all_reduce_small", in_specs=[vmem], out_specs=vmem,
        out_shape=jax.ShapeDtypeStruct((r, LANES), F32),
        scratch_shapes=[pltpu.VMEM((8, r, LANES), F32), pltpu.SemaphoreType.DMA((7,)), pltpu.SemaphoreType.DMA((7,))],
    )(buf)


def adamw(w, g, m, v, name):
    r, c = w.shape
    tm = r
    for cand in (512, 256, 128, 64, 32, 16, 8):
        if r % cand == 0 and cand * c * 4 * 14 <= 24 * 1024 * 1024:
            tm = cand
            break
    bc1 = 1.0 - ADAM_B1 ** ADAM_STEP
    bc2 = 1.0 - ADAM_B2 ** ADAM_STEP

    def fn(wv, gv, mv, vv):
        m2 = ADAM_B1 * mv + (1.0 - ADAM_B1) * gv
        v2 = ADAM_B2 * vv + (1.0 - ADAM_B2) * (gv * gv)
        delta = -ADAM_LR * ((m2 / bc1) / (jnp.sqrt(v2 / bc2) + ADAM_EPS) + ADAM_WD * wv)
        return delta, m2, v2

    return bmap(name, fn, (r // tm,), [rows(a, tm) for a in (w, g, m, v)], [row_out(r, c, F32, tm)] * 3)


WEIGHTS = ("norm_mix_pre", "norm_mix_post", "norm_ffn_pre", "norm_ffn_post", "w_in", "b_gate", "ret_gn_w", "ssd_conv_w",
           "ssd_conv_b", "ssd_dt_bias", "ssd_a_log", "ssd_d", "ssd_norm_w", "w_branch_ret", "w_branch_sb", "w_branch_ssd",
           "w_out", "ffn_w_gate", "ffn_w_up", "ffn_w_down")
BIG_NAMES = tuple(n for n, _, _ in BIG)
SHARDED = ("w_in",) + BIG_NAMES
SMALL = tuple(n for n in WEIGHTS if n not in SHARDED and n != "ssd_conv_w")
CONV_SHARD = XBC_WIDTH // N_SHARD


def _pack_small(parts):
    flat = jnp.concatenate([p.reshape(-1).astype(F32) for p in parts])
    rows_ = -(-flat.shape[0] // (8 * LANES)) * 8
    return jnp.pad(flat, (0, rows_ * LANES - flat.shape[0])).reshape(rows_, LANES)


def _unpack_small(buf, shapes):
    v, out, off = buf.reshape(-1), [], 0
    for shp in shapes:
        n = int(np.prod(shp))
        out.append(v[off:off + n].reshape(shp))
        off += n
    return out


def kernel(x, positions, norm_mix_pre, norm_mix_post, norm_ffn_pre, norm_ffn_post, w_in, b_gate, ret_gn_w, ssd_conv_w, ssd_conv_b, ssd_dt_bias, ssd_a_log, ssd_d, ssd_norm_w, w_branch_ret, w_branch_sb, w_branch_ssd, w_out, ffn_w_gate, ffn_w_up, ffn_w_down, loss_target, m_norm_mix_pre, m_norm_mix_post, m_norm_ffn_pre, m_norm_ffn_post, m_w_in, m_b_gate, m_ret_gn_w, m_ssd_conv_w, m_ssd_conv_b, m_ssd_dt_bias, m_ssd_a_log, m_ssd_d, m_ssd_norm_w, m_w_branch_ret, m_w_branch_sb, m_w_branch_ssd, m_w_out, m_ffn_w_gate, m_ffn_w_up, m_ffn_w_down, v_norm_mix_pre, v_norm_mix_post, v_norm_ffn_pre, v_norm_ffn_post, v_w_in, v_b_gate, v_ret_gn_w, v_ssd_conv_w, v_ssd_conv_b, v_ssd_dt_bias, v_ssd_a_log, v_ssd_d, v_ssd_norm_w, v_w_branch_ret, v_w_branch_sb, v_w_branch_ssd, v_w_out, v_ffn_w_gate, v_ffn_w_up, v_ffn_w_down):
    given = dict(locals())
    wts = {n: given[n] for n in WEIGHTS}
    mom = {n: given["m_" + n] for n in WEIGHTS}
    var = {n: given["v_" + n] for n in WEIGHTS}
    xi, yi, ci = lax.axis_index("x"), lax.axis_index("y"), lax.axis_index("c")
    shard_id = 2 * xi + yi

    conv_mine = jnp.where(ci == 0, ssd_conv_w, 0.0)
    conv_slots = lax.dynamic_update_slice_in_dim(jnp.zeros((DEPTH, SSD_CONV, XBC_WIDTH), F32), conv_mine,
                                                 shard_id * CONV_SHARD, axis=2)
    conv_buf = _pack_small([conv_slots])
    conv_whole = _unpack_small(all_reduce_small(conv_buf), [(DEPTH, SSD_CONV, XBC_WIDTH)])[0]

    tabs = rope_tables(positions[0]) + ret_decay_tables(x.shape[1])
    layer_w, layer_p = [], []
    for l in range(DEPTH):
        flat, w_in_all = all_gather_shards([pack_shard({n: wts[n][l] for n in BIG_NAMES}, BF16), w_in[l].astype(BF16)])
        full = unpack_gathered(flat)
        layer_w.append(dict(
            w_in=in_cols_pad(w_in_all.transpose(1, 0, 2).reshape(D_MODEL, IN_COLS)),
            w_branch=[full["w_branch_ret"], full["w_branch_sb"], full["w_branch_ssd"]],
            w_out=full["w_out"],
            w_gu=jnp.concatenate([full["ffn_w_gate"], full["ffn_w_up"]], axis=1),
            w_down=full["ffn_w_down"]))
        p = {n: wts[n][l] for n in SMALL}
        p["ssd_conv_w"] = conv_whole[l]
        layer_p.append(p)

    act = x[0]
    saved = []
    for l in range(DEPTH):
        act, sv = layer_forward(act, layer_w[l], layer_p[l], tabs)
        saved.append(sv)
    dact, loss_part = loss_head(act, loss_target[0])

    big_grads = [None] * DEPTH
    small_grads = [None] * DEPTH
    for l in reversed(range(DEPTH)):
        dact, gw, small_grads[l] = layer_backward(dact, saved[l], layer_w[l], layer_p[l], tabs)
        gu = gw["w_gu"]
        whole = dict(w_branch_ret=gw["w_branch"][0], w_branch_sb=gw["w_branch"][1],
                     w_branch_ssd=gw["w_branch"][2], w_out=gw["w_out"], ffn_w_gate=gu[:, :FFN_HIDDEN],
                     ffn_w_up=gu[:, FFN_HIDDEN:], ffn_w_down=gw["w_down"])
        g_in = in_cols_unpad(gw["w_in"]).reshape(D_MODEL, N_SHARD, IN_COLS // N_SHARD).transpose(1, 0, 2)
        red_flat, red_in = reduce_scatter([pack_for_scatter(whole), g_in])
        big_grads[l] = unpack_shard(red_flat)
        big_grads[l]["w_in"] = red_in

    small_names = SMALL + ("ssd_conv_w",)
    small_parts = [jnp.stack([small_grads[l][n] for l in range(DEPTH)]) for n in small_names]
    small_shapes = [p.shape for p in small_parts]
    summed = all_reduce_small(_pack_small([loss_part.reshape(1)] + small_parts))
    loss, *small_sum = _unpack_small(summed, [(1,)] + small_shapes)
    grads = dict(zip(small_names, small_sum))
    grads["ssd_conv_w"] = lax.dynamic_slice_in_dim(grads["ssd_conv_w"], shard_id * CONV_SHARD, CONV_SHARD, axis=2)
    for n in SHARDED:
        grads[n] = jnp.stack([big_grads[l][n] for l in range(DEPTH)])

    delta, new_m, new_v = {}, {}, {}
    for n in WEIGHTS:
        shape = wts[n].shape
        two_d = (-1, shape[-1]) if len(shape) == 3 else (8, -1) if shape[-1] * shape[0] % (8 * LANES) == 0 else shape
        args = [a.reshape(two_d) for a in (wts[n], grads[n], mom[n], var[n])]
        d, m2, v2 = adamw(*args, name="adamw_" + n)
        delta[n], new_m[n], new_v[n] = d.reshape(shape), m2.reshape(shape), v2.reshape(shape)

    return (loss.reshape(()), dact[None], *[grads[n] for n in WEIGHTS], *[delta[n] for n in WEIGHTS],
            *[new_m[n] for n in WEIGHTS], *[new_v[n] for n in WEIGHTS])
```

```python
import functools
import math

import numpy as np
import jax
import jax.numpy as jnp
from jax import lax
from jax.experimental import pallas as pl
from jax.experimental.pallas import tpu as pltpu

F32 = jnp.float32
BF16 = jnp.bfloat16
MESH = pl.DeviceIdType.MESH

D_MODEL = 2048
DEPTH = 4
RET_CHUNK = 64
NORM_EPS = 1e-6
HEAD = 128
N_HEADS = 8
BR_WIDTH = 1024
SSD_P = 64
SSD_HEADS = 16
SSD_GROUPS = 4
SSD_N = 128
SSD_CHUNK = 256
SSD_CONV = 4
FFN_HIDDEN = 5632
IN_COLS = 16400
IN_PAD = 16896
COL_RET, COL_SB, COL_Z, COL_XBC, COL_GATE, COL_DT = 0, 4096, 7168, 8192, 10240, 16384
N_SHARD = 4
LANES = 128
VMEM_LIMIT = 48 * 1024 * 1024

ADAM_LR, ADAM_B1, ADAM_B2, ADAM_EPS, ADAM_WD, ADAM_STEP = 0.001, 0.9, 0.999, 1e-08, 0.01, 10


def _pick(n, prefs):
    for p in prefs:
        if n % p == 0:
            return p
    return n


def _params(sem=None):
    kw = dict(vmem_limit_bytes=VMEM_LIMIT)
    if sem is not None:
        kw["dimension_semantics"] = sem
    return pltpu.CompilerParams(**kw)


MATMUL_VMEM = 36 * 1024 * 1024


def _matmul_tiles(m, n, k, out_bytes):
    def options(dim, prefs):
        opts = [p for p in prefs if dim % p == 0]
        return opts or [dim]

    best, best_vol = None, -1
    for tm in options(m, (1408, 1024, 512, 256, 128)):
        for tn in options(n, (1536, 1408, 1024, 512, 256, 128)):
            for tk in options(k, (2048, 1536, 1408, 1024, 512, 256, 128)):
                need = 2 * 2 * (tm * tk + tk * tn) + tm * tn * (4 + 2 * out_bytes)
                vol = tm * tn * tk
                if need <= MATMUL_VMEM and vol > best_vol:
                    best, best_vol = (tm, tn, tk), vol
    return best


def matmul(a, b, *, nt=False, out_dtype=F32, name="matmul"):
    m, k = a.shape
    n = b.shape[0] if nt else b.shape[1]
    tm, tn, tk = _matmul_tiles(m, n, k, jnp.dtype(out_dtype).itemsize)
    nk = k // tk

    def body(a_ref, b_ref, o_ref, acc_ref):
        kk = pl.program_id(2)
        if nt:
            part = lax.dot_general(a_ref[...], b_ref[...], (((1,), (1,)), ((), ())), preferred_element_type=F32)
        else:
            part = jnp.dot(a_ref[...], b_ref[...], preferred_element_type=F32)

        @pl.when(kk == 0)
        def _():
            acc_ref[...] = part

        @pl.when(kk > 0)
        def _():
            acc_ref[...] += part

        @pl.when(kk == nk - 1)
        def _():
            o_ref[...] = acc_ref[...].astype(o_ref.dtype)

    b_spec = pl.BlockSpec((tn, tk), lambda i, j, kk: (j, kk)) if nt else pl.BlockSpec((tk, tn), lambda i, j, kk: (kk, j))
    return pl.pallas_call(
        body, name=name, grid=(m // tm, n // tn, nk),
        in_specs=[pl.BlockSpec((tm, tk), lambda i, j, kk: (i, kk)), b_spec],
        out_specs=pl.BlockSpec((tm, tn), lambda i, j, kk: (i, j)),
        out_shape=jax.ShapeDtypeStruct((m, n), out_dtype),
        scratch_shapes=[pltpu.VMEM((tm, tn), F32)],
        compiler_params=_params(("parallel", "parallel", "arbitrary")),
    )(a, b)


def bmap(name, fn, grid, ins, outs, accs=()):
    n_in, n_out = len(ins), len(outs)
    last = len(grid) - 1

    def store_acc(ref, val, first):
        @pl.when(first)
        def _():
            ref[...] = val

        @pl.when(jnp.logical_not(first))
        def _():
            ref[...] += val

    def body(*refs):
        vals = fn(*[r[...] for r in refs[:n_in]])
        if not isinstance(vals, (tuple, list)):
            vals = (vals,)
        for r, v in zip(refs[n_in:n_in + n_out], vals[:n_out]):
            r[...] = v.astype(r.dtype)
        first = pl.program_id(last) == 0
        for r, v in zip(refs[n_in + n_out:], vals[n_out:]):
            store_acc(r, v.astype(F32), first)

    sem = ("parallel",) * last + (("arbitrary",) if accs else ("parallel",))
    res = pl.pallas_call(
        body, name=name, grid=grid,
        in_specs=[pl.BlockSpec(bs, im) for (_, bs, im) in ins],
        out_specs=[pl.BlockSpec(bs, im) for (_, _, bs, im) in outs] + [pl.BlockSpec(bs, im) for (_, bs, im) in accs],
        out_shape=[jax.ShapeDtypeStruct(s, dt) for (s, dt, _, _) in outs] + [jax.ShapeDtypeStruct(s, F32) for (s, _, _) in accs],
        compiler_params=_params(sem),
    )(*[a for (a, _, _) in ins])
    return res


def rows(a, tm, width=None, col=0):
    width = a.shape[1] if width is None else width
    assert col % width == 0
    cb = col // width
    return (a, (tm, width), lambda i: (i, cb))


def par(a):
    return (a, a.shape, lambda *_: (0,) * a.ndim)


def row_out(s, width, dtype, tm):
    return ((s, width), dtype, (tm, width), lambda i: (i, 0))


def col_acc(width):
    return ((1, width), (1, width), lambda i: (0, 0))


def _sigmoid(x):
    return 1.0 / (1.0 + jnp.exp(-x))


def _silu(x):
    return x * _sigmoid(x)


def _dsilu(x):
    s = _sigmoid(x)
    return s * (1.0 + x * (1.0 - s))


def _softplus(x):
    return jnp.maximum(x, 0.0) + jnp.log(1.0 + jnp.exp(-jnp.abs(x)))


def _split_bf16(x):
    hi = x.astype(BF16)
    lo = (x - hi.astype(F32)).astype(BF16)
    return hi, lo


def _dot_exact_mask(x, mask_bf16):
    hi, lo = _split_bf16(x)
    return (jnp.dot(hi, mask_bf16, preferred_element_type=F32)
            + jnp.dot(lo, mask_bf16, preferred_element_type=F32))


def rms_fwd(x, w, out_dtype, name, resid=None):
    s, d = x.shape
    tm = _pick(s, (256, 128))

    def fn(xv, wv, *rest):
        r = lax.rsqrt(jnp.mean(xv * xv, axis=-1, keepdims=True) + NORM_EPS)
        y = xv * r * wv
        if rest:
            y = y + rest[0]
        return (y,)

    ins = [rows(x, tm), par(w.reshape(1, d))]
    if resid is not None:
        ins.append(rows(resid, tm))
    return bmap(name, fn, (s // tm,), ins, [row_out(s, d, out_dtype, tm)])[0]


def rms_bwd(x, w, dy, name, resid=None, out_dtype=F32):
    s, d = x.shape
    tm = _pick(s, (256, 128))

    def fn(xv, wv, dyv, *rest):
        dyv = dyv.astype(F32)
        r = lax.rsqrt(jnp.mean(xv * xv, axis=-1, keepdims=True) + NORM_EPS)
        xn = xv * r
        g = dyv * wv
        dx = r * (g - xn * jnp.mean(g * xn, axis=-1, keepdims=True))
        if rest:
            dx = dx + rest[0]
        return dx, jnp.sum(dyv * xn, axis=0, keepdims=True)

    ins = [rows(x, tm), par(w.reshape(1, d)), rows(dy, tm)]
    if resid is not None:
        ins.append(rows(resid, tm))
    dx, dw = bmap(name, fn, (s // tm,), ins, [row_out(s, d, out_dtype, tm)], [col_acc(d)])
    return dx, dw[0]


def linear_attention(q, k, v, ac, ar, *, chunk, causal, reverse, heads, name, out_dtype=F32, out_2d=False,
                     parts_out=False, sub_chunk=None):
    def dims(op):
        return (op[0].shape[0], HEAD) if isinstance(op, tuple) else (op.shape[1], op.shape[2])

    s_len, dk = dims(q)
    dv = dims(v)[1]
    L = chunk
    cpb = _pick(s_len // L, (8, 4, 2, 1))
    nb = s_len // (L * cpb)
    a_chunks = ac.shape[1]

    def blk(i):
        return nb - 1 - i if reverse else i

    def seq_spec(op):
        if isinstance(op, tuple):
            _, first_col, nh = op
            rep, cb = heads // nh, first_col // HEAD
            return pl.BlockSpec((L * cpb, HEAD), lambda h, i: (blk(i), cb + h // rep))
        rep = heads // op.shape[0]
        return pl.BlockSpec((None, L * cpb, op.shape[2]), lambda h, i: (h // rep, blk(i), 0))

    def a_spec(arr, shape):
        rep = heads // arr.shape[0]
        if a_chunks == 1:
            return pl.BlockSpec((None, 1) + shape, lambda h, i: (h // rep, 0, 0, 0))
        return pl.BlockSpec((None, cpb) + shape, lambda h, i: (h // rep, blk(i), 0, 0))

    def body(q_ref, k_ref, v_ref, ac_ref, ar_ref, *rest):
        o_refs, state_ref = rest[:-1], rest[-1]

        @pl.when(pl.program_id(1) == 0)
        def _():
            state_ref[...] = jnp.zeros_like(state_ref)

        row = lax.broadcasted_iota(jnp.int32, (L, L), 0)
        col = lax.broadcasted_iota(jnp.int32, (L, L), 1)

        def decays(ai):
            a_col = ac_ref[ai]
            a_row = ar_ref[ai]
            a_last = ac_ref[ai, pl.ds(L - 1, 1), :]
            seg = a_col - a_row
            if not causal:
                dec = jnp.exp(-jnp.abs(seg))
                if sub_chunk is not None:
                    shift = sub_chunk.bit_length() - 1
                    rs, cs = jnp.right_shift(row, shift), jnp.right_shift(col, shift)
                    dec = jnp.where(cs >= rs if reverse else cs <= rs, dec, 0.0)
            elif reverse:
                dec = jnp.where(col > row, jnp.exp(jnp.minimum(-seg, 0.0)), 0.0)
            else:
                dec = jnp.where(row > col, jnp.exp(jnp.minimum(seg, 0.0)), 0.0)
            if reverse:
                qd, kd = jnp.exp(a_last - a_col), jnp.exp(a_col)
            else:
                qd, kd = jnp.exp(a_col), jnp.exp(a_last - a_col)
            return dec, qd, kd, jnp.exp(a_last)

        fixed = decays(0) if a_chunks == 1 else None

        def one(step, carry):
            c = cpb - 1 - step if reverse else step
            sl = pl.ds(pl.multiple_of(c * L, L), L)
            qc = q_ref[sl, :].astype(F32)
            kc = k_ref[sl, :].astype(F32)
            vf = v_ref[sl, :].astype(F32)
            vc = vf.astype(BF16)
            dec, qd, kd, chunk_decay = fixed if a_chunks == 1 else decays(c)
            sc = lax.dot_general(qc.astype(BF16), kc.astype(BF16), (((1,), (1,)), ((), ())),
                                 preferred_element_type=F32) * dec
            state = state_ref[...]
            intra = jnp.dot(sc.astype(BF16), vc, preferred_element_type=F32)
            cross = jnp.dot((qc * qd).astype(BF16), state.astype(BF16), preferred_element_type=F32)
            out = intra + cross
            if parts_out:
                o_refs[1][sl, :] = cross
                o_refs[2][c] = state
            if causal:
                out = out + jnp.sum(qc * kc, axis=-1, keepdims=True) * vf
            o_refs[0][sl, :] = out.astype(o_refs[0].dtype)
            upd = lax.dot_general((kc * kd).astype(BF16), vc, (((0,), (0,)), ((), ())),
                                  preferred_element_type=F32)
            state_ref[...] = state * chunk_decay + upd
            return carry

        lax.fori_loop(0, cpb, one, 0)

    if out_2d:
        out_spec = pl.BlockSpec((L * cpb, dv), lambda h, i: (blk(i), h))
        out_shape = jax.ShapeDtypeStruct((s_len, heads * dv), out_dtype)
    else:
        out_spec = pl.BlockSpec((None, L * cpb, dv), lambda h, i: (h, blk(i), 0))
        out_shape = jax.ShapeDtypeStruct((heads, s_len, dv), out_dtype)
    out_specs, out_shapes = [out_spec], [out_shape]
    if parts_out:
        assert not out_2d and out_dtype == F32
        out_specs += [out_spec, pl.BlockSpec((None, cpb, dk, dv), lambda h, i: (h, blk(i), 0, 0))]
        out_shapes += [out_shape, jax.ShapeDtypeStruct((heads, s_len // L, dk, dv), F32)]
    arrs = [op[0] if isinstance(op, tuple) else op for op in (q, k, v)]
    res = pl.pallas_call(
        body, name=name, grid=(heads, nb),
        in_specs=[seq_spec(q), seq_spec(k), seq_spec(v), a_spec(ac, (L, 1)), a_spec(ar, (1, L))],
        out_specs=out_specs, out_shape=out_shapes,
        scratch_shapes=[pltpu.VMEM((dk, dv), F32)],
        compiler_params=_params(("parallel", "arbitrary")),
    )(*arrs, ac, ar)
    return res if parts_out else res[0]


SB_KEYS = 256


def _sb_tile(q, kb, scale, valid, tri):
    z = lax.dot_general(q, kb, (((1,), (1,)), ((), ())), preferred_element_type=F32) * scale
    sp = _softplus(z)
    log_beta = z - sp
    lk = -sp if valid is None else jnp.where(valid, -sp, 0.0)
    later_in = _dot_exact_mask(lk, tri)
    return log_beta, lk, log_beta + later_in


def _sb_loops(i, ratio, step, init):
    carry = lax.fori_loop(0, ratio, lambda n, c: step((i + 1) * ratio - 1 - n, True, c), init)
    return lax.fori_loop(0, i * ratio, lambda n, c: step(i * ratio - 1 - n, False, c), carry)


def _sb_masks(i, bq):
    bk = SB_KEYS
    row = lax.broadcasted_iota(jnp.int32, (bk, bk), 0)
    col = lax.broadcasted_iota(jnp.int32, (bk, bk), 1)
    qpos = i * bq + lax.broadcasted_iota(jnp.int32, (bq, bk), 0)
    kcol = lax.broadcasted_iota(jnp.int32, (bq, bk), 1)
    return row, col, lambda kbi: (kbi * bk + kcol) < qpos


def sb_forward(qkv, name="sb_fwd"):
    s_len = qkv.shape[0]
    bk = SB_KEYS
    bq = _pick(s_len, (512, 256))
    scale = HEAD ** -0.5

    def body(q_ref, k_ref, v_ref, o_ref):
        i = pl.program_id(1)
        q = q_ref[...]
        row, col, valid_of = _sb_masks(i, bq)
        tri_gt = (row > col).astype(BF16)

        def step(kbi, masked, carry):
            acc, a_run = carry
            sl = pl.ds(pl.multiple_of(kbi * bk, bk), bk)
            valid = valid_of(kbi) if masked else None
            _, lk, expo = _sb_tile(q, k_ref[sl, :], scale, valid, tri_gt)
            w = jnp.exp(expo + a_run)
            if masked:
                w = jnp.where(valid, w, 0.0)
            acc = acc + jnp.dot(w.astype(BF16), v_ref[sl, :], preferred_element_type=F32)
            return acc, a_run + jnp.sum(lk, axis=-1, keepdims=True)

        acc, _ = _sb_loops(i, bq // bk, step, (jnp.zeros((bq, HEAD), F32), jnp.zeros((bq, 1), F32)))
        o_ref[...] = acc

    return pl.pallas_call(
        body, name=name, grid=(N_HEADS, s_len // bq),
        in_specs=[pl.BlockSpec((bq, HEAD), lambda h, i: (i, h)),
                  pl.BlockSpec((s_len, HEAD), lambda h, i: (0, N_HEADS + h)),
                  pl.BlockSpec((s_len, HEAD), lambda h, i: (0, 2 * N_HEADS + h))],
        out_specs=pl.BlockSpec((bq, HEAD), lambda h, i: (i, h)),
        out_shape=jax.ShapeDtypeStruct((s_len, BR_WIDTH), F32),
        compiler_params=_params(("parallel", "arbitrary")),
    )(qkv, qkv, qkv)


def sb_backward(qkv, o, do, name="sb_bwd"):
    s_len = qkv.shape[0]
    bk = SB_KEYS
    bq = _pick(s_len, (512, 256))
    scale = HEAD ** -0.5

    def body(q_ref, k_ref, v_ref, o_ref, do_ref, dq_ref, dk_ref, dv_ref):
        i = pl.program_id(1)

        @pl.when(i == 0)
        def _():
            dk_ref[...] = jnp.zeros_like(dk_ref)
            dv_ref[...] = jnp.zeros_like(dv_ref)

        q = q_ref[...]
        dob = do_ref[...].astype(BF16)
        delta = jnp.sum(dob.astype(F32) * o_ref[...], axis=-1, keepdims=True)
        row, col, valid_of = _sb_masks(i, bq)
        tri_gt = (row > col).astype(BF16)
        tri_ge = (row >= col).astype(BF16)

        def step(kbi, masked, carry):
            dq, a_run, e_run = carry
            sl = pl.ds(pl.multiple_of(kbi * bk, bk), bk)
            valid = valid_of(kbi) if masked else None
            kb = k_ref[sl, :]
            vb = v_ref[sl, :]
            log_beta, lk, expo = _sb_tile(q, kb, scale, valid, tri_gt)
            w = jnp.exp(expo + a_run)
            if masked:
                w = jnp.where(valid, w, 0.0)
            wb = w.astype(BF16)
            beta = jnp.exp(log_beta)
            dw = lax.dot_general(dob, vb, (((1,), (1,)), ((), ())), preferred_element_type=F32)
            e = dw * wb.astype(F32)
            suffix = e_run + _dot_exact_mask(e, tri_ge)
            dz = (e * (1.0 - beta) - beta * (delta - suffix)) * scale
            if masked:
                dz = jnp.where(valid, dz, 0.0)
            dzb = dz.astype(BF16)
            dq = dq + jnp.dot(dzb, kb, preferred_element_type=F32)
            dk_ref[sl, :] += lax.dot_general(dzb, q, (((0,), (0,)), ((), ())), preferred_element_type=F32)
            dv_ref[sl, :] += lax.dot_general(wb, dob, (((0,), (0,)), ((), ())),
                                             preferred_element_type=F32)
            return (dq, a_run + jnp.sum(lk, axis=-1, keepdims=True),
                    e_run + jnp.sum(e, axis=-1, keepdims=True))

        zero = jnp.zeros((bq, 1), F32)
        dq, _, _ = _sb_loops(i, bq // bk, step, (jnp.zeros((bq, HEAD), F32), zero, zero))
        dq_ref[...] = dq

    blk = pl.BlockSpec((bq, HEAD), lambda h, i: (i, h))
    full = pl.BlockSpec((s_len, HEAD), lambda h, i: (0, h))
    shape = jax.ShapeDtypeStruct((s_len, BR_WIDTH), F32)
    return pl.pallas_call(
        body, name=name, grid=(N_HEADS, s_len // bq),
        in_specs=[blk,
                  pl.BlockSpec((s_len, HEAD), lambda h, i: (0, N_HEADS + h)),
                  pl.BlockSpec((s_len, HEAD), lambda h, i: (0, 2 * N_HEADS + h)),
                  blk, blk],
        out_specs=[blk, full, full],
        out_shape=[shape, shape, shape],
        compiler_params=_params(("parallel", "arbitrary")),
    )(qkv, qkv, qkv, o, do)


ROPE_BASE = 10000.0


def rope_tables(positions):
    half = HEAD // 2
    inv_freq = ROPE_BASE ** (-2.0 * jnp.arange(half, dtype=F32) / HEAD)
    ang = positions.astype(F32)[:, None] * inv_freq
    cos, sin = jnp.cos(ang), jnp.sin(ang)
    return jnp.concatenate([cos, cos], 1), jnp.concatenate([-sin, sin], 1)


def ret_decay_tables(s_len):
    block = _pick(s_len, (512, 256, 128, RET_CHUNK))
    lg = np.log1p(-np.exp2(-5.0 - np.arange(N_HEADS, dtype=np.float64)))
    ac = (lg[:, None] * (np.arange(block) + 1.0)[None, :]).astype(np.float32)
    return jnp.asarray(ac[:, None, :, None]), jnp.asarray(ac[:, None, None, :])


def _head_major(s, tm, d=HEAD, heads=N_HEADS, dtype=F32):
    return ((heads, s, d), dtype, (None, tm, d), lambda h, i: (h, i, 0))


def _head_cols(arr, tm, first_col):
    cb = first_col // HEAD
    return (arr, (tm, HEAD), lambda h, i: (i, cb + h))


def _head_rows(arr, tm):
    return (arr, (None, tm, arr.shape[2]), lambda h, i: (h, i, 0))


def rotary_fwd(proj, cos2, sin2):
    s = proj.shape[0]
    tm = _pick(s, (512, 256, 128))

    def fn(qv, kv, c, sn):
        def rot(t):
            return t * c + pltpu.roll(t, HEAD // 2, 1) * sn
        return rot(qv), rot(kv) * (HEAD ** -0.5)

    tab = lambda t: (t, (tm, HEAD), lambda h, i: (i, 0))
    return bmap("rotary_fwd", fn, (N_HEADS, s // tm),
                [_head_cols(proj, tm, COL_RET), _head_cols(proj, tm, COL_RET + BR_WIDTH), tab(cos2), tab(sin2)],
                [_head_major(s, tm), _head_major(s, tm)])


def rotary_bwd(dqr, dkr, cos2, sin2):
    s = dqr.shape[1]
    tm = _pick(s, (512, 256, 128))

    def fn(dq, dk, c, sn):
        def unrot(t):
            return t * c - pltpu.roll(t, HEAD // 2, 1) * sn
        return unrot(dq), unrot(dk) * (HEAD ** -0.5)

    tab = lambda t: (t, (tm, HEAD), lambda h, i: (i, 0))
    out = ((s, BR_WIDTH), BF16, (tm, HEAD), lambda h, i: (i, h))
    return bmap("rotary_bwd", fn, (N_HEADS, s // tm),
                [_head_rows(dqr, tm), _head_rows(dkr, tm), tab(cos2), tab(sin2)], [out, out])


def _group_norm(o):
    mu = jnp.mean(o, axis=-1, keepdims=True)
    oc = o - mu
    r = lax.rsqrt(jnp.mean(oc * oc, axis=-1, keepdims=True) + NORM_EPS)
    return oc * r, r


def gn_gate_fwd(o, gn_w, proj):
    s = o.shape[1]
    tm = _pick(s, (512, 256, 128))

    def fn(ov, wv, gv):
        on, _ = _group_norm(ov)
        return (on * wv * _silu(gv),)

    w3 = gn_w.reshape(N_HEADS, 1, HEAD)
    return bmap("gn_gate_fwd", fn, (N_HEADS, s // tm),
                [_head_rows(o, tm), (w3, (None, 1, HEAD), lambda h, i: (h, 0, 0)),
                 _head_cols(proj, tm, COL_RET + 3 * BR_WIDTH)],
                [((s, BR_WIDTH), BF16, (tm, HEAD), lambda h, i: (i, h))])[0]


def gn_gate_bwd(o, gn_w, proj, dy):
    s = o.shape[1]
    tm = _pick(s, (512, 256, 128))

    def fn(ov, wv, gv, dyv):
        on, r = _group_norm(ov)
        yn = on * wv
        dyn = dyv * _silu(gv)
        dg = dyv * yn * _dsilu(gv)
        don = dyn * wv
        do = r * (don - jnp.mean(don, axis=-1, keepdims=True) - on * jnp.mean(don * on, axis=-1, keepdims=True))
        return do, dg, jnp.sum(dyn * on, axis=0, keepdims=True)

    w3 = gn_w.reshape(N_HEADS, 1, HEAD)
    do, dg, dw = bmap("gn_gate_bwd", fn, (N_HEADS, s // tm),
                      [_head_rows(o, tm), (w3, (None, 1, HEAD), lambda h, i: (h, 0, 0)),
                       _head_cols(proj, tm, COL_RET + 3 * BR_WIDTH), _head_cols(dy, tm, 0)],
                      [_head_major(s, tm), ((s, BR_WIDTH), BF16, (tm, HEAD), lambda h, i: (i, h))],
                      [((N_HEADS, 1, HEAD), (None, 1, HEAD), lambda h, i: (h, 0, 0))])
    return do, dg, dw.reshape(BR_WIDTH)


XBC_WIDTH = 2048


def _shifted(u, back):
    s = u.shape[0]
    n = SSD_CONV - 1
    if back:
        return [jnp.pad(u, ((0, n - k), (0, 0)))[n - k:] for k in range(n)]
    return [jnp.pad(u, ((n - k, 0), (0, 0)))[:s] for k in range(n)]


def _conv(us, w, b):
    return us[0] * w[0:1] + us[1] * w[1:2] + us[2] * w[2:3] + us[3] * w[3:4] + b


def ssd_conv_fwd(us, conv_w, conv_b):
    s = us[0].shape[0]
    tm = _pick(s, (256, 128))

    def fn(u0, u1, u2, u3, w, b):
        return (_silu(_conv((u0, u1, u2, u3), w, b)),)

    return bmap("ssd_conv_fwd", fn, (s // tm,), [rows(u, tm) for u in us] + [par(conv_w), par(conv_b)],
                [row_out(s, XBC_WIDTH, F32, tm)])[0]


def ssd_conv_bwd_a(us, conv_w, conv_b, dact):
    s = dact.shape[0]
    tm = _pick(s, (256, 128))

    def fn(u0, u1, u2, u3, w, b, da):
        dcv = da * _dsilu(_conv((u0, u1, u2, u3), w, b))
        sums = [jnp.sum(dcv * u, axis=0, keepdims=True) for u in (u0, u1, u2, u3)]
        return (dcv, *sums, jnp.sum(dcv, axis=0, keepdims=True))

    res = bmap("ssd_conv_bwd_a", fn, (s // tm,),
               [rows(u, tm) for u in us] + [par(conv_w), par(conv_b), rows(dact, tm)],
               [row_out(s, XBC_WIDTH, F32, tm)], [col_acc(XBC_WIDTH)] * 5)
    return res[0], jnp.concatenate(res[1:5], axis=0), res[5][0]


def ssd_conv_bwd_b(dcs, conv_w):
    s = dcs[0].shape[0]
    tm = _pick(s, (256, 128))

    def fn(d0, d1, d2, d3, w):
        return (d0 * w[0:1] + d1 * w[1:2] + d2 * w[2:3] + d3 * w[3:4],)

    return bmap("ssd_conv_bwd_b", fn, (s // tm,), [rows(d, tm) for d in dcs] + [par(conv_w)],
                [row_out(s, XBC_WIDTH, BF16, tm)])[0]


def _mask_dot_exact(mask_bf16, x):
    hi = x.astype(BF16)
    r1 = x - hi.astype(F32)
    mid = r1.astype(BF16)
    lo = (r1 - mid.astype(F32)).astype(BF16)
    return (jnp.dot(mask_bf16, hi, preferred_element_type=F32) + jnp.dot(mask_bf16, mid, preferred_element_type=F32)
            + jnp.dot(mask_bf16, lo, preferred_element_type=F32))


def ssd_dt_fwd(proj, dt_bias, a_log):
    s = proj.shape[0]
    L = SSD_CHUNK

    def fn(dtr, bias, alog):
        dt = _softplus(dtr + bias)
        a = dt * (-jnp.exp(alog))
        row = lax.broadcasted_iota(jnp.int32, (L, L), 0)
        col = lax.broadcasted_iota(jnp.int32, (L, L), 1)
        return dt, _mask_dot_exact((row >= col).astype(BF16), a)

    return bmap("ssd_dt_fwd", fn, (s // L,), [rows(proj, L, LANES, COL_DT), par(dt_bias), par(a_log)],
                [row_out(s, LANES, F32, L), row_out(s, LANES, F32, L)])


def ssd_state_dot(s_fwd, s_rev, a_tot):
    heads, nc = s_fwd.shape[0], s_fwd.shape[1]
    cb = _pick(nc, (8, 4, 2, 1))

    def fn(sf, sr, at):
        return (jnp.sum(jnp.sum(sf * sr, axis=2, keepdims=True), axis=1, keepdims=True) * jnp.exp(at),)

    st = lambda a: (a, (None, cb) + a.shape[2:], lambda h, i: (h, i, 0, 0))
    return bmap("ssd_state_dot", fn, (heads, nc // cb), [st(s_fwd), st(s_rev), st(a_tot)],
                [((heads, nc, 1, 1), F32, (None, cb, 1, 1), lambda h, i: (h, i, 0, 0))])[0]


def ssd_decay_grad(e_incl, v_excl, g_chunk, own):
    s = e_incl.shape[0]
    L = SSD_CHUNK

    def fn(ev, vv, gv, ov):
        row = lax.broadcasted_iota(jnp.int32, (L, L), 0)
        col = lax.broadcasted_iota(jnp.int32, (L, L), 1)
        return (_mask_dot_exact((col >= row).astype(BF16), ev) + _mask_dot_exact((col < row).astype(BF16), vv)
                + gv + ov,)

    return bmap("ssd_decay_grad", fn, (s // L,),
                [rows(e_incl, L), rows(v_excl, L), (g_chunk, (None, 1, LANES), lambda i: (i, 0, 0)), rows(own, L)],
                [row_out(s, LANES, F32, L)])[0]


def ssd_decay_grad_own(c_grp, b_grp, dy_h, xdt_h, ac, ar):
    s_len = dy_h.shape[1]
    L = SSD_CHUNK
    nc = s_len // L
    cpb = _pick(nc, (8, 4, 2, 1))
    rep = SSD_HEADS // SSD_GROUPS

    def grp_spec(op):
        cb = op[1] // HEAD
        return pl.BlockSpec((L * cpb, HEAD), lambda h, i: (i, cb + h // rep))

    def body(c_ref, b_ref, dy_ref, x_ref, ac_ref, ar_ref, o_ref):
        row = lax.broadcasted_iota(jnp.int32, (L, L), 0)
        col = lax.broadcasted_iota(jnp.int32, (L, L), 1)
        earlier = (row < col).astype(BF16)

        def one(c, carry):
            sl = pl.ds(pl.multiple_of(c * L, L), L)
            seg = ac_ref[c] - ar_ref[c]
            dec = jnp.where(row > col, jnp.exp(jnp.minimum(seg, 0.0)), 0.0)
            nt = (((1,), (1,)), ((), ()))
            sc = lax.dot_general(c_ref[sl, :].astype(BF16), b_ref[sl, :].astype(BF16), nt, preferred_element_type=F32)
            dx = lax.dot_general(dy_ref[sl, :].astype(BF16), x_ref[sl, :].astype(BF16), nt, preferred_element_type=F32)
            before_r = _dot_exact_mask(sc * dec * dx, earlier)
            o_ref[c] = jnp.sum(jnp.where(row >= col, before_r, 0.0), axis=0, keepdims=True)
            return carry

        lax.fori_loop(0, cpb, one, 0)

    seq = lambda a: pl.BlockSpec((None, L * cpb, a.shape[2]), lambda h, i: (h, i, 0))
    dec_spec = lambda shape: pl.BlockSpec((None, cpb) + shape, lambda h, i: (h, i, 0, 0))
    return pl.pallas_call(
        body, name="ssd_decay_grad_own", grid=(SSD_HEADS, nc // cpb),
        in_specs=[grp_spec(c_grp), grp_spec(b_grp), seq(dy_h), seq(xdt_h), dec_spec((L, 1)), dec_spec((1, L))],
        out_specs=pl.BlockSpec((None, cpb, 1, L), lambda h, i: (h, i, 0, 0)),
        out_shape=jax.ShapeDtypeStruct((SSD_HEADS, nc, 1, L), F32),
        compiler_params=_params(("parallel", "parallel")),
    )(c_grp[0], b_grp[0], dy_h, xdt_h, ac, ar)


def ssd_dt_bwd(da, ddt_x, proj, dt_bias, a_log):
    s = da.shape[0]
    tm = _pick(s, (512, 256, 128))

    def fn(dav, dxv, dtr, bias, alog):
        pre = dtr + bias
        a_coef = -jnp.exp(alog)
        ddtr = (dav * a_coef + dxv) * _sigmoid(pre)
        return (ddtr, jnp.sum(ddtr, axis=0, keepdims=True),
                jnp.sum(dav * _softplus(pre) * a_coef, axis=0, keepdims=True))

    ddtr, dbias, dalog = bmap("ssd_dt_bwd", fn, (s // tm,),
                              [rows(da, tm), rows(ddt_x, tm), rows(proj, tm, LANES, COL_DT), par(dt_bias), par(a_log)],
                              [row_out(s, LANES, BF16, tm)], [col_acc(LANES), col_acc(LANES)])
    return ddtr, dbias[0], dalog[0]


def ssd_xdt(x_h, dt_h):
    s = x_h.shape[1]
    tm = _pick(s, (512, 256, 128))
    return bmap("ssd_xdt", lambda xv, dv: (xv * dv,), (SSD_HEADS, s // tm),
                [_head_rows(x_h, tm), _head_rows(dt_h, tm)], [_head_major(s, tm, SSD_P, SSD_HEADS)])[0]


def ssd_gate_fwd(ytil, act, proj, d_full, norm_w):
    s = ytil.shape[0]
    tm = _pick(s, (256, 128))

    def fn(yt, xv, zv, dv, wv):
        u = (yt + xv * dv) * _silu(zv)
        return (u * lax.rsqrt(jnp.mean(u * u, axis=-1, keepdims=True) + NORM_EPS) * wv,)

    return bmap("ssd_gate_fwd", fn, (s // tm,),
                [rows(ytil, tm), rows(act, tm, BR_WIDTH, 0), rows(proj, tm, BR_WIDTH, COL_Z), par(d_full), par(norm_w)],
                [row_out(s, BR_WIDTH, BF16, tm)])[0]


def ssd_gate_bwd(ytil, act, proj, d_full, norm_w, dout):
    s = ytil.shape[0]
    tm = _pick(s, (256, 128))

    def fn(yt, xv, zv, dv, wv, dov):
        y = yt + xv * dv
        sz = _silu(zv)
        u = y * sz
        r = lax.rsqrt(jnp.mean(u * u, axis=-1, keepdims=True) + NORM_EPS)
        un = u * r
        g = dov * wv
        du = r * (g - un * jnp.mean(g * un, axis=-1, keepdims=True))
        dy = du * sz
        return (dy, du * y * _dsilu(zv), jnp.sum(dov * un, axis=0, keepdims=True),
                jnp.sum(dy * xv, axis=0, keepdims=True))

    dy, dz, dw, dd = bmap("ssd_gate_bwd", fn, (s // tm,),
                          [rows(ytil, tm), rows(act, tm, BR_WIDTH, 0), rows(proj, tm, BR_WIDTH, COL_Z), par(d_full),
                           par(norm_w), rows(dout, tm)],
                          [row_out(s, BR_WIDTH, F32, tm), row_out(s, BR_WIDTH, BF16, tm)],
                          [col_acc(BR_WIDTH), col_acc(BR_WIDTH)])
    return dy, dz, dw[0], dd[0]


def ssd_head_bwd(dy_h, y_cross, dxdt_h, dx_cross, xdt_h, x_h, dt_h, d_h):
    s = dy_h.shape[1]
    tm = _pick(s, (512, 256, 128))

    def fn(dy, yc, dxdt, dxc, xdt, xv, dt, dskip):
        return (dy * dskip + dxdt * dt,
                jnp.sum(dy * yc, axis=-1, keepdims=True),
                jnp.sum(dxc * xdt, axis=-1, keepdims=True),
                jnp.sum(dxdt * xv, axis=-1, keepdims=True))

    one = ((SSD_HEADS, s, 1), F32, (None, tm, 1), lambda h, i: (h, i, 0))
    return bmap("ssd_head_bwd", fn, (SSD_HEADS, s // tm),
                [_head_rows(a, tm) for a in (dy_h, y_cross, dxdt_h, dx_cross, xdt_h, x_h, dt_h)]
                + [(d_h, (None, 1, 1), lambda h, i: (h, 0, 0))],
                [_head_major(s, tm, SSD_P, SSD_HEADS), one, one, one])


def ssd_group_sum(db_h, dc_h):
    s = db_h.shape[1]
    tm = _pick(s, (512, 256, 128))
    e = SSD_HEADS // SSD_GROUPS

    def fn(*v):
        return v[0] + v[1] + v[2] + v[3], v[4] + v[5] + v[6] + v[7]

    def head(arr, j):
        return (arr, (None, tm, SSD_N), lambda g, i: (g * e + j, i, 0))

    out = ((s, SSD_GROUPS * SSD_N), F32, (tm, SSD_N), lambda g, i: (i, g))
    return bmap("ssd_group_sum", fn, (SSD_GROUPS, s // tm),
                [head(db_h, j) for j in range(e)] + [head(dc_h, j) for j in range(e)], [out, out])


def _gate_ins(proj, b_gate, tm):
    b2 = b_gate.reshape(1, 3 * D_MODEL)
    ins = [rows(proj, tm, D_MODEL, COL_GATE + j * D_MODEL) for j in range(3)]
    ins += [(b2, (1, D_MODEL), (lambda j: lambda i: (0, j))(j)) for j in range(3)]
    return ins


def merge_fwd(proj, b_gate, ps):
    s = proj.shape[0]
    tm = 128

    def fn(g0, g1, g2, b0, b1, b2, p0, p1, p2):
        return (_sigmoid(g0 + b0) * p0 + _sigmoid(g1 + b1) * p1 + _sigmoid(g2 + b2) * p2,)

    return bmap("merge_fwd", fn, (s // tm,), _gate_ins(proj, b_gate, tm) + [rows(p, tm) for p in ps],
                [row_out(s, D_MODEL, BF16, tm)])[0]


def merge_bwd(proj, b_gate, ps, dmerged):
    s = proj.shape[0]
    tm = 128

    def fn(g0, g1, g2, b0, b1, b2, p0, p1, p2, dm):
        dps, dgs = [], []
        for g, b, p in ((g0, b0, p0), (g1, b1, p1), (g2, b2, p2)):
            gate = _sigmoid(g + b)
            dps.append(dm * gate)
            dgs.append(dm * p * gate * (1.0 - gate))
        dgl = jnp.concatenate(dgs, axis=1)
        return (*dps, dgl, jnp.sum(dgl, axis=0, keepdims=True))

    res = bmap("merge_bwd", fn, (s // tm,),
               _gate_ins(proj, b_gate, tm) + [rows(p, tm) for p in ps] + [rows(dmerged, tm)],
               [row_out(s, D_MODEL, BF16, tm)] * 3 + [row_out(s, 3 * D_MODEL, BF16, tm)], [col_acc(3 * D_MODEL)])
    return res[0:3], res[3], res[4][0]


def swiglu_fwd(gu):
    s = gu.shape[0]
    tm = 128
    return bmap("swiglu_fwd", lambda g, u: (_silu(g) * u,), (s // tm,),
                [rows(gu, tm, FFN_HIDDEN, 0), rows(gu, tm, FFN_HIDDEN, FFN_HIDDEN)],
                [row_out(s, FFN_HIDDEN, BF16, tm)])[0]


def swiglu_bwd(gu, dact):
    s = gu.shape[0]
    tm = 128

    def fn(g, u, da):
        return (jnp.concatenate([da * u * _dsilu(g), da * _silu(g)], axis=1),)

    return bmap("swiglu_bwd", fn, (s // tm,),
                [rows(gu, tm, FFN_HIDDEN, 0), rows(gu, tm, FFN_HIDDEN, FFN_HIDDEN), rows(dact, tm)],
                [row_out(s, 2 * FFN_HIDDEN, BF16, tm)])[0]


def loss_head(y, target):
    s, d = y.shape
    tm = _pick(s, (256, 128))

    def fn(yv, tv):
        err = yv - tv
        part = jnp.sum(jnp.mean(err * err, axis=-1, keepdims=True), axis=0, keepdims=True)
        return err * (1.0 / d), 0.5 * part

    dy, loss = bmap("loss_head", fn, (s // tm,), [rows(y, tm), rows(target, tm)], [row_out(s, d, F32, tm)],
                    [((1, 1), (1, 1), lambda i: (0, 0))])
    return dy, loss[0, 0]


def _to_heads(a2d, heads, d):
    return a2d.reshape(a2d.shape[0], heads, d).transpose(1, 0, 2)


def _from_heads(a3d):
    return a3d.transpose(1, 0, 2).reshape(a3d.shape[1], a3d.shape[0] * a3d.shape[2])


def _pad_lanes(v):
    return jnp.pad(v.reshape(1, -1), ((0, 0), (0, LANES - v.shape[-1])))


def layer_forward(x, w, p, tabs):
    s = x.shape[0]
    cos2, sin2, ac_ret, ar_ret = tabs
    sv = {"x": x}
    h = rms_fwd(x, p["norm_mix_pre"], BF16, "rms_mix_pre")
    proj = matmul(h, w["w_in"], name="mm_in")
    sv.update(h=h, proj=proj)

    qr, kr = rotary_fwd(proj, cos2, sin2)
    v_ret = (proj, COL_RET + 2 * BR_WIDTH, N_HEADS)
    o_ret = linear_attention(qr, kr, v_ret, ac_ret, ar_ret, chunk=ac_ret.shape[2], sub_chunk=RET_CHUNK, causal=False,
                             reverse=False, heads=N_HEADS, name="ret_fwd")
    y_ret = gn_gate_fwd(o_ret, p["ret_gn_w"], proj)
    sv.update(qr=qr, kr=kr, o_ret=o_ret, y_ret=y_ret)

    qkv = proj[:, COL_SB:COL_SB + 3 * BR_WIDTH].astype(BF16)
    o_sb = sb_forward(qkv)
    y_sb = o_sb.astype(BF16)
    sv.update(qkv=qkv, o_sb=o_sb, y_sb=y_sb)

    xbc = proj[:, COL_XBC:COL_XBC + XBC_WIDTH]
    us = _shifted(xbc, False) + [xbc]
    conv_b = p["ssd_conv_b"].reshape(1, XBC_WIDTH)
    act = ssd_conv_fwd(us, p["ssd_conv_w"], conv_b)
    dt, acum = ssd_dt_fwd(proj, _pad_lanes(p["ssd_dt_bias"]), _pad_lanes(p["ssd_a_log"]))
    nc = s // SSD_CHUNK
    x_h = _to_heads(act[:, :BR_WIDTH], SSD_HEADS, SSD_P)
    dt_h = dt[:, :SSD_HEADS].T[:, :, None]
    acs = acum[:, :SSD_HEADS].T.reshape(SSD_HEADS, nc, SSD_CHUNK)
    ac_h, ar_h = acs[:, :, :, None], acs[:, :, None, :]
    xdt_h = ssd_xdt(x_h, dt_h)
    b_grp = (act, BR_WIDTH, SSD_GROUPS)
    c_grp = (act, BR_WIDTH + SSD_GROUPS * SSD_N, SSD_GROUPS)
    ytil_h, y_cross, s_fwd = linear_attention(c_grp, b_grp, xdt_h, ac_h, ar_h, chunk=SSD_CHUNK, causal=True,
                                              reverse=False, heads=SSD_HEADS, name="ssd_fwd", parts_out=True)
    ytil = _from_heads(ytil_h)
    d_full = jnp.repeat(p["ssd_d"], SSD_P).reshape(1, BR_WIDTH)
    norm_w = p["ssd_norm_w"].reshape(1, BR_WIDTH)
    y_ssd = ssd_gate_fwd(ytil, act, proj, d_full, norm_w)
    sv.update(act=act, x_h=x_h, dt_h=dt_h, ac_h=ac_h, ar_h=ar_h, xdt_h=xdt_h, y_cross=y_cross, s_fwd=s_fwd, ytil=ytil,
              y_ssd=y_ssd)

    ys = (y_ret, y_sb, y_ssd)
    ps = [matmul(y, wb, name="mm_branch") for y, wb in zip(ys, w["w_branch"])]
    merged = merge_fwd(proj, p["b_gate"], ps)
    mo = matmul(merged, w["w_out"], name="mm_out")
    x1 = rms_fwd(mo, p["norm_mix_post"], F32, "rms_mix_post", resid=x)
    sv.update(ps=ps, merged=merged, mo=mo, x1=x1)

    h2 = rms_fwd(x1, p["norm_ffn_pre"], BF16, "rms_ffn_pre")
    gu = matmul(h2, w["w_gu"], name="mm_gu")
    actf = swiglu_fwd(gu)
    f = matmul(actf, w["w_down"], name="mm_down")
    x2 = rms_fwd(f, p["norm_ffn_post"], F32, "rms_ffn_post", resid=x1)
    sv.update(h2=h2, gu=gu, actf=actf, f=f)
    return x2, sv


def layer_backward(dx2, sv, w, p, tabs):
    cos2, sin2, ac_ret, ar_ret = tabs
    proj = sv["proj"]
    s = proj.shape[0]
    gs = {}

    df, gs["norm_ffn_post"] = rms_bwd(sv["f"], p["norm_ffn_post"], dx2, "rms_ffn_post_bwd", out_dtype=BF16)
    dactf = matmul(df, w["w_down"], nt=True, name="mm_down_dx")
    gw_down = matmul(sv["actf"].T, df, out_dtype=BF16, name="mm_down_dw")
    dgu = swiglu_bwd(sv["gu"], dactf)
    dh2 = matmul(dgu, w["w_gu"], nt=True, name="mm_gu_dx")
    gw_gu = matmul(sv["h2"].T, dgu, out_dtype=BF16, name="mm_gu_dw")
    dx1, gs["norm_ffn_pre"] = rms_bwd(sv["x1"], p["norm_ffn_pre"], dh2, "rms_ffn_pre_bwd", resid=dx2)

    dmo, gs["norm_mix_post"] = rms_bwd(sv["mo"], p["norm_mix_post"], dx1, "rms_mix_post_bwd", out_dtype=BF16)
    dmerged = matmul(dmo, w["w_out"], nt=True, name="mm_out_dx")
    gw_out = matmul(sv["merged"].T, dmo, out_dtype=BF16, name="mm_out_dw")
    dps, dgl, gs["b_gate"] = merge_bwd(proj, p["b_gate"], sv["ps"], dmerged)
    ys = (sv["y_ret"], sv["y_sb"], sv["y_ssd"])
    dys = [matmul(dp, wb, nt=True, name="mm_branch_dx") for dp, wb in zip(dps, w["w_branch"])]
    gw_branch = [matmul(y.T, dp, out_dtype=BF16, name="mm_branch_dw") for y, dp in zip(ys, dps)]

    do_h, dg, gs["ret_gn_w"] = gn_gate_bwd(sv["o_ret"], p["ret_gn_w"], proj, dys[0])
    v_ret = (proj, COL_RET + 2 * BR_WIDTH, N_HEADS)
    ret = dict(chunk=ac_ret.shape[2], sub_chunk=RET_CHUNK, causal=False, heads=N_HEADS)
    dqr = linear_attention(do_h, v_ret, sv["kr"], ac_ret, ar_ret, reverse=False, name="ret_dq", **ret)
    dkr = linear_attention(v_ret, do_h, sv["qr"], ac_ret, ar_ret, reverse=True, name="ret_dk", **ret)
    dv = linear_attention(sv["kr"], sv["qr"], do_h, ac_ret, ar_ret, reverse=True, name="ret_dv",
                          out_dtype=BF16, out_2d=True, **ret)
    dq, dk = rotary_bwd(dqr, dkr, cos2, sin2)

    dsq, dsk, dsv = sb_backward(sv["qkv"], sv["o_sb"], dys[1])

    act = sv["act"]
    d_full = jnp.repeat(p["ssd_d"], SSD_P).reshape(1, BR_WIDTH)
    norm_w = p["ssd_norm_w"].reshape(1, BR_WIDTH)
    dy, dz, gs["ssd_norm_w"], dd_lanes = ssd_gate_bwd(sv["ytil"], act, proj, d_full, norm_w, dys[2])
    gs["ssd_d"] = dd_lanes.reshape(SSD_HEADS, SSD_P).sum(axis=1)
    dy_h = _to_heads(dy, SSD_HEADS, SSD_P)
    b_grp = (act, BR_WIDTH, SSD_GROUPS)
    c_grp = (act, BR_WIDTH + SSD_GROUPS * SSD_N, SSD_GROUPS)
    ssd = dict(chunk=SSD_CHUNK, causal=True, heads=SSD_HEADS)
    ac_h, ar_h, xdt_h = sv["ac_h"], sv["ar_h"], sv["xdt_h"]
    dxdt_h, dx_cross, s_rev = linear_attention(b_grp, c_grp, dy_h, ac_h, ar_h, reverse=True, name="ssd_dx",
                                               parts_out=True, **ssd)
    dc_h = linear_attention(dy_h, xdt_h, b_grp, ac_h, ar_h, reverse=False, name="ssd_dc", **ssd)
    db_h = linear_attention(xdt_h, dy_h, c_grp, ac_h, ar_h, reverse=True, name="ssd_db", **ssd)
    d_h = p["ssd_d"].reshape(SSD_HEADS, 1, 1)
    dx_h, e_h, v_h, ddtx_h = ssd_head_bwd(dy_h, sv["y_cross"], dxdt_h, dx_cross, xdt_h, sv["x_h"], sv["dt_h"], d_h)
    lanes = lambda a: jnp.pad(a[:, :, 0].T, ((0, 0), (0, LANES - SSD_HEADS)))
    g_chunk = ssd_state_dot(sv["s_fwd"], s_rev, ac_h[:, :, SSD_CHUNK - 1:, :])
    own = ssd_decay_grad_own(c_grp, b_grp, dy_h, xdt_h, ac_h, ar_h).reshape(SSD_HEADS, s, 1)
    da = ssd_decay_grad(lanes(e_h), lanes(v_h), lanes(g_chunk[:, :, :, 0])[:, None, :], lanes(own))
    dt_bias, a_log = _pad_lanes(p["ssd_dt_bias"]), _pad_lanes(p["ssd_a_log"])
    ddtr, dbias, dalog = ssd_dt_bwd(da, lanes(ddtx_h), proj, dt_bias, a_log)
    gs["ssd_dt_bias"], gs["ssd_a_log"] = dbias[:SSD_HEADS], dalog[:SSD_HEADS]
    db2, dc2 = ssd_group_sum(db_h, dc_h)
    dact = jnp.concatenate([_from_heads(dx_h), db2, dc2], axis=1)
    xbc = proj[:, COL_XBC:COL_XBC + XBC_WIDTH]
    us = _shifted(xbc, False) + [xbc]
    conv_b = p["ssd_conv_b"].reshape(1, XBC_WIDTH)
    dcv, gs["ssd_conv_w"], gs["ssd_conv_b"] = ssd_conv_bwd_a(us, p["ssd_conv_w"], conv_b, dact)
    dxbc = ssd_conv_bwd_b(_shifted(dcv, True) + [dcv], p["ssd_conv_w"])

    dt_cols = jnp.pad(ddtr, ((0, 0), (0, IN_PAD - COL_DT - LANES)))
    dproj = jnp.concatenate([dq, dk, dv, dg, dsq.astype(BF16), dsk.astype(BF16), dsv.astype(BF16),
                             dz, dxbc, dgl, dt_cols], axis=1)
    dh = matmul(dproj, w["w_in"], nt=True, name="mm_in_dx")
    gw_in = matmul(sv["h"].T, dproj, out_dtype=BF16, name="mm_in_dw")
    dx, gs["norm_mix_pre"] = rms_bwd(sv["x"], p["norm_mix_pre"], dh, "rms_mix_pre_bwd", resid=dx1)
    gw = dict(w_in=gw_in, w_branch=gw_branch, w_out=gw_out, w_gu=gw_gu, w_down=gw_down)
    return dx, gw, gs


def in_cols_pad(w):
    zeros = jnp.zeros(w.shape[:-1] + (IN_PAD - IN_COLS,), w.dtype)
    return jnp.concatenate([w[..., :COL_GATE], w[..., COL_GATE + SSD_HEADS:], w[..., COL_GATE:COL_GATE + SSD_HEADS], zeros],
                           axis=-1)


def in_cols_unpad(g):
    return jnp.concatenate([g[..., :COL_GATE], g[..., COL_DT:COL_DT + SSD_HEADS], g[..., COL_GATE:COL_DT]], axis=-1)


W_IN_SHARD = (D_MODEL, IN_COLS // N_SHARD)
BIG = (("w_branch_ret", (BR_WIDTH, D_MODEL // N_SHARD), True),
       ("w_branch_sb", (BR_WIDTH, D_MODEL // N_SHARD), True),
       ("w_branch_ssd", (BR_WIDTH, D_MODEL // N_SHARD), True),
       ("w_out", (D_MODEL // N_SHARD, D_MODEL), False),
       ("ffn_w_gate", (D_MODEL, FFN_HIDDEN // N_SHARD), True),
       ("ffn_w_up", (D_MODEL, FFN_HIDDEN // N_SHARD), True),
       ("ffn_w_down", (FFN_HIDDEN // N_SHARD, D_MODEL), False))
FLAT_ROWS = sum(r * c for _, (r, c), _ in BIG) // LANES
assert FLAT_ROWS * LANES == sum(r * c for _, (r, c), _ in BIG) and FLAT_ROWS % 32 == 0
HALF_ROWS = FLAT_ROWS // 2


def pack_shard(layer_shards, dtype):
    flat = jnp.concatenate([layer_shards[n].reshape(-1).astype(dtype) for n, _, _ in BIG])
    return flat.reshape(FLAT_ROWS, LANES)


def unpack_shard(flat):
    out, off = {}, 0
    v = flat.reshape(-1)
    for n, (r, c), _ in BIG:
        out[n] = v[off:off + r * c].reshape(r, c)
        off += r * c
    return out


def unpack_gathered(full):
    v = full.reshape(N_SHARD, -1)
    out, off = {}, 0
    for n, (r, c), by_cols in BIG:
        seg = v[:, off:off + r * c].reshape(N_SHARD, r, c)
        out[n] = seg.transpose(1, 0, 2).reshape(r, N_SHARD * c) if by_cols else seg.reshape(N_SHARD * r, c)
        off += r * c
    return out


def pack_for_scatter(whole):
    segs = []
    for n, (r, c), by_cols in BIG:
        g = whole[n].astype(BF16)
        g = g.reshape(r, N_SHARD, c).transpose(1, 0, 2) if by_cols else g.reshape(N_SHARD, r, c)
        segs.append(g.reshape(N_SHARD, r * c))
    return jnp.concatenate(segs, axis=1).reshape(N_SHARD, FLAT_ROWS, LANES)


ANY = pl.BlockSpec(memory_space=pl.ANY)


def _place():
    return lax.axis_index("x"), lax.axis_index("y"), lax.axis_index("c")


def _other_chips(x, y):
    return [(1 - x, y), (x, 1 - y), (1 - x, 1 - y)]


def _half(c, rows):
    return pl.ds(pl.multiple_of(c * (rows // 2), 16), rows // 2)


def _sems(n):
    return [pltpu.SemaphoreType.DMA((n,)), pltpu.SemaphoreType.DMA((n,))]


def all_gather_shards(shards):
    n = len(shards)

    def body(*refs):
        x_refs, o_refs = refs[:n], refs[n:2 * n]
        send_sems, recv_sems = refs[2 * n:]
        x, y, c = _place()
        j = 2 * x + y
        sibling = (x, y, 1 - c)
        chips = _other_chips(x, y)

        def copy(k, src, dst, to):
            return pltpu.make_async_remote_copy(src_ref=src, dst_ref=dst, send_sem=send_sems.at[k],
                                                recv_sem=recv_sems.at[k], device_id=to, device_id_type=MESH)

        started = []
        for a, (x_ref, o_ref) in enumerate(zip(x_refs, o_refs)):
            mine = _half(c, x_ref.shape[0])
            for k, (cx, cy) in enumerate(chips):
                cp = copy(6 * a + k, x_ref.at[mine], o_ref.at[j, mine], (cx, cy, c))
                cp.start()
                started.append(cp)
        for a, o_ref in enumerate(o_refs):
            mine = _half(c, o_ref.shape[1])
            for k, (cx, cy) in enumerate(chips):
                landed = o_ref.at[2 * cx + cy, mine]
                copy(6 * a + k, landed, landed, (cx, cy, c)).wait_recv()
                cp = copy(6 * a + 3 + k, landed, landed, sibling)
                cp.start()
                started.append(cp)
        for a, o_ref in enumerate(o_refs):
            other = _half(1 - c, o_ref.shape[1])
            for k, (cx, cy) in enumerate(chips):
                landed = o_ref.at[2 * cx + cy, other]
                copy(6 * a + 3 + k, landed, landed, sibling).wait_recv()
        for cp in started:
            cp.wait_send()

    gathered = pl.pallas_call(
        body, name="all_gather_shards", in_specs=[ANY] * n, out_specs=[ANY] * n,
        out_shape=[jax.ShapeDtypeStruct((N_SHARD,) + s.shape, s.dtype) for s in shards],
        scratch_shapes=_sems(6 * n),
    )(*shards)
    own = 2 * lax.axis_index("x") + lax.axis_index("y")
    slot = lambda g: lax.broadcasted_iota(jnp.int32, (N_SHARD,) + (1,) * (g.ndim - 1), 0)
    return [jnp.where(slot(g) == own, s[None], g) for s, g in zip(shards, gathered)]


def exchange_siblings(gs):
    n = len(gs)

    def body(*refs):
        g_refs, o_refs = refs[:n], refs[n:2 * n]
        send_sems, recv_sems = refs[2 * n:]
        x, y, c = _place()
        copies = []
        for a, (g_ref, o_ref) in enumerate(zip(g_refs, o_refs)):
            cp = pltpu.make_async_remote_copy(src_ref=g_ref.at[:, _half(1 - c, g_ref.shape[1])], dst_ref=o_ref,
                                              send_sem=send_sems.at[a], recv_sem=recv_sems.at[a],
                                              device_id=(x, y, 1 - c), device_id_type=MESH)
            cp.start()
            copies.append(cp)
        for cp in copies:
            cp.wait()

    return pl.pallas_call(
        body, name="exchange_siblings", in_specs=[ANY] * n, out_specs=[ANY] * n,
        out_shape=[jax.ShapeDtypeStruct((N_SHARD, g.shape[1] // 2, g.shape[2]), g.dtype) for g in gs],
        scratch_shapes=_sems(n),
    )(*gs)


def exchange_chips(ps):
    n = len(ps)

    def body(*refs):
        p_refs, o_refs = refs[:n], refs[n:2 * n]
        send_sems, recv_sems, local_sems = refs[2 * n:]
        x, y, c = _place()
        j = 2 * x + y
        copies = []
        for a, (p_ref, o_ref) in enumerate(zip(p_refs, o_refs)):
            local = pltpu.make_async_copy(p_ref.at[j], o_ref.at[j], local_sems.at[a])
            local.start()
            copies.append(local)
            for k, (cx, cy) in enumerate(_other_chips(x, y)):
                cp = pltpu.make_async_remote_copy(src_ref=p_ref.at[2 * cx + cy], dst_ref=o_ref.at[j],
                                                  send_sem=send_sems.at[3 * a + k], recv_sem=recv_sems.at[3 * a + k],
                                                  device_id=(cx, cy, c), device_id_type=MESH)
                cp.start()
                copies.append(cp)
        for cp in copies:
            cp.wait()

    return pl.pallas_call(
        body, name="exchange_chips", in_specs=[ANY] * n, out_specs=[ANY] * n,
        out_shape=[jax.ShapeDtypeStruct(p.shape, p.dtype) for p in ps],
        scratch_shapes=_sems(3 * n) + [pltpu.SemaphoreType.DMA((n,))],
    )(*ps)


def swap_halves(reds):
    n = len(reds)

    def body(*refs):
        r_refs, o_refs = refs[:n], refs[n:2 * n]
        send_sems, recv_sems = refs[2 * n:]
        x, y, c = _place()
        copies = []
        for a, (r_ref, o_ref) in enumerate(zip(r_refs, o_refs)):
            cp = pltpu.make_async_remote_copy(src_ref=r_ref, dst_ref=o_ref, send_sem=send_sems.at[a],
                                              recv_sem=recv_sems.at[a], device_id=(x, y, 1 - c), device_id_type=MESH)
            cp.start()
            copies.append(cp)
        for cp in copies:
            cp.wait()

    return pl.pallas_call(
        body, name="swap_halves", in_specs=[ANY] * n, out_specs=[ANY] * n,
        out_shape=[jax.ShapeDtypeStruct(r.shape, r.dtype) for r in reds],
        scratch_shapes=_sems(n),
    )(*reds)


def _sum_rows(rows, cols):
    best = 16
    for t in range(16, rows + 1, 16):
        if rows % t == 0 and t * cols * 4 <= 1536 * 1024:
            best = t
    return best


def reduce_scatter(gs):
    c = lax.axis_index("c")
    gots = exchange_siblings(gs)
    pairs = []
    for g, got in zip(gs, gots):
        half, cols = got.shape[1], got.shape[2]
        t = _sum_rows(half, cols)
        own = lax.dynamic_slice_in_dim(g, c * half, half, axis=1)
        blk = lambda a: (a, (None, t, cols), lambda j, i: (j, i, 0))
        pairs.append(bmap("sum_siblings", lambda a, b: (a.astype(F32) + b.astype(F32),), (N_SHARD, half // t),
                          [blk(own), blk(got)], [(got.shape, BF16, (None, t, cols), lambda j, i: (j, i, 0))])[0])
    reds = []
    for parts in exchange_chips(pairs):
        half, cols = parts.shape[1], parts.shape[2]
        t = _sum_rows(half, cols)
        part = lambda j: (parts, (None, t, cols), lambda i: (j, i, 0))
        reds.append(bmap("sum_chips",
                         lambda a, b, c_, d: (((a.astype(F32) + b.astype(F32)) + c_.astype(F32)) + d.astype(F32),),
                         (half // t,), [part(j) for j in range(N_SHARD)],
                         [((half, cols), F32, (t, cols), lambda i: (i, 0))])[0])
    out = []
    for mine, theirs in zip(reds, swap_halves(reds)):
        both = jnp.where(c == 0, jnp.stack([mine, theirs]), jnp.stack([theirs, mine]))
        out.append(both.reshape(2 * mine.shape[0], mine.shape[1]))
    return out


def all_reduce_small(buf):
    r = buf.shape[0]

    def body(x_ref, o_ref, land_ref, send_sems, recv_sems):
        x, y, c = _place()
        me = 4 * x + 2 * y + c
        land_ref[me] = x_ref[...]
        copies = []
        for d in range(1, 8):
            peer = tuple(1 - v if (d >> s) & 1 else v for v, s in ((x, 2), (y, 1), (c, 0)))
            cp = pltpu.make_async_remote_copy(src_ref=x_ref, dst_ref=land_ref.at[me], send_sem=send_sems.at[d - 1],
                                              recv_sem=recv_sems.at[d - 1], device_id=peer, device_id_type=MESH)
            cp.start()
            copies.append(cp)
        for cp in copies:
            cp.wait()
        acc = land_ref[0]
        for s in range(1, 8):
            acc = acc + land_ref[s]
        o_ref[...] = acc

    vmem = pl.BlockSpec(memory_space=pltpu.VMEM)
    return pl.pallas_call(
        body, name="all_reduce_small", in_specs=[vmem], out_specs=vmem,
        out_shape=jax.ShapeDtypeStruct((r, LANES), F32),
        scratch_shapes=[pltpu.VMEM((8, r, LANES), F32), pltpu.SemaphoreType.DMA((7,)), pltpu.SemaphoreType.DMA((7,))],
    )(buf)


def adamw(w, g, m, v, name):
    r, c = w.shape
    tm = r
    for cand in (512, 256, 128, 64, 32, 16, 8):
        if r % cand == 0 and cand * c * 4 * 14 <= 24 * 1024 * 1024:
            tm = cand
            break
    bc1 = 1.0 - ADAM_B1 ** ADAM_STEP
    bc2 = 1.0 - ADAM_B2 ** ADAM_STEP

    def fn(wv, gv, mv, vv):
        m2 = ADAM_B1 * mv + (1.0 - ADAM_B1) * gv
        v2 = ADAM_B2 * vv + (1.0 - ADAM_B2) * (gv * gv)
        delta = -ADAM_LR * ((m2 / bc1) / (jnp.sqrt(v2 / bc2) + ADAM_EPS) + ADAM_WD * wv)
        return delta, m2, v2

    return bmap(name, fn, (r // tm,), [rows(a, tm) for a in (w, g, m, v)], [row_out(r, c, F32, tm)] * 3)


WEIGHTS = ("norm_mix_pre", "norm_mix_post", "norm_ffn_pre", "norm_ffn_post", "w_in", "b_gate", "ret_gn_w", "ssd_conv_w",
           "ssd_conv_b", "ssd_dt_bias", "ssd_a_log", "ssd_d", "ssd_norm_w", "w_branch_ret", "w_branch_sb", "w_branch_ssd",
           "w_out", "ffn_w_gate", "ffn_w_up", "ffn_w_down")
BIG_NAMES = tuple(n for n, _, _ in BIG)
SHARDED = ("w_in",) + BIG_NAMES
SMALL = tuple(n for n in WEIGHTS if n not in SHARDED and n != "ssd_conv_w")
CONV_SHARD = XBC_WIDTH // N_SHARD


def _pack_small(parts):
    flat = jnp.concatenate([p.reshape(-1).astype(F32) for p in parts])
    rows_ = -(-flat.shape[0] // (8 * LANES)) * 8
    return jnp.pad(flat, (0, rows_ * LANES - flat.shape[0])).reshape(rows_, LANES)


def _unpack_small(buf, shapes):
    v, out, off = buf.reshape(-1), [], 0
    for shp in shapes:
        n = int(np.prod(shp))
        out.append(v[off:off + n].reshape(shp))
        off += n
    return out


def kernel(x, positions, norm_mix_pre, norm_mix_post, norm_ffn_pre, norm_ffn_post, w_in, b_gate, ret_gn_w, ssd_conv_w, ssd_conv_b, ssd_dt_bias, ssd_a_log, ssd_d, ssd_norm_w, w_branch_ret, w_branch_sb, w_branch_ssd, w_out, ffn_w_gate, ffn_w_up, ffn_w_down, loss_target, m_norm_mix_pre, m_norm_mix_post, m_norm_ffn_pre, m_norm_ffn_post, m_w_in, m_b_gate, m_ret_gn_w, m_ssd_conv_w, m_ssd_conv_b, m_ssd_dt_bias, m_ssd_a_log, m_ssd_d, m_ssd_norm_w, m_w_branch_ret, m_w_branch_sb, m_w_branch_ssd, m_w_out, m_ffn_w_gate, m_ffn_w_up, m_ffn_w_down, v_norm_mix_pre, v_norm_mix_post, v_norm_ffn_pre, v_norm_ffn_post, v_w_in, v_b_gate, v_ret_gn_w, v_ssd_conv_w, v_ssd_conv_b, v_ssd_dt_bias, v_ssd_a_log, v_ssd_d, v_ssd_norm_w, v_w_branch_ret, v_w_branch_sb, v_w_branch_ssd, v_w_out, v_ffn_w_gate, v_ffn_w_up, v_ffn_w_down):
    given = dict(locals())
    wts = {n: given[n] for n in WEIGHTS}
    mom = {n: given["m_" + n] for n in WEIGHTS}
    var = {n: given["v_" + n] for n in WEIGHTS}
    xi, yi, ci = lax.axis_index("x"), lax.axis_index("y"), lax.axis_index("c")
    shard_id = 2 * xi + yi

    conv_mine = jnp.where(ci == 0, ssd_conv_w, 0.0)
    conv_slots = lax.dynamic_update_slice_in_dim(jnp.zeros((DEPTH, SSD_CONV, XBC_WIDTH), F32), conv_mine,
                                                 shard_id * CONV_SHARD, axis=2)
    conv_buf = _pack_small([conv_slots])
    conv_whole = _unpack_small(all_reduce_small(conv_buf), [(DEPTH, SSD_CONV, XBC_WIDTH)])[0]

    tabs = rope_tables(positions[0]) + ret_decay_tables(x.shape[1])
    layer_w, layer_p = [], []
    for l in range(DEPTH):
        flat, w_in_all = all_gather_shards([pack_shard({n: wts[n][l] for n in BIG_NAMES}, BF16), w_in[l].astype(BF16)])
        full = unpack_gathered(flat)
        layer_w.append(dict(
            w_in=in_cols_pad(w_in_all.transpose(1, 0, 2).reshape(D_MODEL, IN_COLS)),
            w_branch=[full["w_branch_ret"], full["w_branch_sb"], full["w_branch_ssd"]],
            w_out=full["w_out"],
            w_gu=jnp.concatenate([full["ffn_w_gate"], full["ffn_w_up"]], axis=1),
            w_down=full["ffn_w_down"]))
        p = {n: wts[n][l] for n in SMALL}
        p["ssd_conv_w"] = conv_whole[l]
        layer_p.append(p)

    act = x[0]
    saved = []
    for l in range(DEPTH):
        act, sv = layer_forward(act, layer_w[l], layer_p[l], tabs)
        saved.append(sv)
    dact, loss_part = loss_head(act, loss_target[0])

    big_grads = [None] * DEPTH
    small_grads = [None] * DEPTH
    for l in reversed(range(DEPTH)):
        dact, gw, small_grads[l] = layer_backward(dact, saved[l], layer_w[l], layer_p[l], tabs)
        gu = gw["w_gu"]
        whole = dict(w_branch_ret=gw["w_branch"][0], w_branch_sb=gw["w_branch"][1],
                     w_branch_ssd=gw["w_branch"][2], w_out=gw["w_out"], ffn_w_gate=gu[:, :FFN_HIDDEN],
                     ffn_w_up=gu[:, FFN_HIDDEN:], ffn_w_down=gw["w_down"])
        g_in = in_cols_unpad(gw["w_in"]).reshape(D_MODEL, N_SHARD, IN_COLS // N_SHARD).transpose(1, 0, 2)
        red_flat, red_in = reduce_scatter([pack_for_scatter(whole), g_in])
        big_grads[l] = unpack_shard(red_flat)
        big_grads[l]["w_in"] = red_in

    small_names = SMALL + ("ssd_conv_w",)
    small_parts = [jnp.stack([small_grads[l][n] for l in range(DEPTH)]) for n in small_names]
    small_shapes = [p.shape for p in small_parts]
    summed = all_reduce_small(_pack_small([loss_part.reshape(1)] + small_parts))
    loss, *small_sum = _unpack_small(summed, [(1,)] + small_shapes)
    grads = dict(zip(small_names, small_sum))
    grads["ssd_conv_w"] = lax.dynamic_slice_in_dim(grads["ssd_conv_w"], shard_id * CONV_SHARD, CONV_SHARD, axis=2)
    for n in SHARDED:
        grads[n] = jnp.stack([big_grads[l][n] for l in range(DEPTH)])

    delta, new_m, new_v = {}, {}, {}
    for n in WEIGHTS:
        shape = wts[n].shape
        two_d = (-1, shape[-1]) if len(shape) == 3 else (8, -1) if shape[-1] * shape[0] % (8 * LANES) == 0 else shape
        args = [a.reshape(two_d) for a in (wts[n], grads[n], mom[n], var[n])]
        d, m2, v2 = adamw(*args, name="adamw_" + n)
        delta[n], new_m[n], new_v[n] = d.reshape(shape), m2.reshape(shape), v2.reshape(shape)

    return (loss.reshape(()), dact[None], *[grads[n] for n in WEIGHTS], *[delta[n] for n in WEIGHTS],
            *[new_m[n] for n in WEIGHTS], *[new_v[n] for n in WEIGHTS])
```

```python
import functools
import math

import numpy as np
import jax
import jax.numpy as jnp
from jax import lax
from jax.experimental import pallas as pl
from jax.experimental.pallas import tpu as pltpu

F32 = jnp.float32
BF16 = jnp.bfloat16
MESH = pl.DeviceIdType.MESH

D_MODEL = 2048
DEPTH = 4
RET_CHUNK = 64
NORM_EPS = 1e-6
HEAD = 128
N_HEADS = 8
BR_WIDTH = 1024
SSD_P = 64
SSD_HEADS = 16
SSD_GROUPS = 4
SSD_N = 128
SSD_CHUNK = 256
SSD_CONV = 4
FFN_HIDDEN = 5632
IN_COLS = 16400
IN_PAD = 16896
COL_RET, COL_SB, COL_Z, COL_XBC, COL_GATE, COL_DT = 0, 4096, 7168, 8192, 10240, 16384
N_SHARD = 4
LANES = 128
VMEM_LIMIT = 48 * 1024 * 1024

ADAM_LR, ADAM_B1, ADAM_B2, ADAM_EPS, ADAM_WD, ADAM_STEP = 0.001, 0.9, 0.999, 1e-08, 0.01, 10


def _pick(n, prefs):
    for p in prefs:
        if n % p == 0:
            return p
    return n


def _params(sem=None):
    kw = dict(vmem_limit_bytes=VMEM_LIMIT)
    if sem is not None:
        kw["dimension_semantics"] = sem
    return pltpu.CompilerParams(**kw)


MATMUL_VMEM = 36 * 1024 * 1024


def _matmul_tiles(m, n, k, out_bytes):
    def options(dim, prefs):
        opts = [p for p in prefs if dim % p == 0]
        return opts or [dim]

    best, best_vol = None, -1
    for tm in options(m, (1408, 1024, 512, 256, 128)):
        for tn in options(n, (1536, 1408, 1024, 512, 256, 128)):
            for tk in options(k, (2048, 1536, 1408, 1024, 512, 256, 128)):
                need = 2 * 2 * (tm * tk + tk * tn) + tm * tn * (4 + 2 * out_bytes)
                vol = tm * tn * tk
                if need <= MATMUL_VMEM and vol > best_vol:
                    best, best_vol = (tm, tn, tk), vol
    return best


def matmul(a, b, *, nt=False, out_dtype=F32, name="matmul"):
    m, k = a.shape
    n = b.shape[0] if nt else b.shape[1]
    tm, tn, tk = _matmul_tiles(m, n, k, jnp.dtype(out_dtype).itemsize)
    nk = k // tk

    def body(a_ref, b_ref, o_ref, acc_ref):
        kk = pl.program_id(2)
        if nt:
            part = lax.dot_general(a_ref[...], b_ref[...], (((1,), (1,)), ((), ())), preferred_element_type=F32)
        else:
            part = jnp.dot(a_ref[...], b_ref[...], preferred_element_type=F32)

        @pl.when(kk == 0)
        def _():
            acc_ref[...] = part

        @pl.when(kk > 0)
        def _():
            acc_ref[...] += part

        @pl.when(kk == nk - 1)
        def _():
            o_ref[...] = acc_ref[...].astype(o_ref.dtype)

    b_spec = pl.BlockSpec((tn, tk), lambda i, j, kk: (j, kk)) if nt else pl.BlockSpec((tk, tn), lambda i, j, kk: (kk, j))
    return pl.pallas_call(
        body, name=name, grid=(m // tm, n // tn, nk),
        in_specs=[pl.BlockSpec((tm, tk), lambda i, j, kk: (i, kk)), b_spec],
        out_specs=pl.BlockSpec((tm, tn), lambda i, j, kk: (i, j)),
        out_shape=jax.ShapeDtypeStruct((m, n), out_dtype),
        scratch_shapes=[pltpu.VMEM((tm, tn), F32)],
        compiler_params=_params(("parallel", "parallel", "arbitrary")),
    )(a, b)


def bmap(name, fn, grid, ins, outs, accs=()):
    n_in, n_out = len(ins), len(outs)
    last = len(grid) - 1

    def store_acc(ref, val, first):
        @pl.when(first)
        def _():
            ref[...] = val

        @pl.when(jnp.logical_not(first))
        def _():
            ref[...] += val

    def body(*refs):
        vals = fn(*[r[...] for r in refs[:n_in]])
        if not isinstance(vals, (tuple, list)):
            vals = (vals,)
        for r, v in zip(refs[n_in:n_in + n_out], vals[:n_out]):
            r[...] = v.astype(r.dtype)
        first = pl.program_id(last) == 0
        for r, v in zip(refs[n_in + n_out:], vals[n_out:]):
            store_acc(r, v.astype(F32), first)

    sem = ("parallel",) * last + (("arbitrary",) if accs else ("parallel",))
    res = pl.pallas_call(
        body, name=name, grid=grid,
        in_specs=[pl.BlockSpec(bs, im) for (_, bs, im) in ins],
        out_specs=[pl.BlockSpec(bs, im) for (_, _, bs, im) in outs] + [pl.BlockSpec(bs, im) for (_, bs, im) in accs],
        out_shape=[jax.ShapeDtypeStruct(s, dt) for (s, dt, _, _) in outs] + [jax.ShapeDtypeStruct(s, F32) for (s, _, _) in accs],
        compiler_params=_params(sem),
    )(*[a for (a, _, _) in ins])
    return res


def rows(a, tm, width=None, col=0):
    width = a.shape[1] if width is None else width
    assert col % width == 0
    cb = col // width
    return (a, (tm, width), lambda i: (i, cb))


def par(a):
    return (a, a.shape, lambda *_: (0,) * a.ndim)


def row_out(s, width, dtype, tm):
    return ((s, width), dtype, (tm, width), lambda i: (i, 0))


def col_acc(width):
    return ((1, width), (1, width), lambda i: (0, 0))


def _sigmoid(x):
    return 1.0 / (1.0 + jnp.exp(-x))


def _silu(x):
    return x * _sigmoid(x)


def _dsilu(x):
    s = _sigmoid(x)
    return s * (1.0 + x * (1.0 - s))


def _softplus(x):
    return jnp.maximum(x, 0.0) + jnp.log(1.0 + jnp.exp(-jnp.abs(x)))


def _split_bf16(x):
    hi = x.astype(BF16)
    lo = (x - hi.astype(F32)).astype(BF16)
    return hi, lo


def _dot_exact_mask(x, mask_bf16):
    hi, lo = _split_bf16(x)
    return (jnp.dot(hi, mask_bf16, preferred_element_type=F32)
            + jnp.dot(lo, mask_bf16, preferred_element_type=F32))


def rms_fwd(x, w, out_dtype, name, resid=None):
    s, d = x.shape
    tm = _pick(s, (256, 128))

    def fn(xv, wv, *rest):
        r = lax.rsqrt(jnp.mean(xv * xv, axis=-1, keepdims=True) + NORM_EPS)
        y = xv * r * wv
        if rest:
            y = y + rest[0]
        return (y,)

    ins = [rows(x, tm), par(w.reshape(1, d))]
    if resid is not None:
        ins.append(rows(resid, tm))
    return bmap(name, fn, (s // tm,), ins, [row_out(s, d, out_dtype, tm)])[0]


def rms_bwd(x, w, dy, name, resid=None, out_dtype=F32):
    s, d = x.shape
    tm = _pick(s, (256, 128))

    def fn(xv, wv, dyv, *rest):
        dyv = dyv.astype(F32)
        r = lax.rsqrt(jnp.mean(xv * xv, axis=-1, keepdims=True) + NORM_EPS)
        xn = xv * r
        g = dyv * wv
        dx = r * (g - xn * jnp.mean(g * xn, axis=-1, keepdims=True))
        if rest:
            dx = dx + rest[0]
        return dx, jnp.sum(dyv * xn, axis=0, keepdims=True)

    ins = [rows(x, tm), par(w.reshape(1, d)), rows(dy, tm)]
    if resid is not None:
        ins.append(rows(resid, tm))
    dx, dw = bmap(name, fn, (s // tm,), ins, [row_out(s, d, out_dtype, tm)], [col_acc(d)])
    return dx, dw[0]


def linear_attention(q, k, v, ac, ar, *, chunk, causal, reverse, heads, name, out_dtype=F32, out_2d=False,
                     parts_out=False, sub_chunk=None):
    def dims(op):
        return (op[0].shape[0], HEAD) if isinstance(op, tuple) else (op.shape[1], op.shape[2])

    s_len, dk = dims(q)
    dv = dims(v)[1]
    L = chunk
    cpb = _pick(s_len // L, (8, 4, 2, 1))
    nb = s_len // (L * cpb)
    a_chunks = ac.shape[1]

    def blk(i):
        return nb - 1 - i if reverse else i

    def seq_spec(op):
        if isinstance(op, tuple):
            _, first_col, nh = op
            rep, cb = heads // nh, first_col // HEAD
            return pl.BlockSpec((L * cpb, HEAD), lambda h, i: (blk(i), cb + h // rep))
        rep = heads // op.shape[0]
        return pl.BlockSpec((None, L * cpb, op.shape[2]), lambda h, i: (h // rep, blk(i), 0))

    def a_spec(arr, shape):
        rep = heads // arr.shape[0]
        if a_chunks == 1:
            return pl.BlockSpec((None, 1) + shape, lambda h, i: (h // rep, 0, 0, 0))
        return pl.BlockSpec((None, cpb) + shape, lambda h, i: (h // rep, blk(i), 0, 0))

    def body(q_ref, k_ref, v_ref, ac_ref, ar_ref, *rest):
        o_refs, state_ref = rest[:-1], rest[-1]

        @pl.when(pl.program_id(1) == 0)
        def _():
            state_ref[...] = jnp.zeros_like(state_ref)

        row = lax.broadcasted_iota(jnp.int32, (L, L), 0)
        col = lax.broadcasted_iota(jnp.int32, (L, L), 1)

        def decays(ai):
            a_col = ac_ref[ai]
            a_row = ar_ref[ai]
            a_last = ac_ref[ai, pl.ds(L - 1, 1), :]
            seg = a_col - a_row
            if not causal:
                dec = jnp.exp(-jnp.abs(seg))
                if sub_chunk is not None:
                    shift = sub_chunk.bit_length() - 1
                    rs, cs = jnp.right_shift(row, shift), jnp.right_shift(col, shift)
                    dec = jnp.where(cs >= rs if reverse else cs <= rs, dec, 0.0)
            elif reverse:
                dec = jnp.where(col > row, jnp.exp(jnp.minimum(-seg, 0.0)), 0.0)
            else:
                dec = jnp.where(row > col, jnp.exp(jnp.minimum(seg, 0.0)), 0.0)
            if reverse:
                qd, kd = jnp.exp(a_last - a_col), jnp.exp(a_col)
            else:
                qd, kd = jnp.exp(a_col), jnp.exp(a_last - a_col)
            return dec, qd, kd, jnp.exp(a_last)

        fixed = decays(0) if a_chunks == 1 else None

        def one(step, carry):
            c = cpb - 1 - step if reverse else step
            sl = pl.ds(pl.multiple_of(c * L, L), L)
            qc = q_ref[sl, :].astype(F32)
            kc = k_ref[sl, :].astype(F32)
            vf = v_ref[sl, :].astype(F32)
            vc = vf.astype(BF16)
            dec, qd, kd, chunk_decay = fixed if a_chunks == 1 else decays(c)
            sc = lax.dot_general(qc.astype(BF16), kc.astype(BF16), (((1,), (1,)), ((), ())),
                                 preferred_element_type=F32) * dec
            state = state_ref[...]
            intra = jnp.dot(sc.astype(BF16), vc, preferred_element_type=F32)
            cross = jnp.dot((qc * qd).astype(BF16), state.astype(BF16), preferred_element_type=F32)
            out = intra + cross
            if parts_out:
                o_refs[1][sl, :] = cross
                o_refs[2][c] = state
            if causal:
                out = out + jnp.sum(qc * kc, axis=-1, keepdims=True) * vf
            o_refs[0][sl, :] = out.astype(o_refs[0].dtype)
            upd = lax.dot_general((kc * kd).astype(BF16), vc, (((0,), (0,)), ((), ())),
                                  preferred_element_type=F32)
            state_ref[...] = state * chunk_decay + upd
            return carry

        lax.fori_loop(0, cpb, one, 0)

    if out_2d:
        out_spec = pl.BlockSpec((L * cpb, dv), lambda h, i: (blk(i), h))
        out_shape = jax.ShapeDtypeStruct((s_len, heads * dv), out_dtype)
    else:
        out_spec = pl.BlockSpec((None, L * cpb, dv), lambda h, i: (h, blk(i), 0))
        out_shape = jax.ShapeDtypeStruct((heads, s_len, dv), out_dtype)
    out_specs, out_shapes = [out_spec], [out_shape]
    if parts_out:
        assert not out_2d and out_dtype == F32
        out_specs += [out_spec, pl.BlockSpec((None, cpb, dk, dv), lambda h, i: (h, blk(i), 0, 0))]
        out_shapes += [out_shape, jax.ShapeDtypeStruct((heads, s_len // L, dk, dv), F32)]
    arrs = [op[0] if isinstance(op, tuple) else op for op in (q, k, v)]
    res = pl.pallas_call(
        body, name=name, grid=(heads, nb),
        in_specs=[seq_spec(q), seq_spec(k), seq_spec(v), a_spec(ac, (L, 1)), a_spec(ar, (1, L))],
        out_specs=out_specs, out_shape=out_shapes,
        scratch_shapes=[pltpu.VMEM((dk, dv), F32)],
        compiler_params=_params(("parallel", "arbitrary")),
    )(*arrs, ac, ar)
    return res if parts_out else res[0]


SB_KEYS = 256


def _sb_tile(q, kb, scale, valid, tri):
    z = lax.dot_general(q, kb, (((1,), (1,)), ((), ())), preferred_element_type=F32) * scale
    sp = _softplus(z)
    log_beta = z - sp
    lk = -sp if valid is None else jnp.where(valid, -sp, 0.0)
    later_in = _dot_exact_mask(lk, tri)
    return log_beta, lk, log_beta + later_in


def _sb_loops(i, ratio, step, init):
    carry = lax.fori_loop(0, ratio, lambda n, c: step((i + 1) * ratio - 1 - n, True, c), init)
    return lax.fori_loop(0, i * ratio, lambda n, c: step(i * ratio - 1 - n, False, c), carry)


def _sb_masks(i, bq):
    bk = SB_KEYS
    row = lax.broadcasted_iota(jnp.int32, (bk, bk), 0)
    col = lax.broadcasted_iota(jnp.int32, (bk, bk), 1)
    qpos = i * bq + lax.broadcasted_iota(jnp.int32, (bq, bk), 0)
    kcol = lax.broadcasted_iota(jnp.int32, (bq, bk), 1)
    return row, col, lambda kbi: (kbi * bk + kcol) < qpos


def sb_forward(qkv, name="sb_fwd"):
    s_len = qkv.shape[0]
    bk = SB_KEYS
    bq = _pick(s_len, (1024, 512, 256))
    scale = HEAD ** -0.5

    def body(q_ref, k_ref, v_ref, o_ref):
        i = pl.program_id(1)
        q = q_ref[...]
        row, col, valid_of = _sb_masks(i, bq)
        tri_gt = (row > col).astype(BF16)

        def step(kbi, masked, carry):
            acc, a_run = carry
            sl = pl.ds(pl.multiple_of(kbi * bk, bk), bk)
            valid = valid_of(kbi) if masked else None
            _, lk, expo = _sb_tile(q, k_ref[sl, :], scale, valid, tri_gt)
            w = jnp.exp(expo + a_run)
            if masked:
                w = jnp.where(valid, w, 0.0)
            acc = acc + jnp.dot(w.astype(BF16), v_ref[sl, :], preferred_element_type=F32)
            return acc, a_run + jnp.sum(lk, axis=-1, keepdims=True)

        acc, _ = _sb_loops(i, bq // bk, step, (jnp.zeros((bq, HEAD), F32), jnp.zeros((bq, 1), F32)))
        o_ref[...] = acc

    return pl.pallas_call(
        body, name=name, grid=(N_HEADS, s_len // bq),
        in_specs=[pl.BlockSpec((bq, HEAD), lambda h, i: (i, h)),
                  pl.BlockSpec((s_len, HEAD), lambda h, i: (0, N_HEADS + h)),
                  pl.BlockSpec((s_len, HEAD), lambda h, i: (0, 2 * N_HEADS + h))],
        out_specs=pl.BlockSpec((bq, HEAD), lambda h, i: (i, h)),
        out_shape=jax.ShapeDtypeStruct((s_len, BR_WIDTH), F32),
        compiler_params=_params(("parallel", "arbitrary")),
    )(qkv, qkv, qkv)


def sb_backward(qkv, o, do, name="sb_bwd"):
    s_len = qkv.shape[0]
    bk = SB_KEYS
    bq = _pick(s_len, (1024, 512, 256))
    scale = HEAD ** -0.5

    def body(q_ref, k_ref, v_ref, o_ref, do_ref, dq_ref, dk_ref, dv_ref):
        i = pl.program_id(1)

        @pl.when(i == 0)
        def _():
            dk_ref[...] = jnp.zeros_like(dk_ref)
            dv_ref[...] = jnp.zeros_like(dv_ref)

        q = q_ref[...]
        dob = do_ref[...].astype(BF16)
        delta = jnp.sum(dob.astype(F32) * o_ref[...], axis=-1, keepdims=True)
        row, col, valid_of = _sb_masks(i, bq)
        tri_gt = (row > col).astype(BF16)
        tri_ge = (row >= col).astype(BF16)

        def step(kbi, masked, carry):
            dq, a_run, e_run = carry
            sl = pl.ds(pl.multiple_of(kbi * bk, bk), bk)
            valid = valid_of(kbi) if masked else None
            kb = k_ref[sl, :]
            vb = v_ref[sl, :]
            log_beta, lk, expo = _sb_tile(q, kb, scale, valid, tri_gt)
            w = jnp.exp(expo + a_run)
            if masked:
                w = jnp.where(valid, w, 0.0)
            wb = w.astype(BF16)
            beta = jnp.exp(log_beta)
            dw = lax.dot_general(dob, vb, (((1,), (1,)), ((), ())), preferred_element_type=F32)
            e = dw * wb.astype(F32)
            suffix = e_run + _dot_exact_mask(e, tri_ge)
            dz = (e * (1.0 - beta) - beta * (delta - suffix)) * scale
            if masked:
                dz = jnp.where(valid, dz, 0.0)
            dzb = dz.astype(BF16)
            dq = dq + jnp.dot(dzb, kb, preferred_element_type=F32)
            dk_ref[sl, :] += lax.dot_general(dzb, q, (((0,), (0,)), ((), ())), preferred_element_type=F32)
            dv_ref[sl, :] += lax.dot_general(wb, dob, (((0,), (0,)), ((), ())),
                                             preferred_element_type=F32)
            return (dq, a_run + jnp.sum(lk, axis=-1, keepdims=True),
                    e_run + jnp.sum(e, axis=-1, keepdims=True))

        zero = jnp.zeros((bq, 1), F32)
        dq, _, _ = _sb_loops(i, bq // bk, step, (jnp.zeros((bq, HEAD), F32), zero, zero))
        dq_ref[...] = dq

    blk = pl.BlockSpec((bq, HEAD), lambda h, i: (i, h))
    full = pl.BlockSpec((s_len, HEAD), lambda h, i: (0, h))
    shape = jax.ShapeDtypeStruct((s_len, BR_WIDTH), F32)
    return pl.pallas_call(
        body, name=name, grid=(N_HEADS, s_len // bq),
        in_specs=[blk,
                  pl.BlockSpec((s_len, HEAD), lambda h, i: (0, N_HEADS + h)),
                  pl.BlockSpec((s_len, HEAD), lambda h, i: (0, 2 * N_HEADS + h)),
                  blk, blk],
        out_specs=[blk, full, full],
        out_shape=[shape, shape, shape],
        compiler_params=_params(("parallel", "arbitrary")),
    )(qkv, qkv, qkv, o, do)


ROPE_BASE = 10000.0


def rope_tables(positions):
    half = HEAD // 2
    inv_freq = ROPE_BASE ** (-2.0 * jnp.arange(half, dtype=F32) / HEAD)
    ang = positions.astype(F32)[:, None] * inv_freq
    cos, sin = jnp.cos(ang), jnp.sin(ang)
    return jnp.concatenate([cos, cos], 1), jnp.concatenate([-sin, sin], 1)


def ret_decay_tables(s_len):
    block = _pick(s_len, (512, 256, 128, RET_CHUNK))
    lg = np.log1p(-np.exp2(-5.0 - np.arange(N_HEADS, dtype=np.float64)))
    ac = (lg[:, None] * (np.arange(block) + 1.0)[None, :]).astype(np.float32)
    return jnp.asarray(ac[:, None, :, None]), jnp.asarray(ac[:, None, None, :])


def _head_major(s, tm, d=HEAD, heads=N_HEADS, dtype=F32):
    return ((heads, s, d), dtype, (None, tm, d), lambda h, i: (h, i, 0))


def _head_cols(arr, tm, first_col):
    cb = first_col // HEAD
    return (arr, (tm, HEAD), lambda h, i: (i, cb + h))


def _head_rows(arr, tm):
    return (arr, (None, tm, arr.shape[2]), lambda h, i: (h, i, 0))


def rotary_fwd(proj, cos2, sin2):
    s = proj.shape[0]
    tm = _pick(s, (512, 256, 128))

    def fn(qv, kv, c, sn):
        def rot(t):
            return t * c + pltpu.roll(t, HEAD // 2, 1) * sn
        return rot(qv), rot(kv) * (HEAD ** -0.5)

    tab = lambda t: (t, (tm, HEAD), lambda h, i: (i, 0))
    return bmap("rotary_fwd", fn, (N_HEADS, s // tm),
                [_head_cols(proj, tm, COL_RET), _head_cols(proj, tm, COL_RET + BR_WIDTH), tab(cos2), tab(sin2)],
                [_head_major(s, tm), _head_major(s, tm)])


def rotary_bwd(dqr, dkr, cos2, sin2):
    s = dqr.shape[1]
    tm = _pick(s, (512, 256, 128))

    def fn(dq, dk, c, sn):
        def unrot(t):
            return t * c - pltpu.roll(t, HEAD // 2, 1) * sn
        return unrot(dq), unrot(dk) * (HEAD ** -0.5)

    tab = lambda t: (t, (tm, HEAD), lambda h, i: (i, 0))
    out = ((s, BR_WIDTH), BF16, (tm, HEAD), lambda h, i: (i, h))
    return bmap("rotary_bwd", fn, (N_HEADS, s // tm),
                [_head_rows(dqr, tm), _head_rows(dkr, tm), tab(cos2), tab(sin2)], [out, out])


def _group_norm(o):
    mu = jnp.mean(o, axis=-1, keepdims=True)
    oc = o - mu
    r = lax.rsqrt(jnp.mean(oc * oc, axis=-1, keepdims=True) + NORM_EPS)
    return oc * r, r


def gn_gate_fwd(o, gn_w, proj):
    s = o.shape[1]
    tm = _pick(s, (512, 256, 128))

    def fn(ov, wv, gv):
        on, _ = _group_norm(ov)
        return (on * wv * _silu(gv),)

    w3 = gn_w.reshape(N_HEADS, 1, HEAD)
    return bmap("gn_gate_fwd", fn, (N_HEADS, s // tm),
                [_head_rows(o, tm), (w3, (None, 1, HEAD), lambda h, i: (h, 0, 0)),
                 _head_cols(proj, tm, COL_RET + 3 * BR_WIDTH)],
                [((s, BR_WIDTH), BF16, (tm, HEAD), lambda h, i: (i, h))])[0]


def gn_gate_bwd(o, gn_w, proj, dy):
    s = o.shape[1]
    tm = _pick(s, (512, 256, 128))

    def fn(ov, wv, gv, dyv):
        on, r = _group_norm(ov)
        yn = on * wv
        dyn = dyv * _silu(gv)
        dg = dyv * yn * _dsilu(gv)
        don = dyn * wv
        do = r * (don - jnp.mean(don, axis=-1, keepdims=True) - on * jnp.mean(don * on, axis=-1, keepdims=True))
        return do, dg, jnp.sum(dyn * on, axis=0, keepdims=True)

    w3 = gn_w.reshape(N_HEADS, 1, HEAD)
    do, dg, dw = bmap("gn_gate_bwd", fn, (N_HEADS, s // tm),
                      [_head_rows(o, tm), (w3, (None, 1, HEAD), lambda h, i: (h, 0, 0)),
                       _head_cols(proj, tm, COL_RET + 3 * BR_WIDTH), _head_cols(dy, tm, 0)],
                      [_head_major(s, tm), ((s, BR_WIDTH), BF16, (tm, HEAD), lambda h, i: (i, h))],
                      [((N_HEADS, 1, HEAD), (None, 1, HEAD), lambda h, i: (h, 0, 0))])
    return do, dg, dw.reshape(BR_WIDTH)


XBC_WIDTH = 2048


def _shifted(u, back):
    s = u.shape[0]
    n = SSD_CONV - 1
    if back:
        return [jnp.pad(u, ((0, n - k), (0, 0)))[n - k:] for k in range(n)]
    return [jnp.pad(u, ((n - k, 0), (0, 0)))[:s] for k in range(n)]


def _conv(us, w, b):
    return us[0] * w[0:1] + us[1] * w[1:2] + us[2] * w[2:3] + us[3] * w[3:4] + b


def ssd_conv_fwd(us, conv_w, conv_b):
    s = us[0].shape[0]
    tm = _pick(s, (256, 128))

    def fn(u0, u1, u2, u3, w, b):
        return (_silu(_conv((u0, u1, u2, u3), w, b)),)

    return bmap("ssd_conv_fwd", fn, (s // tm,), [rows(u, tm) for u in us] + [par(conv_w), par(conv_b)],
                [row_out(s, XBC_WIDTH, F32, tm)])[0]


def ssd_conv_bwd_a(us, conv_w, conv_b, dact):
    s = dact.shape[0]
    tm = _pick(s, (256, 128))

    def fn(u0, u1, u2, u3, w, b, da):
        dcv = da * _dsilu(_conv((u0, u1, u2, u3), w, b))
        sums = [jnp.sum(dcv * u, axis=0, keepdims=True) for u in (u0, u1, u2, u3)]
        return (dcv, *sums, jnp.sum(dcv, axis=0, keepdims=True))

    res = bmap("ssd_conv_bwd_a", fn, (s // tm,),
               [rows(u, tm) for u in us] + [par(conv_w), par(conv_b), rows(dact, tm)],
               [row_out(s, XBC_WIDTH, F32, tm)], [col_acc(XBC_WIDTH)] * 5)
    return res[0], jnp.concatenate(res[1:5], axis=0), res[5][0]


def ssd_conv_bwd_b(dcs, conv_w):
    s = dcs[0].shape[0]
    tm = _pick(s, (256, 128))

    def fn(d0, d1, d2, d3, w):
        return (d0 * w[0:1] + d1 * w[1:2] + d2 * w[2:3] + d3 * w[3:4],)

    return bmap("ssd_conv_bwd_b", fn, (s // tm,), [rows(d, tm) for d in dcs] + [par(conv_w)],
                [row_out(s, XBC_WIDTH, BF16, tm)])[0]


def _mask_dot_exact(mask_bf16, x):
    hi = x.astype(BF16)
    r1 = x - hi.astype(F32)
    mid = r1.astype(BF16)
    lo = (r1 - mid.astype(F32)).astype(BF16)
    return (jnp.dot(mask_bf16, hi, preferred_element_type=F32) + jnp.dot(mask_bf16, mid, preferred_element_type=F32)
            + jnp.dot(mask_bf16, lo, preferred_element_type=F32))


def ssd_dt_fwd(proj, dt_bias, a_log):
    s = proj.shape[0]
    L = SSD_CHUNK

    def fn(dtr, bias, alog):
        dt = _softplus(dtr + bias)
        a = dt * (-jnp.exp(alog))
        row = lax.broadcasted_iota(jnp.int32, (L, L), 0)
        col = lax.broadcasted_iota(jnp.int32, (L, L), 1)
        return dt, _mask_dot_exact((row >= col).astype(BF16), a)

    return bmap("ssd_dt_fwd", fn, (s // L,), [rows(proj, L, LANES, COL_DT), par(dt_bias), par(a_log)],
                [row_out(s, LANES, F32, L), row_out(s, LANES, F32, L)])


def ssd_state_dot(s_fwd, s_rev, a_tot):
    heads, nc = s_fwd.shape[0], s_fwd.shape[1]
    cb = _pick(nc, (8, 4, 2, 1))

    def fn(sf, sr, at):
        return (jnp.sum(jnp.sum(sf * sr, axis=2, keepdims=True), axis=1, keepdims=True) * jnp.exp(at),)

    st = lambda a: (a, (None, cb) + a.shape[2:], lambda h, i: (h, i, 0, 0))
    return bmap("ssd_state_dot", fn, (heads, nc // cb), [st(s_fwd), st(s_rev), st(a_tot)],
                [((heads, nc, 1, 1), F32, (None, cb, 1, 1), lambda h, i: (h, i, 0, 0))])[0]


def ssd_decay_grad(e_incl, v_excl, g_chunk, own):
    s = e_incl.shape[0]
    L = SSD_CHUNK

    def fn(ev, vv, gv, ov):
        row = lax.broadcasted_iota(jnp.int32, (L, L), 0)
        col = lax.broadcasted_iota(jnp.int32, (L, L), 1)
        return (_mask_dot_exact((col >= row).astype(BF16), ev) + _mask_dot_exact((col < row).astype(BF16), vv)
                + gv + ov,)

    return bmap("ssd_decay_grad", fn, (s // L,),
                [rows(e_incl, L), rows(v_excl, L), (g_chunk, (None, 1, LANES), lambda i: (i, 0, 0)), rows(own, L)],
                [row_out(s, LANES, F32, L)])[0]


def ssd_decay_grad_own(c_grp, b_grp, dy_h, xdt_h, ac, ar):
    s_len = dy_h.shape[1]
    L = SSD_CHUNK
    nc = s_len // L
    cpb = _pick(nc, (8, 4, 2, 1))
    rep = SSD_HEADS // SSD_GROUPS

    def grp_spec(op):
        cb = op[1] // HEAD
        return pl.BlockSpec((L * cpb, HEAD), lambda h, i: (i, cb + h // rep))

    def body(c_ref, b_ref, dy_ref, x_ref, ac_ref, ar_ref, o_ref):
        row = lax.broadcasted_iota(jnp.int32, (L, L), 0)
        col = lax.broadcasted_iota(jnp.int32, (L, L), 1)
        earlier = (row < col).astype(BF16)

        def one(c, carry):
            sl = pl.ds(pl.multiple_of(c * L, L), L)
            seg = ac_ref[c] - ar_ref[c]
            dec = jnp.where(row > col, jnp.exp(jnp.minimum(seg, 0.0)), 0.0)
            nt = (((1,), (1,)), ((), ()))
            sc = lax.dot_general(c_ref[sl, :].astype(BF16), b_ref[sl, :].astype(BF16), nt, preferred_element_type=F32)
            dx = lax.dot_general(dy_ref[sl, :].astype(BF16), x_ref[sl, :].astype(BF16), nt, preferred_element_type=F32)
            before_r = _dot_exact_mask(sc * dec * dx, earlier)
            o_ref[c] = jnp.sum(jnp.where(row >= col, before_r, 0.0), axis=0, keepdims=True)
            return carry

        lax.fori_loop(0, cpb, one, 0)

    seq = lambda a: pl.BlockSpec((None, L * cpb, a.shape[2]), lambda h, i: (h, i, 0))
    dec_spec = lambda shape: pl.BlockSpec((None, cpb) + shape, lambda h, i: (h, i, 0, 0))
    return pl.pallas_call(
        body, name="ssd_decay_grad_own", grid=(SSD_HEADS, nc // cpb),
        in_specs=[grp_spec(c_grp), grp_spec(b_grp), seq(dy_h), seq(xdt_h), dec_spec((L, 1)), dec_spec((1, L))],
        out_specs=pl.BlockSpec((None, cpb, 1, L), lambda h, i: (h, i, 0, 0)),
        out_shape=jax.ShapeDtypeStruct((SSD_HEADS, nc, 1, L), F32),
        compiler_params=_params(("parallel", "parallel")),
    )(c_grp[0], b_grp[0], dy_h, xdt_h, ac, ar)


def ssd_dt_bwd(da, ddt_x, proj, dt_bias, a_log):
    s = da.shape[0]
    tm = _pick(s, (512, 256, 128))

    def fn(dav, dxv, dtr, bias, alog):
        pre = dtr + bias
        a_coef = -jnp.exp(alog)
        ddtr = (dav * a_coef + dxv) * _sigmoid(pre)
        return (ddtr, jnp.sum(ddtr, axis=0, keepdims=True),
                jnp.sum(dav * _softplus(pre) * a_coef, axis=0, keepdims=True))

    ddtr, dbias, dalog = bmap("ssd_dt_bwd", fn, (s // tm,),
                              [rows(da, tm), rows(ddt_x, tm), rows(proj, tm, LANES, COL_DT), par(dt_bias), par(a_log)],
                              [row_out(s, LANES, BF16, tm)], [col_acc(LANES), col_acc(LANES)])
    return ddtr, dbias[0], dalog[0]


def ssd_xdt(x_h, dt_h):
    s = x_h.shape[1]
    tm = _pick(s, (512, 256, 128))
    return bmap("ssd_xdt", lambda xv, dv: (xv * dv,), (SSD_HEADS, s // tm),
                [_head_rows(x_h, tm), _head_rows(dt_h, tm)], [_head_major(s, tm, SSD_P, SSD_HEADS)])[0]


def ssd_gate_fwd(ytil, act, proj, d_full, norm_w):
    s = ytil.shape[0]
    tm = _pick(s, (256, 128))

    def fn(yt, xv, zv, dv, wv):
        u = (yt + xv * dv) * _silu(zv)
        return (u * lax.rsqrt(jnp.mean(u * u, axis=-1, keepdims=True) + NORM_EPS) * wv,)

    return bmap("ssd_gate_fwd", fn, (s // tm,),
                [rows(ytil, tm), rows(act, tm, BR_WIDTH, 0), rows(proj, tm, BR_WIDTH, COL_Z), par(d_full), par(norm_w)],
                [row_out(s, BR_WIDTH, BF16, tm)])[0]


def ssd_gate_bwd(ytil, act, proj, d_full, norm_w, dout):
    s = ytil.shape[0]
    tm = _pick(s, (256, 128))

    def fn(yt, xv, zv, dv, wv, dov):
        y = yt + xv * dv
        sz = _silu(zv)
        u = y * sz
        r = lax.rsqrt(jnp.mean(u * u, axis=-1, keepdims=True) + NORM_EPS)
        un = u * r
        g = dov * wv
        du = r * (g - un * jnp.mean(g * un, axis=-1, keepdims=True))
        dy = du * sz
        return (dy, du * y * _dsilu(zv), jnp.sum(dov * un, axis=0, keepdims=True),
                jnp.sum(dy * xv, axis=0, keepdims=True))

    dy, dz, dw, dd = bmap("ssd_gate_bwd", fn, (s // tm,),
                          [rows(ytil, tm), rows(act, tm, BR_WIDTH, 0), rows(proj, tm, BR_WIDTH, COL_Z), par(d_full),
                           par(norm_w), rows(dout, tm)],
                          [row_out(s, BR_WIDTH, F32, tm), row_out(s, BR_WIDTH, BF16, tm)],
                          [col_acc(BR_WIDTH), col_acc(BR_WIDTH)])
    return dy, dz, dw[0], dd[0]


def ssd_head_bwd(dy_h, y_cross, dxdt_h, dx_cross, xdt_h, x_h, dt_h, d_h):
    s = dy_h.shape[1]
    tm = _pick(s, (512, 256, 128))

    def fn(dy, yc, dxdt, dxc, xdt, xv, dt, dskip):
        return (dy * dskip + dxdt * dt,
                jnp.sum(dy * yc, axis=-1, keepdims=True),
                jnp.sum(dxc * xdt, axis=-1, keepdims=True),
                jnp.sum(dxdt * xv, axis=-1, keepdims=True))

    one = ((SSD_HEADS, s, 1), F32, (None, tm, 1), lambda h, i: (h, i, 0))
    return bmap("ssd_head_bwd", fn, (SSD_HEADS, s // tm),
                [_head_rows(a, tm) for a in (dy_h, y_cross, dxdt_h, dx_cross, xdt_h, x_h, dt_h)]
                + [(d_h, (None, 1, 1), lambda h, i: (h, 0, 0))],
                [_head_major(s, tm, SSD_P, SSD_HEADS), one, one, one])


def ssd_group_sum(db_h, dc_h):
    s = db_h.shape[1]
    tm = _pick(s, (512, 256, 128))
    e = SSD_HEADS // SSD_GROUPS

    def fn(*v):
        return v[0] + v[1] + v[2] + v[3], v[4] + v[5] + v[6] + v[7]

    def head(arr, j):
        return (arr, (None, tm, SSD_N), lambda g, i: (g * e + j, i, 0))

    out = ((s, SSD_GROUPS * SSD_N), F32, (tm, SSD_N), lambda g, i: (i, g))
    return bmap("ssd_group_sum", fn, (SSD_GROUPS, s // tm),
                [head(db_h, j) for j in range(e)] + [head(dc_h, j) for j in range(e)], [out, out])


def _gate_ins(proj, b_gate, tm):
    b2 = b_gate.reshape(1, 3 * D_MODEL)
    ins = [rows(proj, tm, D_MODEL, COL_GATE + j * D_MODEL) for j in range(3)]
    ins += [(b2, (1, D_MODEL), (lambda j: lambda i: (0, j))(j)) for j in range(3)]
    return ins


def merge_fwd(proj, b_gate, ps):
    s = proj.shape[0]
    tm = 128

    def fn(g0, g1, g2, b0, b1, b2, p0, p1, p2):
        return (_sigmoid(g0 + b0) * p0 + _sigmoid(g1 + b1) * p1 + _sigmoid(g2 + b2) * p2,)

    return bmap("merge_fwd", fn, (s // tm,), _gate_ins(proj, b_gate, tm) + [rows(p, tm) for p in ps],
                [row_out(s, D_MODEL, BF16, tm)])[0]


def merge_bwd(proj, b_gate, ps, dmerged):
    s = proj.shape[0]
    tm = 128

    def fn(g0, g1, g2, b0, b1, b2, p0, p1, p2, dm):
        dps, dgs = [], []
        for g, b, p in ((g0, b0, p0), (g1, b1, p1), (g2, b2, p2)):
            gate = _sigmoid(g + b)
            dps.append(dm * gate)
            dgs.append(dm * p * gate * (1.0 - gate))
        dgl = jnp.concatenate(dgs, axis=1)
        return (*dps, dgl, jnp.sum(dgl, axis=0, keepdims=True))

    res = bmap("merge_bwd", fn, (s // tm,),
               _gate_ins(proj, b_gate, tm) + [rows(p, tm) for p in ps] + [rows(dmerged, tm)],
               [row_out(s, D_MODEL, BF16, tm)] * 3 + [row_out(s, 3 * D_MODEL, BF16, tm)], [col_acc(3 * D_MODEL)])
    return res[0:3], res[3], res[4][0]


def swiglu_fwd(gu):
    s = gu.shape[0]
    tm = 128
    return bmap("swiglu_fwd", lambda g, u: (_silu(g) * u,), (s // tm,),
                [rows(gu, tm, FFN_HIDDEN, 0), rows(gu, tm, FFN_HIDDEN, FFN_HIDDEN)],
                [row_out(s, FFN_HIDDEN, BF16, tm)])[0]


def swiglu_bwd(gu, dact):
    s = gu.shape[0]
    tm = 128

    def fn(g, u, da):
        return (jnp.concatenate([da * u * _dsilu(g), da * _silu(g)], axis=1),)

    return bmap("swiglu_bwd", fn, (s // tm,),
                [rows(gu, tm, FFN_HIDDEN, 0), rows(gu, tm, FFN_HIDDEN, FFN_HIDDEN), rows(dact, tm)],
                [row_out(s, 2 * FFN_HIDDEN, BF16, tm)])[0]


def loss_head(y, target):
    s, d = y.shape
    tm = _pick(s, (256, 128))

    def fn(yv, tv):
        err = yv - tv
        part = jnp.sum(jnp.mean(err * err, axis=-1, keepdims=True), axis=0, keepdims=True)
        return err * (1.0 / d), 0.5 * part

    dy, loss = bmap("loss_head", fn, (s // tm,), [rows(y, tm), rows(target, tm)], [row_out(s, d, F32, tm)],
                    [((1, 1), (1, 1), lambda i: (0, 0))])
    return dy, loss[0, 0]


def _to_heads(a2d, heads, d):
    return a2d.reshape(a2d.shape[0], heads, d).transpose(1, 0, 2)


def _from_heads(a3d):
    return a3d.transpose(1, 0, 2).reshape(a3d.shape[1], a3d.shape[0] * a3d.shape[2])


def _pad_lanes(v):
    return jnp.pad(v.reshape(1, -1), ((0, 0), (0, LANES - v.shape[-1])))


def layer_forward(x, w, p, tabs):
    s = x.shape[0]
    cos2, sin2, ac_ret, ar_ret = tabs
    sv = {"x": x}
    h = rms_fwd(x, p["norm_mix_pre"], BF16, "rms_mix_pre")
    proj = matmul(h, w["w_in"], name="mm_in")
    sv.update(h=h, proj=proj)

    qr, kr = rotary_fwd(proj, cos2, sin2)
    v_ret = (proj, COL_RET + 2 * BR_WIDTH, N_HEADS)
    o_ret = linear_attention(qr, kr, v_ret, ac_ret, ar_ret, chunk=ac_ret.shape[2], sub_chunk=RET_CHUNK, causal=False,
                             reverse=False, heads=N_HEADS, name="ret_fwd")
    y_ret = gn_gate_fwd(o_ret, p["ret_gn_w"], proj)
    sv.update(qr=qr, kr=kr, o_ret=o_ret, y_ret=y_ret)

    qkv = proj[:, COL_SB:COL_SB + 3 * BR_WIDTH].astype(BF16)
    o_sb = sb_forward(qkv)
    y_sb = o_sb.astype(BF16)
    sv.update(qkv=qkv, o_sb=o_sb, y_sb=y_sb)

    xbc = proj[:, COL_XBC:COL_XBC + XBC_WIDTH]
    us = _shifted(xbc, False) + [xbc]
    conv_b = p["ssd_conv_b"].reshape(1, XBC_WIDTH)
    act = ssd_conv_fwd(us, p["ssd_conv_w"], conv_b)
    dt, acum = ssd_dt_fwd(proj, _pad_lanes(p["ssd_dt_bias"]), _pad_lanes(p["ssd_a_log"]))
    nc = s // SSD_CHUNK
    x_h = _to_heads(act[:, :BR_WIDTH], SSD_HEADS, SSD_P)
    dt_h = dt[:, :SSD_HEADS].T[:, :, None]
    acs = acum[:, :SSD_HEADS].T.reshape(SSD_HEADS, nc, SSD_CHUNK)
    ac_h, ar_h = acs[:, :, :, None], acs[:, :, None, :]
    xdt_h = ssd_xdt(x_h, dt_h)
    b_grp = (act, BR_WIDTH, SSD_GROUPS)
    c_grp = (act, BR_WIDTH + SSD_GROUPS * SSD_N, SSD_GROUPS)
    ytil_h, y_cross, s_fwd = linear_attention(c_grp, b_grp, xdt_h, ac_h, ar_h, chunk=SSD_CHUNK, causal=True,
                                              reverse=False, heads=SSD_HEADS, name="ssd_fwd", parts_out=True)
    ytil = _from_heads(ytil_h)
    d_full = jnp.repeat(p["ssd_d"], SSD_P).reshape(1, BR_WIDTH)
    norm_w = p["ssd_norm_w"].reshape(1, BR_WIDTH)
    y_ssd = ssd_gate_fwd(ytil, act, proj, d_full, norm_w)
    sv.update(act=act, x_h=x_h, dt_h=dt_h, ac_h=ac_h, ar_h=ar_h, xdt_h=xdt_h, y_cross=y_cross, s_fwd=s_fwd, ytil=ytil,
              y_ssd=y_ssd)

    ys = (y_ret, y_sb, y_ssd)
    ps = [matmul(y, wb, name="mm_branch") for y, wb in zip(ys, w["w_branch"])]
    merged = merge_fwd(proj, p["b_gate"], ps)
    mo = matmul(merged, w["w_out"], name="mm_out")
    x1 = rms_fwd(mo, p["norm_mix_post"], F32, "rms_mix_post", resid=x)
    sv.update(ps=ps, merged=merged, mo=mo, x1=x1)

    h2 = rms_fwd(x1, p["norm_ffn_pre"], BF16, "rms_ffn_pre")
    gu = matmul(h2, w["w_gu"], name="mm_gu")
    actf = swiglu_fwd(gu)
    f = matmul(actf, w["w_down"], name="mm_down")
    x2 = rms_fwd(f, p["norm_ffn_post"], F32, "rms_ffn_post", resid=x1)
    sv.update(h2=h2, gu=gu, actf=actf, f=f)
    return x2, sv


def layer_backward(dx2, sv, w, p, tabs):
    cos2, sin2, ac_ret, ar_ret = tabs
    proj = sv["proj"]
    s = proj.shape[0]
    gs = {}

    df, gs["norm_ffn_post"] = rms_bwd(sv["f"], p["norm_ffn_post"], dx2, "rms_ffn_post_bwd", out_dtype=BF16)
    dactf = matmul(df, w["w_down"], nt=True, name="mm_down_dx")
    gw_down = matmul(sv["actf"].T, df, out_dtype=BF16, name="mm_down_dw")
    dgu = swiglu_bwd(sv["gu"], dactf)
    dh2 = matmul(dgu, w["w_gu"], nt=True, name="mm_gu_dx")
    gw_gu = matmul(sv["h2"].T, dgu, out_dtype=BF16, name="mm_gu_dw")
    dx1, gs["norm_ffn_pre"] = rms_bwd(sv["x1"], p["norm_ffn_pre"], dh2, "rms_ffn_pre_bwd", resid=dx2)

    dmo, gs["norm_mix_post"] = rms_bwd(sv["mo"], p["norm_mix_post"], dx1, "rms_mix_post_bwd", out_dtype=BF16)
    dmerged = matmul(dmo, w["w_out"], nt=True, name="mm_out_dx")
    gw_out = matmul(sv["merged"].T, dmo, out_dtype=BF16, name="mm_out_dw")
    dps, dgl, gs["b_gate"] = merge_bwd(proj, p["b_gate"], sv["ps"], dmerged)
    ys = (sv["y_ret"], sv["y_sb"], sv["y_ssd"])
    dys = [matmul(dp, wb, nt=True, name="mm_branch_dx") for dp, wb in zip(dps, w["w_branch"])]
    gw_branch = [matmul(y.T, dp, out_dtype=BF16, name="mm_branch_dw") for y, dp in zip(ys, dps)]

    do_h, dg, gs["ret_gn_w"] = gn_gate_bwd(sv["o_ret"], p["ret_gn_w"], proj, dys[0])
    v_ret = (proj, COL_RET + 2 * BR_WIDTH, N_HEADS)
    ret = dict(chunk=ac_ret.shape[2], sub_chunk=RET_CHUNK, causal=False, heads=N_HEADS)
    dqr = linear_attention(do_h, v_ret, sv["kr"], ac_ret, ar_ret, reverse=False, name="ret_dq", **ret)
    dkr = linear_attention(v_ret, do_h, sv["qr"], ac_ret, ar_ret, reverse=True, name="ret_dk", **ret)
    dv = linear_attention(sv["kr"], sv["qr"], do_h, ac_ret, ar_ret, reverse=True, name="ret_dv",
                          out_dtype=BF16, out_2d=True, **ret)
    dq, dk = rotary_bwd(dqr, dkr, cos2, sin2)

    dsq, dsk, dsv = sb_backward(sv["qkv"], sv["o_sb"], dys[1])

    act = sv["act"]
    d_full = jnp.repeat(p["ssd_d"], SSD_P).reshape(1, BR_WIDTH)
    norm_w = p["ssd_norm_w"].reshape(1, BR_WIDTH)
    dy, dz, gs["ssd_norm_w"], dd_lanes = ssd_gate_bwd(sv["ytil"], act, proj, d_full, norm_w, dys[2])
    gs["ssd_d"] = dd_lanes.reshape(SSD_HEADS, SSD_P).sum(axis=1)
    dy_h = _to_heads(dy, SSD_HEADS, SSD_P)
    b_grp = (act, BR_WIDTH, SSD_GROUPS)
    c_grp = (act, BR_WIDTH + SSD_GROUPS * SSD_N, SSD_GROUPS)
    ssd = dict(chunk=SSD_CHUNK, causal=True, heads=SSD_HEADS)
    ac_h, ar_h, xdt_h = sv["ac_h"], sv["ar_h"], sv["xdt_h"]
    dxdt_h, dx_cross, s_rev = linear_attention(b_grp, c_grp, dy_h, ac_h, ar_h, reverse=True, name="ssd_dx",
                                               parts_out=True, **ssd)
    dc_h = linear_attention(dy_h, xdt_h, b_grp, ac_h, ar_h, reverse=False, name="ssd_dc", **ssd)
    db_h = linear_attention(xdt_h, dy_h, c_grp, ac_h, ar_h, reverse=True, name="ssd_db", **ssd)
    d_h = p["ssd_d"].reshape(SSD_HEADS, 1, 1)
    dx_h, e_h, v_h, ddtx_h = ssd_head_bwd(dy_h, sv["y_cross"], dxdt_h, dx_cross, xdt_h, sv["x_h"], sv["dt_h"], d_h)
    lanes = lambda a: jnp.pad(a[:, :, 0].T, ((0, 0), (0, LANES - SSD_HEADS)))
    g_chunk = ssd_state_dot(sv["s_fwd"], s_rev, ac_h[:, :, SSD_CHUNK - 1:, :])
    own = ssd_decay_grad_own(c_grp, b_grp, dy_h, xdt_h, ac_h, ar_h).reshape(SSD_HEADS, s, 1)
    da = ssd_decay_grad(lanes(e_h), lanes(v_h), lanes(g_chunk[:, :, :, 0])[:, None, :], lanes(own))
    dt_bias, a_log = _pad_lanes(p["ssd_dt_bias"]), _pad_lanes(p["ssd_a_log"])
    ddtr, dbias, dalog = ssd_dt_bwd(da, lanes(ddtx_h), proj, dt_bias, a_log)
    gs["ssd_dt_bias"], gs["ssd_a_log"] = dbias[:SSD_HEADS], dalog[:SSD_HEADS]
    db2, dc2 = ssd_group_sum(db_h, dc_h)
    dact = jnp.concatenate([_from_heads(dx_h), db2, dc2], axis=1)
    xbc = proj[:, COL_XBC:COL_XBC + XBC_WIDTH]
    us = _shifted(xbc, False) + [xbc]
    conv_b = p["ssd_conv_b"].reshape(1, XBC_WIDTH)
    dcv, gs["ssd_conv_w"], gs["ssd_conv_b"] = ssd_conv_bwd_a(us, p["ssd_conv_w"], conv_b, dact)
    dxbc = ssd_conv_bwd_b(_shifted(dcv, True) + [dcv], p["ssd_conv_w"])

    dt_cols = jnp.pad(ddtr, ((0, 0), (0, IN_PAD - COL_DT - LANES)))
    dproj = jnp.concatenate([dq, dk, dv, dg, dsq.astype(BF16), dsk.astype(BF16), dsv.astype(BF16),
                             dz, dxbc, dgl, dt_cols], axis=1)
    dh = matmul(dproj, w["w_in"], nt=True, name="mm_in_dx")
    gw_in = matmul(sv["h"].T, dproj, out_dtype=BF16, name="mm_in_dw")
    dx, gs["norm_mix_pre"] = rms_bwd(sv["x"], p["norm_mix_pre"], dh, "rms_mix_pre_bwd", resid=dx1)
    gw = dict(w_in=gw_in, w_branch=gw_branch, w_out=gw_out, w_gu=gw_gu, w_down=gw_down)
    return dx, gw, gs


def in_cols_pad(w):
    zeros = jnp.zeros(w.shape[:-1] + (IN_PAD - IN_COLS,), w.dtype)
    return jnp.concatenate([w[..., :COL_GATE], w[..., COL_GATE + SSD_HEADS:], w[..., COL_GATE:COL_GATE + SSD_HEADS], zeros],
                           axis=-1)


def in_cols_unpad(g):
    return jnp.concatenate([g[..., :COL_GATE], g[..., COL_DT:COL_DT + SSD_HEADS], g[..., COL_GATE:COL_DT]], axis=-1)


W_IN_SHARD = (D_MODEL, IN_COLS // N_SHARD)
BIG = (("w_branch_ret", (BR_WIDTH, D_MODEL // N_SHARD), True),
       ("w_branch_sb", (BR_WIDTH, D_MODEL // N_SHARD), True),
       ("w_branch_ssd", (BR_WIDTH, D_MODEL // N_SHARD), True),
       ("w_out", (D_MODEL // N_SHARD, D_MODEL), False),
       ("ffn_w_gate", (D_MODEL, FFN_HIDDEN // N_SHARD), True),
       ("ffn_w_up", (D_MODEL, FFN_HIDDEN // N_SHARD), True),
       ("ffn_w_down", (FFN_HIDDEN // N_SHARD, D_MODEL), False))
FLAT_ROWS = sum(r * c for _, (r, c), _ in BIG) // LANES
assert FLAT_ROWS * LANES == sum(r * c for _, (r, c), _ in BIG) and FLAT_ROWS % 32 == 0
HALF_ROWS = FLAT_ROWS // 2


def pack_shard(layer_shards, dtype):
    flat = jnp.concatenate([layer_shards[n].reshape(-1).astype(dtype) for n, _, _ in BIG])
    return flat.reshape(FLAT_ROWS, LANES)


def unpack_shard(flat):
    out, off = {}, 0
    v = flat.reshape(-1)
    for n, (r, c), _ in BIG:
        out[n] = v[off:off + r * c].reshape(r, c)
        off += r * c
    return out


def unpack_gathered(full):
    v = full.reshape(N_SHARD, -1)
    out, off = {}, 0
    for n, (r, c), by_cols in BIG:
        seg = v[:, off:off + r * c].reshape(N_SHARD, r, c)
        out[n] = seg.transpose(1, 0, 2).reshape(r, N_SHARD * c) if by_cols else seg.reshape(N_SHARD * r, c)
        off += r * c
    return out


def pack_for_scatter(whole):
    segs = []
    for n, (r, c), by_cols in BIG:
        g = whole[n].astype(BF16)
        g = g.reshape(r, N_SHARD, c).transpose(1, 0, 2) if by_cols else g.reshape(N_SHARD, r, c)
        segs.append(g.reshape(N_SHARD, r * c))
    return jnp.concatenate(segs, axis=1).reshape(N_SHARD, FLAT_ROWS, LANES)


ANY = pl.BlockSpec(memory_space=pl.ANY)


def _place():
    return lax.axis_index("x"), lax.axis_index("y"), lax.axis_index("c")


def _other_chips(x, y):
    return [(1 - x, y), (x, 1 - y), (1 - x, 1 - y)]


def _half(c, rows):
    return pl.ds(pl.multiple_of(c * (rows // 2), 16), rows // 2)


def _sems(n):
    return [pltpu.SemaphoreType.DMA((n,)), pltpu.SemaphoreType.DMA((n,))]


def all_gather_shards(shards):
    n = len(shards)

    def body(*refs):
        x_refs, o_refs = refs[:n], refs[n:2 * n]
        send_sems, recv_sems = refs[2 * n:]
        x, y, c = _place()
        j = 2 * x + y
        sibling = (x, y, 1 - c)
        chips = _other_chips(x, y)

        def copy(k, src, dst, to):
            return pltpu.make_async_remote_copy(src_ref=src, dst_ref=dst, send_sem=send_sems.at[k],
                                                recv_sem=recv_sems.at[k], device_id=to, device_id_type=MESH)

        started = []
        for a, (x_ref, o_ref) in enumerate(zip(x_refs, o_refs)):
            mine = _half(c, x_ref.shape[0])
            for k, (cx, cy) in enumerate(chips):
                cp = copy(6 * a + k, x_ref.at[mine], o_ref.at[j, mine], (cx, cy, c))
                cp.start()
                started.append(cp)
        for a, o_ref in enumerate(o_refs):
            mine = _half(c, o_ref.shape[1])
            for k, (cx, cy) in enumerate(chips):
                landed = o_ref.at[2 * cx + cy, mine]
                copy(6 * a + k, landed, landed, (cx, cy, c)).wait_recv()
                cp = copy(6 * a + 3 + k, landed, landed, sibling)
                cp.start()
                started.append(cp)
        for a, o_ref in enumerate(o_refs):
            other = _half(1 - c, o_ref.shape[1])
            for k, (cx, cy) in enumerate(chips):
                landed = o_ref.at[2 * cx + cy, other]
                copy(6 * a + 3 + k, landed, landed, sibling).wait_recv()
        for cp in started:
            cp.wait_send()

    gathered = pl.pallas_call(
        body, name="all_gather_shards", in_specs=[ANY] * n, out_specs=[ANY] * n,
        out_shape=[jax.ShapeDtypeStruct((N_SHARD,) + s.shape, s.dtype) for s in shards],
        scratch_shapes=_sems(6 * n),
    )(*shards)
    own = 2 * lax.axis_index("x") + lax.axis_index("y")
    slot = lambda g: lax.broadcasted_iota(jnp.int32, (N_SHARD,) + (1,) * (g.ndim - 1), 0)
    return [jnp.where(slot(g) == own, s[None], g) for s, g in zip(shards, gathered)]


def exchange_siblings(gs):
    n = len(gs)

    def body(*refs):
        g_refs, o_refs = refs[:n], refs[n:2 * n]
        send_sems, recv_sems = refs[2 * n:]
        x, y, c = _place()
        copies = []
        for a, (g_ref, o_ref) in enumerate(zip(g_refs, o_refs)):
            cp = pltpu.make_async_remote_copy(src_ref=g_ref.at[:, _half(1 - c, g_ref.shape[1])], dst_ref=o_ref,
                                              send_sem=send_sems.at[a], recv_sem=recv_sems.at[a],
                                              device_id=(x, y, 1 - c), device_id_type=MESH)
            cp.start()
            copies.append(cp)
        for cp in copies:
            cp.wait()

    return pl.pallas_call(
        body, name="exchange_siblings", in_specs=[ANY] * n, out_specs=[ANY] * n,
        out_shape=[jax.ShapeDtypeStruct((N_SHARD, g.shape[1] // 2, g.shape[2]), g.dtype) for g in gs],
        scratch_shapes=_sems(n),
    )(*gs)


def exchange_chips(ps):
    n = len(ps)

    def body(*refs):
        p_refs, o_refs = refs[:n], refs[n:2 * n]
        send_sems, recv_sems, local_sems = refs[2 * n:]
        x, y, c = _place()
        j = 2 * x + y
        copies = []
        for a, (p_ref, o_ref) in enumerate(zip(p_refs, o_refs)):
            local = pltpu.make_async_copy(p_ref.at[j], o_ref.at[j], local_sems.at[a])
            local.start()
            copies.append(local)
            for k, (cx, cy) in enumerate(_other_chips(x, y)):
                cp = pltpu.make_async_remote_copy(src_ref=p_ref.at[2 * cx + cy], dst_ref=o_ref.at[j],
                                                  send_sem=send_sems.at[3 * a + k], recv_sem=recv_sems.at[3 * a + k],
                                                  device_id=(cx, cy, c), device_id_type=MESH)
                cp.start()
                copies.append(cp)
        for cp in copies:
            cp.wait()

    return pl.pallas_call(
        body, name="exchange_chips", in_specs=[ANY] * n, out_specs=[ANY] * n,
        out_shape=[jax.ShapeDtypeStruct(p.shape, p.dtype) for p in ps],
        scratch_shapes=_sems(3 * n) + [pltpu.SemaphoreType.DMA((n,))],
    )(*ps)


def swap_halves(reds):
    n = len(reds)

    def body(*refs):
        r_refs, o_refs = refs[:n], refs[n:2 * n]
        send_sems, recv_sems = refs[2 * n:]
        x, y, c = _place()
        copies = []
        for a, (r_ref, o_ref) in enumerate(zip(r_refs, o_refs)):
            cp = pltpu.make_async_remote_copy(src_ref=r_ref, dst_ref=o_ref, send_sem=send_sems.at[a],
                                              recv_sem=recv_sems.at[a], device_id=(x, y, 1 - c), device_id_type=MESH)
            cp.start()
            copies.append(cp)
        for cp in copies:
            cp.wait()

    return pl.pallas_call(
        body, name="swap_halves", in_specs=[ANY] * n, out_specs=[ANY] * n,
        out_shape=[jax.ShapeDtypeStruct(r.shape, r.dtype) for r in reds],
        scratch_shapes=_sems(n),
    )(*reds)


def _sum_rows(rows, cols):
    best = 16
    for t in range(16, rows + 1, 16):
        if rows % t == 0 and t * cols * 4 <= 1536 * 1024:
            best = t
    return best


def reduce_scatter(gs):
    c = lax.axis_index("c")
    gots = exchange_siblings(gs)
    pairs = []
    for g, got in zip(gs, gots):
        half, cols = got.shape[1], got.shape[2]
        t = _sum_rows(half, cols)
        own = lax.dynamic_slice_in_dim(g, c * half, half, axis=1)
        blk = lambda a: (a, (None, t, cols), lambda j, i: (j, i, 0))
        pairs.append(bmap("sum_siblings", lambda a, b: (a.astype(F32) + b.astype(F32),), (N_SHARD, half // t),
                          [blk(own), blk(got)], [(got.shape, BF16, (None, t, cols), lambda j, i: (j, i, 0))])[0])
    reds = []
    for parts in exchange_chips(pairs):
        half, cols = parts.shape[1], parts.shape[2]
        t = _sum_rows(half, cols)
        part = lambda j: (parts, (None, t, cols), lambda i: (j, i, 0))
        reds.append(bmap("sum_chips",
                         lambda a, b, c_, d: (((a.astype(F32) + b.astype(F32)) + c_.astype(F32)) + d.astype(F32),),
                         (half // t,), [part(j) for j in range(N_SHARD)],
                         [((half, cols), F32, (t, cols), lambda i: (i, 0))])[0])
    out = []
    for mine, theirs in zip(reds, swap_halves(reds)):
        both = jnp.where(c == 0, jnp.stack([mine, theirs]), jnp.stack([theirs, mine]))
        out.append(both.reshape(2 * mine.shape[0], mine.shape[1]))
    return out


def all_reduce_small(buf):
    r = buf.shape[0]

    def body(x_ref, o_ref, land_ref, send_sems, recv_sems):
        x, y, c = _place()
        me = 4 * x + 2 * y + c
        land_ref[me] = x_ref[...]
        copies = []
        for d in range(1, 8):
            peer = tuple(1 - v if (d >> s) & 1 else v for v, s in ((x, 2), (y, 1), (c, 0)))
            cp = pltpu.make_async_remote_copy(src_ref=x_ref, dst_ref=land_ref.at[me], send_sem=send_sems.at[d - 1],
                                              recv_sem=recv_sems.at[d - 1], device_id=peer, device_id_type=MESH)
            cp.start()
            copies.append(cp)
        for cp in copies:
            cp.wait()
        acc = land_ref[0]
        for s in range(1, 8):
            acc = acc + land_ref[s]
        o_ref[...] = acc

    vmem = pl.BlockSpec(memory_space=pltpu.VMEM)
    return pl.pallas_call(
        body, name="all_reduce_small", in_specs=[vmem], out_specs=vmem,
        out_shape=jax.ShapeDtypeStruct((r, LANES), F32),
        scratch_shapes=[pltpu.VMEM((8, r, LANES), F32), pltpu.SemaphoreType.DMA((7,)), pltpu.SemaphoreType.DMA((7,))],
    )(buf)


def adamw(w, g, m, v, name):
    r, c = w.shape
    tm = r
    for cand in (512, 256, 128, 64, 32, 16, 8):
        if r % cand == 0 and cand * c * 4 * 14 <= 24 * 1024 * 1024:
            tm = cand
            break
    bc1 = 1.0 - ADAM_B1 ** ADAM_STEP
    bc2 = 1.0 - ADAM_B2 ** ADAM_STEP

    def fn(wv, gv, mv, vv):
        m2 = ADAM_B1 * mv + (1.0 - ADAM_B1) * gv
        v2 = ADAM_B2 * vv + (1.0 - ADAM_B2) * (gv * gv)
        delta = -ADAM_LR * ((m2 / bc1) / (jnp.sqrt(v2 / bc2) + ADAM_EPS) + ADAM_WD * wv)
        return delta, m2, v2

    return bmap(name, fn, (r // tm,), [rows(a, tm) for a in (w, g, m, v)], [row_out(r, c, F32, tm)] * 3)


WEIGHTS = ("norm_mix_pre", "norm_mix_post", "norm_ffn_pre", "norm_ffn_post", "w_in", "b_gate", "ret_gn_w", "ssd_conv_w",
           "ssd_conv_b", "ssd_dt_bias", "ssd_a_log", "ssd_d", "ssd_norm_w", "w_branch_ret", "w_branch_sb", "w_branch_ssd",
           "w_out", "ffn_w_gate", "ffn_w_up", "ffn_w_down")
BIG_NAMES = tuple(n for n, _, _ in BIG)
SHARDED = ("w_in",) + BIG_NAMES
SMALL = tuple(n for n in WEIGHTS if n not in SHARDED and n != "ssd_conv_w")
CONV_SHARD = XBC_WIDTH // N_SHARD


def _pack_small(parts):
    flat = jnp.concatenate([p.reshape(-1).astype(F32) for p in parts])
    rows_ = -(-flat.shape[0] // (8 * LANES)) * 8
    return jnp.pad(flat, (0, rows_ * LANES - flat.shape[0])).reshape(rows_, LANES)


def _unpack_small(buf, shapes):
    v, out, off = buf.reshape(-1), [], 0
    for shp in shapes:
        n = int(np.prod(shp))
        out.append(v[off:off + n].reshape(shp))
        off += n
    return out


def kernel(x, positions, norm_mix_pre, norm_mix_post, norm_ffn_pre, norm_ffn_post, w_in, b_gate, ret_gn_w, ssd_conv_w, ssd_conv_b, ssd_dt_bias, ssd_a_log, ssd_d, ssd_norm_w, w_branch_ret, w_branch_sb, w_branch_ssd, w_out, ffn_w_gate, ffn_w_up, ffn_w_down, loss_target, m_norm_mix_pre, m_norm_mix_post, m_norm_ffn_pre, m_norm_ffn_post, m_w_in, m_b_gate, m_ret_gn_w, m_ssd_conv_w, m_ssd_conv_b, m_ssd_dt_bias, m_ssd_a_log, m_ssd_d, m_ssd_norm_w, m_w_branch_ret, m_w_branch_sb, m_w_branch_ssd, m_w_out, m_ffn_w_gate, m_ffn_w_up, m_ffn_w_down, v_norm_mix_pre, v_norm_mix_post, v_norm_ffn_pre, v_norm_ffn_post, v_w_in, v_b_gate, v_ret_gn_w, v_ssd_conv_w, v_ssd_conv_b, v_ssd_dt_bias, v_ssd_a_log, v_ssd_d, v_ssd_norm_w, v_w_branch_ret, v_w_branch_sb, v_w_branch_ssd, v_w_out, v_ffn_w_gate, v_ffn_w_up, v_ffn_w_down):
    given = dict(locals())
    wts = {n: given[n] for n in WEIGHTS}
    mom = {n: given["m_" + n] for n in WEIGHTS}
    var = {n: given["v_" + n] for n in WEIGHTS}
    xi, yi, ci = lax.axis_index("x"), lax.axis_index("y"), lax.axis_index("c")
    shard_id = 2 * xi + yi

    conv_mine = jnp.where(ci == 0, ssd_conv_w, 0.0)
    conv_slots = lax.dynamic_update_slice_in_dim(jnp.zeros((DEPTH, SSD_CONV, XBC_WIDTH), F32), conv_mine,
                                                 shard_id * CONV_SHARD, axis=2)
    conv_buf = _pack_small([conv_slots])
    conv_whole = _unpack_small(all_reduce_small(conv_buf), [(DEPTH, SSD_CONV, XBC_WIDTH)])[0]

    tabs = rope_tables(positions[0]) + ret_decay_tables(x.shape[1])
    layer_w, layer_p = [], []
    for l in range(DEPTH):
        flat, w_in_all = all_gather_shards([pack_shard({n: wts[n][l] for n in BIG_NAMES}, BF16), w_in[l].astype(BF16)])
        full = unpack_gathered(flat)
        layer_w.append(dict(
            w_in=in_cols_pad(w_in_all.transpose(1, 0, 2).reshape(D_MODEL, IN_COLS)),
            w_branch=[full["w_branch_ret"], full["w_branch_sb"], full["w_branch_ssd"]],
            w_out=full["w_out"],
            w_gu=jnp.concatenate([full["ffn_w_gate"], full["ffn_w_up"]], axis=1),
            w_down=full["ffn_w_down"]))
        p = {n: wts[n][l] for n in SMALL}
        p["ssd_conv_w"] = conv_whole[l]
        layer_p.append(p)

    act = x[0]
    saved = []
    for l in range(DEPTH):
        act, sv = layer_forward(act, layer_w[l], layer_p[l], tabs)
        saved.append(sv)
    dact, loss_part = loss_head(act, loss_target[0])

    big_grads = [None] * DEPTH
    small_grads = [None] * DEPTH
    for l in reversed(range(DEPTH)):
        dact, gw, small_grads[l] = layer_backward(dact, saved[l], layer_w[l], layer_p[l], tabs)
        gu = gw["w_gu"]
        whole = dict(w_branch_ret=gw["w_branch"][0], w_branch_sb=gw["w_branch"][1],
                     w_branch_ssd=gw["w_branch"][2], w_out=gw["w_out"], ffn_w_gate=gu[:, :FFN_HIDDEN],
                     ffn_w_up=gu[:, FFN_HIDDEN:], ffn_w_down=gw["w_down"])
        g_in = in_cols_unpad(gw["w_in"]).reshape(D_MODEL, N_SHARD, IN_COLS // N_SHARD).transpose(1, 0, 2)
        red_flat, red_in = reduce_scatter([pack_for_scatter(whole), g_in])
        big_grads[l] = unpack_shard(red_flat)
        big_grads[l]["w_in"] = red_in

    small_names = SMALL + ("ssd_conv_w",)
    small_parts = [jnp.stack([small_grads[l][n] for l in range(DEPTH)]) for n in small_names]
    small_shapes = [p.shape for p in small_parts]
    summed = all_reduce_small(_pack_small([loss_part.reshape(1)] + small_parts))
    loss, *small_sum = _unpack_small(summed, [(1,)] + small_shapes)
    grads = dict(zip(small_names, small_sum))
    grads["ssd_conv_w"] = lax.dynamic_slice_in_dim(grads["ssd_conv_w"], shard_id * CONV_SHARD, CONV_SHARD, axis=2)
    for n in SHARDED:
        grads[n] = jnp.stack([big_grads[l][n] for l in range(DEPTH)])

    delta, new_m, new_v = {}, {}, {}
    for n in WEIGHTS:
        shape = wts[n].shape
        two_d = (-1, shape[-1]) if len(shape) == 3 else (8, -1) if shape[-1] * shape[0] % (8 * LANES) == 0 else shape
        args = [a.reshape(two_d) for a in (wts[n], grads[n], mom[n], var[n])]
        d, m2, v2 = adamw(*args, name="adamw_" + n)
        delta[n], new_m[n], new_v[n] = d.reshape(shape), m2.reshape(shape), v2.reshape(shape)

    return (loss.reshape(()), dact[None], *[grads[n] for n in WEIGHTS], *[delta[n] for n in WEIGHTS],
            *[new_m[n] for n in WEIGHTS], *[new_v[n] for n in WEIGHTS])
```

```python
import functools
import math

import numpy as np
import jax
import jax.numpy as jnp
from jax import lax
from jax.experimental import pallas as pl
from jax.experimental.pallas import tpu as pltpu

F32 = jnp.float32
BF16 = jnp.bfloat16
MESH = pl.DeviceIdType.MESH

D_MODEL = 2048
DEPTH = 4
RET_CHUNK = 64
NORM_EPS = 1e-6
HEAD = 128
N_HEADS = 8
BR_WIDTH = 1024
SSD_P = 64
SSD_HEADS = 16
SSD_GROUPS = 4
SSD_N = 128
SSD_CHUNK = 256
SSD_CONV = 4
FFN_HIDDEN = 5632
IN_COLS = 16400
IN_PAD = 16896
COL_RET, COL_SB, COL_Z, COL_XBC, COL_GATE, COL_DT = 0, 4096, 7168, 8192, 10240, 16384
N_SHARD = 4
LANES = 128
VMEM_LIMIT = 48 * 1024 * 1024

ADAM_LR, ADAM_B1, ADAM_B2, ADAM_EPS, ADAM_WD, ADAM_STEP = 0.001, 0.9, 0.999, 1e-08, 0.01, 10


def _pick(n, prefs):
    for p in prefs:
        if n % p == 0:
            return p
    return n


def _params(sem=None):
    kw = dict(vmem_limit_bytes=VMEM_LIMIT)
    if sem is not None:
        kw["dimension_semantics"] = sem
    return pltpu.CompilerParams(**kw)


MATMUL_VMEM = 36 * 1024 * 1024


def _matmul_tiles(m, n, k, out_bytes):
    def options(dim, prefs):
        opts = [p for p in prefs if dim % p == 0]
        return opts or [dim]

    best, best_vol = None, -1
    for tm in options(m, (1408, 1024, 512, 256, 128)):
        for tn in options(n, (1536, 1408, 1024, 512, 256, 128)):
            for tk in options(k, (2048, 1536, 1408, 1024, 512, 256, 128)):
                need = 2 * 2 * (tm * tk + tk * tn) + tm * tn * (4 + 2 * out_bytes)
                vol = tm * tn * tk
                if need <= MATMUL_VMEM and vol > best_vol:
                    best, best_vol = (tm, tn, tk), vol
    return best


def matmul(a, b, *, nt=False, out_dtype=F32, name="matmul"):
    m, k = a.shape
    n = b.shape[0] if nt else b.shape[1]
    tm, tn, tk = _matmul_tiles(m, n, k, jnp.dtype(out_dtype).itemsize)
    nk = k // tk

    def body(a_ref, b_ref, o_ref, acc_ref):
        kk = pl.program_id(2)
        if nt:
            part = lax.dot_general(a_ref[...], b_ref[...], (((1,), (1,)), ((), ())), preferred_element_type=F32)
        else:
            part = jnp.dot(a_ref[...], b_ref[...], preferred_element_type=F32)

        @pl.when(kk == 0)
        def _():
            acc_ref[...] = part

        @pl.when(kk > 0)
        def _():
            acc_ref[...] += part

        @pl.when(kk == nk - 1)
        def _():
            o_ref[...] = acc_ref[...].astype(o_ref.dtype)

    b_spec = pl.BlockSpec((tn, tk), lambda i, j, kk: (j, kk)) if nt else pl.BlockSpec((tk, tn), lambda i, j, kk: (kk, j))
    return pl.pallas_call(
        body, name=name, grid=(m // tm, n // tn, nk),
        in_specs=[pl.BlockSpec((tm, tk), lambda i, j, kk: (i, kk)), b_spec],
        out_specs=pl.BlockSpec((tm, tn), lambda i, j, kk: (i, j)),
        out_shape=jax.ShapeDtypeStruct((m, n), out_dtype),
        scratch_shapes=[pltpu.VMEM((tm, tn), F32)],
        compiler_params=_params(("parallel", "parallel", "arbitrary")),
    )(a, b)


def bmap(name, fn, grid, ins, outs, accs=()):
    n_in, n_out = len(ins), len(outs)
    last = len(grid) - 1

    def store_acc(ref, val, first):
        @pl.when(first)
        def _():
            ref[...] = val

        @pl.when(jnp.logical_not(first))
        def _():
            ref[...] += val

    def body(*refs):
        vals = fn(*[r[...] for r in refs[:n_in]])
        if not isinstance(vals, (tuple, list)):
            vals = (vals,)
        for r, v in zip(refs[n_in:n_in + n_out], vals[:n_out]):
            r[...] = v.astype(r.dtype)
        first = pl.program_id(last) == 0
        for r, v in zip(refs[n_in + n_out:], vals[n_out:]):
            store_acc(r, v.astype(F32), first)

    sem = ("parallel",) * last + (("arbitrary",) if accs else ("parallel",))
    res = pl.pallas_call(
        body, name=name, grid=grid,
        in_specs=[pl.BlockSpec(bs, im) for (_, bs, im) in ins],
        out_specs=[pl.BlockSpec(bs, im) for (_, _, bs, im) in outs] + [pl.BlockSpec(bs, im) for (_, bs, im) in accs],
        out_shape=[jax.ShapeDtypeStruct(s, dt) for (s, dt, _, _) in outs] + [jax.ShapeDtypeStruct(s, F32) for (s, _, _) in accs],
        compiler_params=_params(sem),
    )(*[a for (a, _, _) in ins])
    return res


def rows(a, tm, width=None, col=0):
    width = a.shape[1] if width is None else width
    assert col % width == 0
    cb = col // width
    return (a, (tm, width), lambda i: (i, cb))


def par(a):
    return (a, a.shape, lambda *_: (0,) * a.ndim)


def row_out(s, width, dtype, tm):
    return ((s, width), dtype, (tm, width), lambda i: (i, 0))


def col_acc(width):
    return ((1, width), (1, width), lambda i: (0, 0))


def _sigmoid(x):
    return 1.0 / (1.0 + jnp.exp(-x))


def _silu(x):
    return x * _sigmoid(x)


def _dsilu(x):
    s = _sigmoid(x)
    return s * (1.0 + x * (1.0 - s))


def _softplus(x):
    return jnp.maximum(x, 0.0) + jnp.log(1.0 + jnp.exp(-jnp.abs(x)))


def _split_bf16(x):
    hi = x.astype(BF16)
    lo = (x - hi.astype(F32)).astype(BF16)
    return hi, lo


def _dot_exact_mask(x, mask_bf16):
    hi, lo = _split_bf16(x)
    return (jnp.dot(hi, mask_bf16, preferred_element_type=F32)
            + jnp.dot(lo, mask_bf16, preferred_element_type=F32))


def rms_fwd(x, w, out_dtype, name, resid=None):
    s, d = x.shape
    tm = _pick(s, (256, 128))

    def fn(xv, wv, *rest):
        r = lax.rsqrt(jnp.mean(xv * xv, axis=-1, keepdims=True) + NORM_EPS)
        y = xv * r * wv
        if rest:
            y = y + rest[0]
        return (y,)

    ins = [rows(x, tm), par(w.reshape(1, d))]
    if resid is not None:
        ins.append(rows(resid, tm))
    return bmap(name, fn, (s // tm,), ins, [row_out(s, d, out_dtype, tm)])[0]


def rms_bwd(x, w, dy, name, resid=None, out_dtype=F32):
    s, d = x.shape
    tm = _pick(s, (256, 128))

    def fn(xv, wv, dyv, *rest):
        dyv = dyv.astype(F32)
        r = lax.rsqrt(jnp.mean(xv * xv, axis=-1, keepdims=True) + NORM_EPS)
        xn = xv * r
        g = dyv * wv
        dx = r * (g - xn * jnp.mean(g * xn, axis=-1, keepdims=True))
        if rest:
            dx = dx + rest[0]
        return dx, jnp.sum(dyv * xn, axis=0, keepdims=True)

    ins = [rows(x, tm), par(w.reshape(1, d)), rows(dy, tm)]
    if resid is not None:
        ins.append(rows(resid, tm))
    dx, dw = bmap(name, fn, (s // tm,), ins, [row_out(s, d, out_dtype, tm)], [col_acc(d)])
    return dx, dw[0]


def linear_attention(q, k, v, ac, ar, *, chunk, causal, reverse, heads, name, out_dtype=F32, out_2d=False,
                     parts_out=False, sub_chunk=None):
    def dims(op):
        return (op[0].shape[0], HEAD) if isinstance(op, tuple) else (op.shape[1], op.shape[2])

    s_len, dk = dims(q)
    dv = dims(v)[1]
    L = chunk
    cpb = _pick(s_len // L, (8, 4, 2, 1))
    nb = s_len // (L * cpb)
    a_chunks = ac.shape[1]

    def blk(i):
        return nb - 1 - i if reverse else i

    def seq_spec(op):
        if isinstance(op, tuple):
            _, first_col, nh = op
            rep, cb = heads // nh, first_col // HEAD
            return pl.BlockSpec((L * cpb, HEAD), lambda h, i: (blk(i), cb + h // rep))
        rep = heads // op.shape[0]
        return pl.BlockSpec((None, L * cpb, op.shape[2]), lambda h, i: (h // rep, blk(i), 0))

    def a_spec(arr, shape):
        rep = heads // arr.shape[0]
        if a_chunks == 1:
            return pl.BlockSpec((None, 1) + shape, lambda h, i: (h // rep, 0, 0, 0))
        return pl.BlockSpec((None, cpb) + shape, lambda h, i: (h // rep, blk(i), 0, 0))

    def body(q_ref, k_ref, v_ref, ac_ref, ar_ref, *rest):
        o_refs, state_ref = rest[:-1], rest[-1]

        @pl.when(pl.program_id(1) == 0)
        def _():
            state_ref[...] = jnp.zeros_like(state_ref)

        row = lax.broadcasted_iota(jnp.int32, (L, L), 0)
        col = lax.broadcasted_iota(jnp.int32, (L, L), 1)

        def decays(ai):
            a_col = ac_ref[ai]
            a_row = ar_ref[ai]
            a_last = ac_ref[ai, pl.ds(L - 1, 1), :]
            seg = a_col - a_row
            if not causal:
                dec = jnp.exp(-jnp.abs(seg))
                if sub_chunk is not None:
                    shift = sub_chunk.bit_length() - 1
                    rs, cs = jnp.right_shift(row, shift), jnp.right_shift(col, shift)
                    dec = jnp.where(cs >= rs if reverse else cs <= rs, dec, 0.0)
            elif reverse:
                dec = jnp.where(col > row, jnp.exp(jnp.minimum(-seg, 0.0)), 0.0)
            else:
                dec = jnp.where(row > col, jnp.exp(jnp.minimum(seg, 0.0)), 0.0)
            if reverse:
                qd, kd = jnp.exp(a_last - a_col), jnp.exp(a_col)
            else:
                qd, kd = jnp.exp(a_col), jnp.exp(a_last - a_col)
            return dec, qd, kd, jnp.exp(a_last)

        fixed = decays(0) if a_chunks == 1 else None

        def one(step, carry):
            c = cpb - 1 - step if reverse else step
            sl = pl.ds(pl.multiple_of(c * L, L), L)
            qc = q_ref[sl, :].astype(F32)
            kc = k_ref[sl, :].astype(F32)
            vf = v_ref[sl, :].astype(F32)
            vc = vf.astype(BF16)
            dec, qd, kd, chunk_decay = fixed if a_chunks == 1 else decays(c)
            sc = lax.dot_general(qc.astype(BF16), kc.astype(BF16), (((1,), (1,)), ((), ())),
                                 preferred_element_type=F32) * dec
            state = state_ref[...]
            intra = jnp.dot(sc.astype(BF16), vc, preferred_element_type=F32)
            cross = jnp.dot((qc * qd).astype(BF16), state.astype(BF16), preferred_element_type=F32)
            out = intra + cross
            if parts_out:
                o_refs[1][sl, :] = cross
                o_refs[2][c] = state
            if causal:
                out = out + jnp.sum(qc * kc, axis=-1, keepdims=True) * vf
            o_refs[0][sl, :] = out.astype(o_refs[0].dtype)
            upd = lax.dot_general((kc * kd).astype(BF16), vc, (((0,), (0,)), ((), ())),
                                  preferred_element_type=F32)
            state_ref[...] = state * chunk_decay + upd
            return carry

        lax.fori_loop(0, cpb, one, 0)

    if out_2d:
        out_spec = pl.BlockSpec((L * cpb, dv), lambda h, i: (blk(i), h))
        out_shape = jax.ShapeDtypeStruct((s_len, heads * dv), out_dtype)
    else:
        out_spec = pl.BlockSpec((None, L * cpb, dv), lambda h, i: (h, blk(i), 0))
        out_shape = jax.ShapeDtypeStruct((heads, s_len, dv), out_dtype)
    out_specs, out_shapes = [out_spec], [out_shape]
    if parts_out:
        assert not out_2d and out_dtype == F32
        out_specs += [out_spec, pl.BlockSpec((None, cpb, dk, dv), lambda h, i: (h, blk(i), 0, 0))]
        out_shapes += [out_shape, jax.ShapeDtypeStruct((heads, s_len // L, dk, dv), F32)]
    arrs = [op[0] if isinstance(op, tuple) else op for op in (q, k, v)]
    res = pl.pallas_call(
        body, name=name, grid=(heads, nb),
        in_specs=[seq_spec(q), seq_spec(k), seq_spec(v), a_spec(ac, (L, 1)), a_spec(ar, (1, L))],
        out_specs=out_specs, out_shape=out_shapes,
        scratch_shapes=[pltpu.VMEM((dk, dv), F32)],
        compiler_params=_params(("parallel", "arbitrary")),
    )(*arrs, ac, ar)
    return res if parts_out else res[0]


SB_KEYS = 256


def _sb_tile(q, kb, scale, valid, tri):
    z = lax.dot_general(q, kb, (((1,), (1,)), ((), ())), preferred_element_type=F32) * scale
    sp = _softplus(z)
    log_beta = z - sp
    lk = -sp if valid is None else jnp.where(valid, -sp, 0.0)
    later_in = _dot_exact_mask(lk, tri)
    return log_beta, lk, log_beta + later_in


def _sb_loops(i, ratio, step, init):
    carry = lax.fori_loop(0, ratio, lambda n, c: step((i + 1) * ratio - 1 - n, True, c), init)
    return lax.fori_loop(0, i * ratio, lambda n, c: step(i * ratio - 1 - n, False, c), carry)


def _sb_masks(i, bq):
    bk = SB_KEYS
    row = lax.broadcasted_iota(jnp.int32, (bk, bk), 0)
    col = lax.broadcasted_iota(jnp.int32, (bk, bk), 1)
    qpos = i * bq + lax.broadcasted_iota(jnp.int32, (bq, bk), 0)
    kcol = lax.broadcasted_iota(jnp.int32, (bq, bk), 1)
    return row, col, lambda kbi: (kbi * bk + kcol) < qpos


def sb_forward(qkv, name="sb_fwd"):
    s_len = qkv.shape[0]
    bk = SB_KEYS
    bq = _pick(s_len, (1024, 512, 256))
    scale = HEAD ** -0.5

    def body(q_ref, k_ref, v_ref, o_ref):
        i = pl.program_id(1)
        q = q_ref[...]
        row, col, valid_of = _sb_masks(i, bq)
        tri_gt = (row > col).astype(BF16)

        def step(kbi, masked, carry):
            acc, a_run = carry
            sl = pl.ds(pl.multiple_of(kbi * bk, bk), bk)
            valid = valid_of(kbi) if masked else None
            _, lk, expo = _sb_tile(q, k_ref[sl, :], scale, valid, tri_gt)
            w = jnp.exp(expo + a_run)
            if masked:
                w = jnp.where(valid, w, 0.0)
            acc = acc + jnp.dot(w.astype(BF16), v_ref[sl, :], preferred_element_type=F32)
            return acc, a_run + jnp.sum(lk, axis=-1, keepdims=True)

        acc, _ = _sb_loops(i, bq // bk, step, (jnp.zeros((bq, HEAD), F32), jnp.zeros((bq, 1), F32)))
        o_ref[...] = acc

    return pl.pallas_call(
        body, name=name, grid=(N_HEADS, s_len // bq),
        in_specs=[pl.BlockSpec((bq, HEAD), lambda h, i: (i, h)),
                  pl.BlockSpec((s_len, HEAD), lambda h, i: (0, N_HEADS + h)),
                  pl.BlockSpec((s_len, HEAD), lambda h, i: (0, 2 * N_HEADS + h))],
        out_specs=pl.BlockSpec((bq, HEAD), lambda h, i: (i, h)),
        out_shape=jax.ShapeDtypeStruct((s_len, BR_WIDTH), F32),
        compiler_params=_params(("parallel", "arbitrary")),
    )(qkv, qkv, qkv)


def sb_backward(qkv, o, do, name="sb_bwd"):
    s_len = qkv.shape[0]
    bk = SB_KEYS
    bq = _pick(s_len, (1024, 512, 256))
    scale = HEAD ** -0.5

    def body(q_ref, k_ref, v_ref, o_ref, do_ref, dq_ref, dk_ref, dv_ref):
        i = pl.program_id(1)

        @pl.when(i == 0)
        def _():
            dk_ref[...] = jnp.zeros_like(dk_ref)
            dv_ref[...] = jnp.zeros_like(dv_ref)

        q = q_ref[...]
        dob = do_ref[...].astype(BF16)
        delta = jnp.sum(dob.astype(F32) * o_ref[...], axis=-1, keepdims=True)
        row, col, valid_of = _sb_masks(i, bq)
        tri_gt = (row > col).astype(BF16)
        tri_ge = (row >= col).astype(BF16)

        def step(kbi, masked, carry):
            dq, a_run, e_run = carry
            sl = pl.ds(pl.multiple_of(kbi * bk, bk), bk)
            valid = valid_of(kbi) if masked else None
            kb = k_ref[sl, :]
            vb = v_ref[sl, :]
            log_beta, lk, expo = _sb_tile(q, kb, scale, valid, tri_gt)
            w = jnp.exp(expo + a_run)
            if masked:
                w = jnp.where(valid, w, 0.0)
            wb = w.astype(BF16)
            beta = jnp.exp(log_beta)
            dw = lax.dot_general(dob, vb, (((1,), (1,)), ((), ())), preferred_element_type=F32)
            e = dw * wb.astype(F32)
            suffix = e_run + _dot_exact_mask(e, tri_ge)
            dz = (e * (1.0 - beta) - beta * (delta - suffix)) * scale
            if masked:
                dz = jnp.where(valid, dz, 0.0)
            dzb = dz.astype(BF16)
            dq = dq + jnp.dot(dzb, kb, preferred_element_type=F32)
            dk_ref[sl, :] += lax.dot_general(dzb, q, (((0,), (0,)), ((), ())), preferred_element_type=F32)
            dv_ref[sl, :] += lax.dot_general(wb, dob, (((0,), (0,)), ((), ())),
                                             preferred_element_type=F32)
            return (dq, a_run + jnp.sum(lk, axis=-1, keepdims=True),
                    e_run + jnp.sum(e, axis=-1, keepdims=True))

        zero = jnp.zeros((bq, 1), F32)
        dq, _, _ = _sb_loops(i, bq // bk, step, (jnp.zeros((bq, HEAD), F32), zero, zero))
        dq_ref[...] = dq

    blk = pl.BlockSpec((bq, HEAD), lambda h, i: (i, h))
    full = pl.BlockSpec((s_len, HEAD), lambda h, i: (0, h))
    shape = jax.ShapeDtypeStruct((s_len, BR_WIDTH), F32)
    return pl.pallas_call(
        body, name=name, grid=(N_HEADS, s_len // bq),
        in_specs=[blk,
                  pl.BlockSpec((s_len, HEAD), lambda h, i: (0, N_HEADS + h)),
                  pl.BlockSpec((s_len, HEAD), lambda h, i: (0, 2 * N_HEADS + h)),
                  blk, blk],
        out_specs=[blk, full, full],
        out_shape=[shape, shape, shape],
        compiler_params=_params(("parallel", "arbitrary")),
    )(qkv, qkv, qkv, o, do)


ROPE_BASE = 10000.0


def rope_tables(positions):
    half = HEAD // 2
    inv_freq = ROPE_BASE ** (-2.0 * jnp.arange(half, dtype=F32) / HEAD)
    ang = positions.astype(F32)[:, None] * inv_freq
    cos, sin = jnp.cos(ang), jnp.sin(ang)
    return jnp.concatenate([cos, cos], 1), jnp.concatenate([-sin, sin], 1)


def ret_decay_tables(s_len):
    block = _pick(s_len, (512, 256, 128, RET_CHUNK))
    lg = np.log1p(-np.exp2(-5.0 - np.arange(N_HEADS, dtype=np.float64)))
    ac = (lg[:, None] * (np.arange(block) + 1.0)[None, :]).astype(np.float32)
    return jnp.asarray(ac[:, None, :, None]), jnp.asarray(ac[:, None, None, :])


def _head_major(s, tm, d=HEAD, heads=N_HEADS, dtype=F32):
    return ((heads, s, d), dtype, (None, tm, d), lambda h, i: (h, i, 0))


def _head_cols(arr, tm, first_col):
    cb = first_col // HEAD
    return (arr, (tm, HEAD), lambda h, i: (i, cb + h))


def _head_rows(arr, tm):
    return (arr, (None, tm, arr.shape[2]), lambda h, i: (h, i, 0))


def rotary_fwd(proj, cos2, sin2):
    s = proj.shape[0]
    tm = _pick(s, (512, 256, 128))

    def fn(qv, kv, c, sn):
        def rot(t):
            return t * c + pltpu.roll(t, HEAD // 2, 1) * sn
        return rot(qv), rot(kv) * (HEAD ** -0.5)

    tab = lambda t: (t, (tm, HEAD), lambda h, i: (i, 0))
    return bmap("rotary_fwd", fn, (N_HEADS, s // tm),
                [_head_cols(proj, tm, COL_RET), _head_cols(proj, tm, COL_RET + BR_WIDTH), tab(cos2), tab(sin2)],
                [_head_major(s, tm), _head_major(s, tm)])


def rotary_bwd(dqr, dkr, cos2, sin2):
    s = dqr.shape[1]
    tm = _pick(s, (512, 256, 128))

    def fn(dq, dk, c, sn):
        def unrot(t):
            return t * c - pltpu.roll(t, HEAD // 2, 1) * sn
        return unrot(dq), unrot(dk) * (HEAD ** -0.5)

    tab = lambda t: (t, (tm, HEAD), lambda h, i: (i, 0))
    out = ((s, BR_WIDTH), BF16, (tm, HEAD), lambda h, i: (i, h))
    return bmap("rotary_bwd", fn, (N_HEADS, s // tm),
                [_head_rows(dqr, tm), _head_rows(dkr, tm), tab(cos2), tab(sin2)], [out, out])


def _group_norm(o):
    mu = jnp.mean(o, axis=-1, keepdims=True)
    oc = o - mu
    r = lax.rsqrt(jnp.mean(oc * oc, axis=-1, keepdims=True) + NORM_EPS)
    return oc * r, r


def gn_gate_fwd(o, gn_w, proj):
    s = o.shape[1]
    tm = _pick(s, (512, 256, 128))

    def fn(ov, wv, gv):
        on, _ = _group_norm(ov)
        return (on * wv * _silu(gv),)

    w3 = gn_w.reshape(N_HEADS, 1, HEAD)
    return bmap("gn_gate_fwd", fn, (N_HEADS, s // tm),
                [_head_rows(o, tm), (w3, (None, 1, HEAD), lambda h, i: (h, 0, 0)),
                 _head_cols(proj, tm, COL_RET + 3 * BR_WIDTH)],
                [((s, BR_WIDTH), BF16, (tm, HEAD), lambda h, i: (i, h))])[0]


def gn_gate_bwd(o, gn_w, proj, dy):
    s = o.shape[1]
    tm = _pick(s, (512, 256, 128))

    def fn(ov, wv, gv, dyv):
        on, r = _group_norm(ov)
        yn = on * wv
        dyn = dyv * _silu(gv)
        dg = dyv * yn * _dsilu(gv)
        don = dyn * wv
        do = r * (don - jnp.mean(don, axis=-1, keepdims=True) - on * jnp.mean(don * on, axis=-1, keepdims=True))
        return do, dg, jnp.sum(dyn * on, axis=0, keepdims=True)

    w3 = gn_w.reshape(N_HEADS, 1, HEAD)
    do, dg, dw = bmap("gn_gate_bwd", fn, (N_HEADS, s // tm),
                      [_head_rows(o, tm), (w3, (None, 1, HEAD), lambda h, i: (h, 0, 0)),
                       _head_cols(proj, tm, COL_RET + 3 * BR_WIDTH), _head_cols(dy, tm, 0)],
                      [_head_major(s, tm), ((s, BR_WIDTH), BF16, (tm, HEAD), lambda h, i: (i, h))],
                      [((N_HEADS, 1, HEAD), (None, 1, HEAD), lambda h, i: (h, 0, 0))])
    return do, dg, dw.reshape(BR_WIDTH)


XBC_WIDTH = 2048


def _shifted(u, back):
    s = u.shape[0]
    n = SSD_CONV - 1
    if back:
        return [jnp.pad(u, ((0, n - k), (0, 0)))[n - k:] for k in range(n)]
    return [jnp.pad(u, ((n - k, 0), (0, 0)))[:s] for k in range(n)]


def _conv(us, w, b):
    return us[0] * w[0:1] + us[1] * w[1:2] + us[2] * w[2:3] + us[3] * w[3:4] + b


def ssd_conv_fwd(us, conv_w, conv_b):
    s = us[0].shape[0]
    tm = _pick(s, (256, 128))

    def fn(u0, u1, u2, u3, w, b):
        return (_silu(_conv((u0, u1, u2, u3), w, b)),)

    return bmap("ssd_conv_fwd", fn, (s // tm,), [rows(u, tm) for u in us] + [par(conv_w), par(conv_b)],
                [row_out(s, XBC_WIDTH, F32, tm)])[0]


def ssd_conv_bwd_a(us, conv_w, conv_b, dact):
    s = dact.shape[0]
    tm = _pick(s, (256, 128))

    def fn(u0, u1, u2, u3, w, b, da):
        dcv = da * _dsilu(_conv((u0, u1, u2, u3), w, b))
        sums = [jnp.sum(dcv * u, axis=0, keepdims=True) for u in (u0, u1, u2, u3)]
        return (dcv, *sums, jnp.sum(dcv, axis=0, keepdims=True))

    res = bmap("ssd_conv_bwd_a", fn, (s // tm,),
               [rows(u, tm) for u in us] + [par(conv_w), par(conv_b), rows(dact, tm)],
               [row_out(s, XBC_WIDTH, F32, tm)], [col_acc(XBC_WIDTH)] * 5)
    return res[0], jnp.concatenate(res[1:5], axis=0), res[5][0]


def ssd_conv_bwd_b(dcs, conv_w):
    s = dcs[0].shape[0]
    tm = _pick(s, (256, 128))

    def fn(d0, d1, d2, d3, w):
        return (d0 * w[0:1] + d1 * w[1:2] + d2 * w[2:3] + d3 * w[3:4],)

    return bmap("ssd_conv_bwd_b", fn, (s // tm,), [rows(d, tm) for d in dcs] + [par(conv_w)],
                [row_out(s, XBC_WIDTH, BF16, tm)])[0]


def _mask_dot_exact(mask_bf16, x):
    hi = x.astype(BF16)
    r1 = x - hi.astype(F32)
    mid = r1.astype(BF16)
    lo = (r1 - mid.astype(F32)).astype(BF16)
    return (jnp.dot(mask_bf16, hi, preferred_element_type=F32) + jnp.dot(mask_bf16, mid, preferred_element_type=F32)
            + jnp.dot(mask_bf16, lo, preferred_element_type=F32))


def ssd_dt_fwd(proj, dt_bias, a_log):
    s = proj.shape[0]
    L = SSD_CHUNK

    def fn(dtr, bias, alog):
        dt = _softplus(dtr + bias)
        a = dt * (-jnp.exp(alog))
        row = lax.broadcasted_iota(jnp.int32, (L, L), 0)
        col = lax.broadcasted_iota(jnp.int32, (L, L), 1)
        return dt, _mask_dot_exact((row >= col).astype(BF16), a)

    return bmap("ssd_dt_fwd", fn, (s // L,), [rows(proj, L, LANES, COL_DT), par(dt_bias), par(a_log)],
                [row_out(s, LANES, F32, L), row_out(s, LANES, F32, L)])


def ssd_state_dot(s_fwd, s_rev, a_tot):
    heads, nc = s_fwd.shape[0], s_fwd.shape[1]
    cb = _pick(nc, (8, 4, 2, 1))

    def fn(sf, sr, at):
        return (jnp.sum(jnp.sum(sf * sr, axis=2, keepdims=True), axis=1, keepdims=True) * jnp.exp(at),)

    st = lambda a: (a, (None, cb) + a.shape[2:], lambda h, i: (h, i, 0, 0))
    return bmap("ssd_state_dot", fn, (heads, nc // cb), [st(s_fwd), st(s_rev), st(a_tot)],
                [((heads, nc, 1, 1), F32, (None, cb, 1, 1), lambda h, i: (h, i, 0, 0))])[0]


def ssd_decay_grad(e_incl, v_excl, g_chunk, own):
    s = e_incl.shape[0]
    L = SSD_CHUNK

    def fn(ev, vv, gv, ov):
        row = lax.broadcasted_iota(jnp.int32, (L, L), 0)
        col = lax.broadcasted_iota(jnp.int32, (L, L), 1)
        return (_mask_dot_exact((col >= row).astype(BF16), ev) + _mask_dot_exact((col < row).astype(BF16), vv)
                + gv + ov,)

    return bmap("ssd_decay_grad", fn, (s // L,),
                [rows(e_incl, L), rows(v_excl, L), (g_chunk, (None, 1, LANES), lambda i: (i, 0, 0)), rows(own, L)],
                [row_out(s, LANES, F32, L)])[0]


def ssd_decay_grad_own(c_grp, b_grp, dy_h, xdt_h, ac, ar):
    s_len = dy_h.shape[1]
    L = SSD_CHUNK
    nc = s_len // L
    cpb = _pick(nc, (8, 4, 2, 1))
    rep = SSD_HEADS // SSD_GROUPS

    def grp_spec(op):
        cb = op[1] // HEAD
        return pl.BlockSpec((L * cpb, HEAD), lambda h, i: (i, cb + h // rep))

    def body(c_ref, b_ref, dy_ref, x_ref, ac_ref, ar_ref, o_ref):
        row = lax.broadcasted_iota(jnp.int32, (L, L), 0)
        col = lax.broadcasted_iota(jnp.int32, (L, L), 1)
        earlier = (row < col).astype(BF16)

        def one(c, carry):
            sl = pl.ds(pl.multiple_of(c * L, L), L)
            seg = ac_ref[c] - ar_ref[c]
            dec = jnp.where(row > col, jnp.exp(jnp.minimum(seg, 0.0)), 0.0)
            nt = (((1,), (1,)), ((), ()))
            sc = lax.dot_general(c_ref[sl, :].astype(BF16), b_ref[sl, :].astype(BF16), nt, preferred_element_type=F32)
            dx = lax.dot_general(dy_ref[sl, :].astype(BF16), x_ref[sl, :].astype(BF16), nt, preferred_element_type=F32)
            before_r = _dot_exact_mask(sc * dec * dx, earlier)
            o_ref[c] = jnp.sum(jnp.where(row >= col, before_r, 0.0), axis=0, keepdims=True)
            return carry

        lax.fori_loop(0, cpb, one, 0)

    seq = lambda a: pl.BlockSpec((None, L * cpb, a.shape[2]), lambda h, i: (h, i, 0))
    dec_spec = lambda shape: pl.BlockSpec((None, cpb) + shape, lambda h, i: (h, i, 0, 0))
    return pl.pallas_call(
        body, name="ssd_decay_grad_own", grid=(SSD_HEADS, nc // cpb),
        in_specs=[grp_spec(c_grp), grp_spec(b_grp), seq(dy_h), seq(xdt_h), dec_spec((L, 1)), dec_spec((1, L))],
        out_specs=pl.BlockSpec((None, cpb, 1, L), lambda h, i: (h, i, 0, 0)),
        out_shape=jax.ShapeDtypeStruct((SSD_HEADS, nc, 1, L), F32),
        compiler_params=_params(("parallel", "parallel")),
    )(c_grp[0], b_grp[0], dy_h, xdt_h, ac, ar)


def ssd_dt_bwd(da, ddt_x, proj, dt_bias, a_log):
    s = da.shape[0]
    tm = _pick(s, (512, 256, 128))

    def fn(dav, dxv, dtr, bias, alog):
        pre = dtr + bias
        a_coef = -jnp.exp(alog)
        ddtr = (dav * a_coef + dxv) * _sigmoid(pre)
        return (ddtr, jnp.sum(ddtr, axis=0, keepdims=True),
                jnp.sum(dav * _softplus(pre) * a_coef, axis=0, keepdims=True))

    ddtr, dbias, dalog = bmap("ssd_dt_bwd", fn, (s // tm,),
                              [rows(da, tm), rows(ddt_x, tm), rows(proj, tm, LANES, COL_DT), par(dt_bias), par(a_log)],
                              [row_out(s, LANES, BF16, tm)], [col_acc(LANES), col_acc(LANES)])
    return ddtr, dbias[0], dalog[0]


def ssd_xdt(x_h, dt_h):
    s = x_h.shape[1]
    tm = _pick(s, (512, 256, 128))
    return bmap("ssd_xdt", lambda xv, dv: (xv * dv,), (SSD_HEADS, s // tm),
                [_head_rows(x_h, tm), _head_rows(dt_h, tm)], [_head_major(s, tm, SSD_P, SSD_HEADS)])[0]


def ssd_gate_fwd(ytil, act, proj, d_full, norm_w):
    s = ytil.shape[0]
    tm = _pick(s, (256, 128))

    def fn(yt, xv, zv, dv, wv):
        u = (yt + xv * dv) * _silu(zv)
        return (u * lax.rsqrt(jnp.mean(u * u, axis=-1, keepdims=True) + NORM_EPS) * wv,)

    return bmap("ssd_gate_fwd", fn, (s // tm,),
                [rows(ytil, tm), rows(act, tm, BR_WIDTH, 0), rows(proj, tm, BR_WIDTH, COL_Z), par(d_full), par(norm_w)],
                [row_out(s, BR_WIDTH, BF16, tm)])[0]


def ssd_gate_bwd(ytil, act, proj, d_full, norm_w, dout):
    s = ytil.shape[0]
    tm = _pick(s, (256, 128))

    def fn(yt, xv, zv, dv, wv, dov):
        y = yt + xv * dv
        sz = _silu(zv)
        u = y * sz
        r = lax.rsqrt(jnp.mean(u * u, axis=-1, keepdims=True) + NORM_EPS)
        un = u * r
        g = dov * wv
        du = r * (g - un * jnp.mean(g * un, axis=-1, keepdims=True))
        dy = du * sz
        return (dy, du * y * _dsilu(zv), jnp.sum(dov * un, axis=0, keepdims=True),
                jnp.sum(dy * xv, axis=0, keepdims=True))

    dy, dz, dw, dd = bmap("ssd_gate_bwd", fn, (s // tm,),
                          [rows(ytil, tm), rows(act, tm, BR_WIDTH, 0), rows(proj, tm, BR_WIDTH, COL_Z), par(d_full),
                           par(norm_w), rows(dout, tm)],
                          [row_out(s, BR_WIDTH, F32, tm), row_out(s, BR_WIDTH, BF16, tm)],
                          [col_acc(BR_WIDTH), col_acc(BR_WIDTH)])
    return dy, dz, dw[0], dd[0]


def ssd_head_bwd(dy_h, y_cross, dxdt_h, dx_cross, xdt_h, x_h, dt_h, d_h):
    s = dy_h.shape[1]
    tm = _pick(s, (512, 256, 128))

    def fn(dy, yc, dxdt, dxc, xdt, xv, dt, dskip):
        return (dy * dskip + dxdt * dt,
                jnp.sum(dy * yc, axis=-1, keepdims=True),
                jnp.sum(dxc * xdt, axis=-1, keepdims=True),
                jnp.sum(dxdt * xv, axis=-1, keepdims=True))

    one = ((SSD_HEADS, s, 1), F32, (None, tm, 1), lambda h, i: (h, i, 0))
    return bmap("ssd_head_bwd", fn, (SSD_HEADS, s // tm),
                [_head_rows(a, tm) for a in (dy_h, y_cross, dxdt_h, dx_cross, xdt_h, x_h, dt_h)]
                + [(d_h, (None, 1, 1), lambda h, i: (h, 0, 0))],
                [_head_major(s, tm, SSD_P, SSD_HEADS), one, one, one])


def ssd_group_sum(db_h, dc_h):
    s = db_h.shape[1]
    tm = _pick(s, (512, 256, 128))
    e = SSD_HEADS // SSD_GROUPS

    def fn(*v):
        return v[0] + v[1] + v[2] + v[3], v[4] + v[5] + v[6] + v[7]

    def head(arr, j):
        return (arr, (None, tm, SSD_N), lambda g, i: (g * e + j, i, 0))

    out = ((s, SSD_GROUPS * SSD_N), F32, (tm, SSD_N), lambda g, i: (i, g))
    return bmap("ssd_group_sum", fn, (SSD_GROUPS, s // tm),
                [head(db_h, j) for j in range(e)] + [head(dc_h, j) for j in range(e)], [out, out])


def _gate_ins(proj, b_gate, tm):
    b2 = b_gate.reshape(1, 3 * D_MODEL)
    ins = [rows(proj, tm, D_MODEL, COL_GATE + j * D_MODEL) for j in range(3)]
    ins += [(b2, (1, D_MODEL), (lambda j: lambda i: (0, j))(j)) for j in range(3)]
    return ins


def merge_fwd(proj, b_gate, ps):
    s = proj.shape[0]
    tm = 128

    def fn(g0, g1, g2, b0, b1, b2, p0, p1, p2):
        return (_sigmoid(g0 + b0) * p0 + _sigmoid(g1 + b1) * p1 + _sigmoid(g2 + b2) * p2,)

    return bmap("merge_fwd", fn, (s // tm,), _gate_ins(proj, b_gate, tm) + [rows(p, tm) for p in ps],
                [row_out(s, D_MODEL, BF16, tm)])[0]


def merge_bwd(proj, b_gate, ps, dmerged):
    s = proj.shape[0]
    tm = 128

    def fn(g0, g1, g2, b0, b1, b2, p0, p1, p2, dm):
        dps, dgs = [], []
        for g, b, p in ((g0, b0, p0), (g1, b1, p1), (g2, b2, p2)):
            gate = _sigmoid(g + b)
            dps.append(dm * gate)
            dgs.append(dm * p * gate * (1.0 - gate))
        dgl = jnp.concatenate(dgs, axis=1)
        return (*dps, dgl, jnp.sum(dgl, axis=0, keepdims=True))

    res = bmap("merge_bwd", fn, (s // tm,),
               _gate_ins(proj, b_gate, tm) + [rows(p, tm) for p in ps] + [rows(dmerged, tm)],
               [row_out(s, D_MODEL, BF16, tm)] * 3 + [row_out(s, 3 * D_MODEL, BF16, tm)], [col_acc(3 * D_MODEL)])
    return res[0:3], res[3], res[4][0]


def swiglu_fwd(gu):
    s = gu.shape[0]
    tm = 128
    return bmap("swiglu_fwd", lambda g, u: (_silu(g.astype(F32)) * u.astype(F32),), (s // tm,),
                [rows(gu, tm, FFN_HIDDEN, 0), rows(gu, tm, FFN_HIDDEN, FFN_HIDDEN)],
                [row_out(s, FFN_HIDDEN, BF16, tm)])[0]


def swiglu_bwd(gu, dact):
    s = gu.shape[0]
    tm = 128

    def fn(g, u, da):
        g, u = g.astype(F32), u.astype(F32)
        return (jnp.concatenate([da * u * _dsilu(g), da * _silu(g)], axis=1),)

    return bmap("swiglu_bwd", fn, (s // tm,),
                [rows(gu, tm, FFN_HIDDEN, 0), rows(gu, tm, FFN_HIDDEN, FFN_HIDDEN), rows(dact, tm)],
                [row_out(s, 2 * FFN_HIDDEN, BF16, tm)])[0]


def loss_head(y, target):
    s, d = y.shape
    tm = _pick(s, (256, 128))

    def fn(yv, tv):
        err = yv - tv
        part = jnp.sum(jnp.mean(err * err, axis=-1, keepdims=True), axis=0, keepdims=True)
        return err * (1.0 / d), 0.5 * part

    dy, loss = bmap("loss_head", fn, (s // tm,), [rows(y, tm), rows(target, tm)], [row_out(s, d, F32, tm)],
                    [((1, 1), (1, 1), lambda i: (0, 0))])
    return dy, loss[0, 0]


def _to_heads(a2d, heads, d):
    return a2d.reshape(a2d.shape[0], heads, d).transpose(1, 0, 2)


def _from_heads(a3d):
    return a3d.transpose(1, 0, 2).reshape(a3d.shape[1], a3d.shape[0] * a3d.shape[2])


def _pad_lanes(v):
    return jnp.pad(v.reshape(1, -1), ((0, 0), (0, LANES - v.shape[-1])))


def layer_forward(x, w, p, tabs):
    s = x.shape[0]
    cos2, sin2, ac_ret, ar_ret = tabs
    sv = {"x": x}
    h = rms_fwd(x, p["norm_mix_pre"], BF16, "rms_mix_pre")
    proj = matmul(h, w["w_in"], name="mm_in")
    sv.update(h=h, proj=proj)

    qr, kr = rotary_fwd(proj, cos2, sin2)
    v_ret = (proj, COL_RET + 2 * BR_WIDTH, N_HEADS)
    o_ret = linear_attention(qr, kr, v_ret, ac_ret, ar_ret, chunk=ac_ret.shape[2], sub_chunk=RET_CHUNK, causal=False,
                             reverse=False, heads=N_HEADS, name="ret_fwd")
    y_ret = gn_gate_fwd(o_ret, p["ret_gn_w"], proj)
    sv.update(qr=qr, kr=kr, o_ret=o_ret, y_ret=y_ret)

    qkv = proj[:, COL_SB:COL_SB + 3 * BR_WIDTH].astype(BF16)
    o_sb = sb_forward(qkv)
    y_sb = o_sb.astype(BF16)
    sv.update(qkv=qkv, o_sb=o_sb, y_sb=y_sb)

    xbc = proj[:, COL_XBC:COL_XBC + XBC_WIDTH]
    us = _shifted(xbc, False) + [xbc]
    conv_b = p["ssd_conv_b"].reshape(1, XBC_WIDTH)
    act = ssd_conv_fwd(us, p["ssd_conv_w"], conv_b)
    dt, acum = ssd_dt_fwd(proj, _pad_lanes(p["ssd_dt_bias"]), _pad_lanes(p["ssd_a_log"]))
    nc = s // SSD_CHUNK
    x_h = _to_heads(act[:, :BR_WIDTH], SSD_HEADS, SSD_P)
    dt_h = dt[:, :SSD_HEADS].T[:, :, None]
    acs = acum[:, :SSD_HEADS].T.reshape(SSD_HEADS, nc, SSD_CHUNK)
    ac_h, ar_h = acs[:, :, :, None], acs[:, :, None, :]
    xdt_h = ssd_xdt(x_h, dt_h)
    b_grp = (act, BR_WIDTH, SSD_GROUPS)
    c_grp = (act, BR_WIDTH + SSD_GROUPS * SSD_N, SSD_GROUPS)
    ytil_h, y_cross, s_fwd = linear_attention(c_grp, b_grp, xdt_h, ac_h, ar_h, chunk=SSD_CHUNK, causal=True,
                                              reverse=False, heads=SSD_HEADS, name="ssd_fwd", parts_out=True)
    ytil = _from_heads(ytil_h)
    d_full = jnp.repeat(p["ssd_d"], SSD_P).reshape(1, BR_WIDTH)
    norm_w = p["ssd_norm_w"].reshape(1, BR_WIDTH)
    y_ssd = ssd_gate_fwd(ytil, act, proj, d_full, norm_w)
    sv.update(act=act, x_h=x_h, dt_h=dt_h, ac_h=ac_h, ar_h=ar_h, xdt_h=xdt_h, y_cross=y_cross, s_fwd=s_fwd, ytil=ytil,
              y_ssd=y_ssd)

    ys = (y_ret, y_sb, y_ssd)
    ps = [matmul(y, wb, name="mm_branch") for y, wb in zip(ys, w["w_branch"])]
    merged = merge_fwd(proj, p["b_gate"], ps)
    mo = matmul(merged, w["w_out"], name="mm_out")
    x1 = rms_fwd(mo, p["norm_mix_post"], F32, "rms_mix_post", resid=x)
    sv.update(ps=ps, merged=merged, mo=mo, x1=x1)

    h2 = rms_fwd(x1, p["norm_ffn_pre"], BF16, "rms_ffn_pre")
    gu = matmul(h2, w["w_gu"], out_dtype=BF16, name="mm_gu")
    actf = swiglu_fwd(gu)
    f = matmul(actf, w["w_down"], name="mm_down")
    x2 = rms_fwd(f, p["norm_ffn_post"], F32, "rms_ffn_post", resid=x1)
    sv.update(h2=h2, gu=gu, actf=actf, f=f)
    return x2, sv


def layer_backward(dx2, sv, w, p, tabs):
    cos2, sin2, ac_ret, ar_ret = tabs
    proj = sv["proj"]
    s = proj.shape[0]
    gs = {}

    df, gs["norm_ffn_post"] = rms_bwd(sv["f"], p["norm_ffn_post"], dx2, "rms_ffn_post_bwd", out_dtype=BF16)
    dactf = matmul(df, w["w_down"], nt=True, name="mm_down_dx")
    gw_down = matmul(sv["actf"].T, df, out_dtype=BF16, name="mm_down_dw")
    dgu = swiglu_bwd(sv["gu"], dactf)
    dh2 = matmul(dgu, w["w_gu"], nt=True, name="mm_gu_dx")
    gw_gu = matmul(sv["h2"].T, dgu, out_dtype=BF16, name="mm_gu_dw")
    dx1, gs["norm_ffn_pre"] = rms_bwd(sv["x1"], p["norm_ffn_pre"], dh2, "rms_ffn_pre_bwd", resid=dx2)

    dmo, gs["norm_mix_post"] = rms_bwd(sv["mo"], p["norm_mix_post"], dx1, "rms_mix_post_bwd", out_dtype=BF16)
    dmerged = matmul(dmo, w["w_out"], nt=True, name="mm_out_dx")
    gw_out = matmul(sv["merged"].T, dmo, out_dtype=BF16, name="mm_out_dw")
    dps, dgl, gs["b_gate"] = merge_bwd(proj, p["b_gate"], sv["ps"], dmerged)
    ys = (sv["y_ret"], sv["y_sb"], sv["y_ssd"])
    dys = [matmul(dp, wb, nt=True, name="mm_branch_dx") for dp, wb in zip(dps, w["w_branch"])]
    gw_branch = [matmul(y.T, dp, out_dtype=BF16, name="mm_branch_dw") for y, dp in zip(ys, dps)]

    do_h, dg, gs["ret_gn_w"] = gn_gate_bwd(sv["o_ret"], p["ret_gn_w"], proj, dys[0])
    v_ret = (proj, COL_RET + 2 * BR_WIDTH, N_HEADS)
    ret = dict(chunk=ac_ret.shape[2], sub_chunk=RET_CHUNK, causal=False, heads=N_HEADS)
    dqr = linear_attention(do_h, v_ret, sv["kr"], ac_ret, ar_ret, reverse=False, name="ret_dq", **ret)
    dkr = linear_attention(v_ret, do_h, sv["qr"], ac_ret, ar_ret, reverse=True, name="ret_dk", **ret)
    dv = linear_attention(sv["kr"], sv["qr"], do_h, ac_ret, ar_ret, reverse=True, name="ret_dv",
                          out_dtype=BF16, out_2d=True, **ret)
    dq, dk = rotary_bwd(dqr, dkr, cos2, sin2)

    dsq, dsk, dsv = sb_backward(sv["qkv"], sv["o_sb"], dys[1])

    act = sv["act"]
    d_full = jnp.repeat(p["ssd_d"], SSD_P).reshape(1, BR_WIDTH)
    norm_w = p["ssd_norm_w"].reshape(1, BR_WIDTH)
    dy, dz, gs["ssd_norm_w"], dd_lanes = ssd_gate_bwd(sv["ytil"], act, proj, d_full, norm_w, dys[2])
    gs["ssd_d"] = dd_lanes.reshape(SSD_HEADS, SSD_P).sum(axis=1)
    dy_h = _to_heads(dy, SSD_HEADS, SSD_P)
    b_grp = (act, BR_WIDTH, SSD_GROUPS)
    c_grp = (act, BR_WIDTH + SSD_GROUPS * SSD_N, SSD_GROUPS)
    ssd = dict(chunk=SSD_CHUNK, causal=True, heads=SSD_HEADS)
    ac_h, ar_h, xdt_h = sv["ac_h"], sv["ar_h"], sv["xdt_h"]
    dxdt_h, dx_cross, s_rev = linear_attention(b_grp, c_grp, dy_h, ac_h, ar_h, reverse=True, name="ssd_dx",
                                               parts_out=True, **ssd)
    dc_h = linear_attention(dy_h, xdt_h, b_grp, ac_h, ar_h, reverse=False, name="ssd_dc", **ssd)
    db_h = linear_attention(xdt_h, dy_h, c_grp, ac_h, ar_h, reverse=True, name="ssd_db", **ssd)
    d_h = p["ssd_d"].reshape(SSD_HEADS, 1, 1)
    dx_h, e_h, v_h, ddtx_h = ssd_head_bwd(dy_h, sv["y_cross"], dxdt_h, dx_cross, xdt_h, sv["x_h"], sv["dt_h"], d_h)
    lanes = lambda a: jnp.pad(a[:, :, 0].T, ((0, 0), (0, LANES - SSD_HEADS)))
    g_chunk = ssd_state_dot(sv["s_fwd"], s_rev, ac_h[:, :, SSD_CHUNK - 1:, :])
    own = ssd_decay_grad_own(c_grp, b_grp, dy_h, xdt_h, ac_h, ar_h).reshape(SSD_HEADS, s, 1)
    da = ssd_decay_grad(lanes(e_h), lanes(v_h), lanes(g_chunk[:, :, :, 0])[:, None, :], lanes(own))
    dt_bias, a_log = _pad_lanes(p["ssd_dt_bias"]), _pad_lanes(p["ssd_a_log"])
    ddtr, dbias, dalog = ssd_dt_bwd(da, lanes(ddtx_h), proj, dt_bias, a_log)
    gs["ssd_dt_bias"], gs["ssd_a_log"] = dbias[:SSD_HEADS], dalog[:SSD_HEADS]
    db2, dc2 = ssd_group_sum(db_h, dc_h)
    dact = jnp.concatenate([_from_heads(dx_h), db2, dc2], axis=1)
    xbc = proj[:, COL_XBC:COL_XBC + XBC_WIDTH]
    us = _shifted(xbc, False) + [xbc]
    conv_b = p["ssd_conv_b"].reshape(1, XBC_WIDTH)
    dcv, gs["ssd_conv_w"], gs["ssd_conv_b"] = ssd_conv_bwd_a(us, p["ssd_conv_w"], conv_b, dact)
    dxbc = ssd_conv_bwd_b(_shifted(dcv, True) + [dcv], p["ssd_conv_w"])

    dt_cols = jnp.pad(ddtr, ((0, 0), (0, IN_PAD - COL_DT - LANES)))
    dproj = jnp.concatenate([dq, dk, dv, dg, dsq.astype(BF16), dsk.astype(BF16), dsv.astype(BF16),
                             dz, dxbc, dgl, dt_cols], axis=1)
    dh = matmul(dproj, w["w_in"], nt=True, name="mm_in_dx")
    gw_in = matmul(sv["h"].T, dproj, out_dtype=BF16, name="mm_in_dw")
    dx, gs["norm_mix_pre"] = rms_bwd(sv["x"], p["norm_mix_pre"], dh, "rms_mix_pre_bwd", resid=dx1)
    gw = dict(w_in=gw_in, w_branch=gw_branch, w_out=gw_out, w_gu=gw_gu, w_down=gw_down)
    return dx, gw, gs


def in_cols_pad(w):
    zeros = jnp.zeros(w.shape[:-1] + (IN_PAD - IN_COLS,), w.dtype)
    return jnp.concatenate([w[..., :COL_GATE], w[..., COL_GATE + SSD_HEADS:], w[..., COL_GATE:COL_GATE + SSD_HEADS], zeros],
                           axis=-1)


def in_cols_unpad(g):
    return jnp.concatenate([g[..., :COL_GATE], g[..., COL_DT:COL_DT + SSD_HEADS], g[..., COL_GATE:COL_DT]], axis=-1)


W_IN_SHARD = (D_MODEL, IN_COLS // N_SHARD)
BIG = (("w_branch_ret", (BR_WIDTH, D_MODEL // N_SHARD), True),
       ("w_branch_sb", (BR_WIDTH, D_MODEL // N_SHARD), True),
       ("w_branch_ssd", (BR_WIDTH, D_MODEL // N_SHARD), True),
       ("w_out", (D_MODEL // N_SHARD, D_MODEL), False),
       ("ffn_w_gate", (D_MODEL, FFN_HIDDEN // N_SHARD), True),
       ("ffn_w_up", (D_MODEL, FFN_HIDDEN // N_SHARD), True),
       ("ffn_w_down", (FFN_HIDDEN // N_SHARD, D_MODEL), False))
FLAT_ROWS = sum(r * c for _, (r, c), _ in BIG) // LANES
assert FLAT_ROWS * LANES == sum(r * c for _, (r, c), _ in BIG) and FLAT_ROWS % 32 == 0
HALF_ROWS = FLAT_ROWS // 2


def pack_shard(layer_shards, dtype):
    flat = jnp.concatenate([layer_shards[n].reshape(-1).astype(dtype) for n, _, _ in BIG])
    return flat.reshape(FLAT_ROWS, LANES)


def unpack_shard(flat):
    out, off = {}, 0
    v = flat.reshape(-1)
    for n, (r, c), _ in BIG:
        out[n] = v[off:off + r * c].reshape(r, c)
        off += r * c
    return out


def unpack_gathered(full):
    v = full.reshape(N_SHARD, -1)
    out, off = {}, 0
    for n, (r, c), by_cols in BIG:
        seg = v[:, off:off + r * c].reshape(N_SHARD, r, c)
        out[n] = seg.transpose(1, 0, 2).reshape(r, N_SHARD * c) if by_cols else seg.reshape(N_SHARD * r, c)
        off += r * c
    return out


def pack_for_scatter(whole):
    segs = []
    for n, (r, c), by_cols in BIG:
        g = whole[n].astype(BF16)
        g = g.reshape(r, N_SHARD, c).transpose(1, 0, 2) if by_cols else g.reshape(N_SHARD, r, c)
        segs.append(g.reshape(N_SHARD, r * c))
    return jnp.concatenate(segs, axis=1).reshape(N_SHARD, FLAT_ROWS, LANES)


ANY = pl.BlockSpec(memory_space=pl.ANY)


def _place():
    return lax.axis_index("x"), lax.axis_index("y"), lax.axis_index("c")


def _other_chips(x, y):
    return [(1 - x, y), (x, 1 - y), (1 - x, 1 - y)]


def _half(c, rows):
    return pl.ds(pl.multiple_of(c * (rows // 2), 16), rows // 2)


def _sems(n):
    return [pltpu.SemaphoreType.DMA((n,)), pltpu.SemaphoreType.DMA((n,))]


def all_gather_shards(shards):
    n = len(shards)

    def body(*refs):
        x_refs, o_refs = refs[:n], refs[n:2 * n]
        send_sems, recv_sems = refs[2 * n:]
        x, y, c = _place()
        j = 2 * x + y
        sibling = (x, y, 1 - c)
        chips = _other_chips(x, y)

        def copy(k, src, dst, to):
            return pltpu.make_async_remote_copy(src_ref=src, dst_ref=dst, send_sem=send_sems.at[k],
                                                recv_sem=recv_sems.at[k], device_id=to, device_id_type=MESH)

        started = []
        for a, (x_ref, o_ref) in enumerate(zip(x_refs, o_refs)):
            mine = _half(c, x_ref.shape[0])
            for k, (cx, cy) in enumerate(chips):
                cp = copy(6 * a + k, x_ref.at[mine], o_ref.at[j, mine], (cx, cy, c))
                cp.start()
                started.append(cp)
        for a, o_ref in enumerate(o_refs):
            mine = _half(c, o_ref.shape[1])
            for k, (cx, cy) in enumerate(chips):
                landed = o_ref.at[2 * cx + cy, mine]
                copy(6 * a + k, landed, landed, (cx, cy, c)).wait_recv()
                cp = copy(6 * a + 3 + k, landed, landed, sibling)
                cp.start()
                started.append(cp)
        for a, o_ref in enumerate(o_refs):
            other = _half(1 - c, o_ref.shape[1])
            for k, (cx, cy) in enumerate(chips):
                landed = o_ref.at[2 * cx + cy, other]
                copy(6 * a + 3 + k, landed, landed, sibling).wait_recv()
        for cp in started:
            cp.wait_send()

    gathered = pl.pallas_call(
        body, name="all_gather_shards", in_specs=[ANY] * n, out_specs=[ANY] * n,
        out_shape=[jax.ShapeDtypeStruct((N_SHARD,) + s.shape, s.dtype) for s in shards],
        scratch_shapes=_sems(6 * n),
    )(*shards)
    own = 2 * lax.axis_index("x") + lax.axis_index("y")
    slot = lambda g: lax.broadcasted_iota(jnp.int32, (N_SHARD,) + (1,) * (g.ndim - 1), 0)
    return [jnp.where(slot(g) == own, s[None], g) for s, g in zip(shards, gathered)]


def exchange_siblings(gs):
    n = len(gs)

    def body(*refs):
        g_refs, o_refs = refs[:n], refs[n:2 * n]
        send_sems, recv_sems = refs[2 * n:]
        x, y, c = _place()
        copies = []
        for a, (g_ref, o_ref) in enumerate(zip(g_refs, o_refs)):
            cp = pltpu.make_async_remote_copy(src_ref=g_ref.at[:, _half(1 - c, g_ref.shape[1])], dst_ref=o_ref,
                                              send_sem=send_sems.at[a], recv_sem=recv_sems.at[a],
                                              device_id=(x, y, 1 - c), device_id_type=MESH)
            cp.start()
            copies.append(cp)
        for cp in copies:
            cp.wait()

    return pl.pallas_call(
        body, name="exchange_siblings", in_specs=[ANY] * n, out_specs=[ANY] * n,
        out_shape=[jax.ShapeDtypeStruct((N_SHARD, g.shape[1] // 2, g.shape[2]), g.dtype) for g in gs],
        scratch_shapes=_sems(n),
    )(*gs)


def exchange_chips(ps):
    n = len(ps)

    def body(*refs):
        p_refs, o_refs = refs[:n], refs[n:2 * n]
        send_sems, recv_sems, local_sems = refs[2 * n:]
        x, y, c = _place()
        j = 2 * x + y
        copies = []
        for a, (p_ref, o_ref) in enumerate(zip(p_refs, o_refs)):
            local = pltpu.make_async_copy(p_ref.at[j], o_ref.at[j], local_sems.at[a])
            local.start()
            copies.append(local)
            for k, (cx, cy) in enumerate(_other_chips(x, y)):
                cp = pltpu.make_async_remote_copy(src_ref=p_ref.at[2 * cx + cy], dst_ref=o_ref.at[j],
                                                  send_sem=send_sems.at[3 * a + k], recv_sem=recv_sems.at[3 * a + k],
                                                  device_id=(cx, cy, c), device_id_type=MESH)
                cp.start()
                copies.append(cp)
        for cp in copies:
            cp.wait()

    return pl.pallas_call(
        body, name="exchange_chips", in_specs=[ANY] * n, out_specs=[ANY] * n,
        out_shape=[jax.ShapeDtypeStruct(p.shape, p.dtype) for p in ps],
        scratch_shapes=_sems(3 * n) + [pltpu.SemaphoreType.DMA((n,))],
    )(*ps)


def swap_halves(reds):
    n = len(reds)

    def body(*refs):
        r_refs, o_refs = refs[:n], refs[n:2 * n]
        send_sems, recv_sems = refs[2 * n:]
        x, y, c = _place()
        copies = []
        for a, (r_ref, o_ref) in enumerate(zip(r_refs, o_refs)):
            cp = pltpu.make_async_remote_copy(src_ref=r_ref, dst_ref=o_ref, send_sem=send_sems.at[a],
                                              recv_sem=recv_sems.at[a], device_id=(x, y, 1 - c), device_id_type=MESH)
            cp.start()
            copies.append(cp)
        for cp in copies:
            cp.wait()

    return pl.pallas_call(
        body, name="swap_halves", in_specs=[ANY] * n, out_specs=[ANY] * n,
        out_shape=[jax.ShapeDtypeStruct(r.shape, r.dtype) for r in reds],
        scratch_shapes=_sems(n),
    )(*reds)


def _sum_rows(rows, cols):
    best = 16
    for t in range(16, rows + 1, 16):
        if rows % t == 0 and t * cols * 4 <= 1536 * 1024:
            best = t
    return best


def reduce_scatter(gs):
    c = lax.axis_index("c")
    gots = exchange_siblings(gs)
    pairs = []
    for g, got in zip(gs, gots):
        half, cols = got.shape[1], got.shape[2]
        t = _sum_rows(half, cols)
        own = lax.dynamic_slice_in_dim(g, c * half, half, axis=1)
        blk = lambda a: (a, (None, t, cols), lambda j, i: (j, i, 0))
        pairs.append(bmap("sum_siblings", lambda a, b: (a.astype(F32) + b.astype(F32),), (N_SHARD, half // t),
                          [blk(own), blk(got)], [(got.shape, BF16, (None, t, cols), lambda j, i: (j, i, 0))])[0])
    reds = []
    for parts in exchange_chips(pairs):
        half, cols = parts.shape[1], parts.shape[2]
        t = _sum_rows(half, cols)
        part = lambda j: (parts, (None, t, cols), lambda i: (j, i, 0))
        reds.append(bmap("sum_chips",
                         lambda a, b, c_, d: (((a.astype(F32) + b.astype(F32)) + c_.astype(F32)) + d.astype(F32),),
                         (half // t,), [part(j) for j in range(N_SHARD)],
                         [((half, cols), F32, (t, cols), lambda i: (i, 0))])[0])
    out = []
    for mine, theirs in zip(reds, swap_halves(reds)):
        both = jnp.where(c == 0, jnp.stack([mine, theirs]), jnp.stack([theirs, mine]))
        out.append(both.reshape(2 * mine.shape[0], mine.shape[1]))
    return out


def all_reduce_small(buf):
    r = buf.shape[0]

    def body(x_ref, o_ref, land_ref, send_sems, recv_sems):
        x, y, c = _place()
        me = 4 * x + 2 * y + c
        land_ref[me] = x_ref[...]
        copies = []
        for d in range(1, 8):
            peer = tuple(1 - v if (d >> s) & 1 else v for v, s in ((x, 2), (y, 1), (c, 0)))
            cp = pltpu.make_async_remote_copy(src_ref=x_ref, dst_ref=land_ref.at[me], send_sem=send_sems.at[d - 1],
                                              recv_sem=recv_sems.at[d - 1], device_id=peer, device_id_type=MESH)
            cp.start()
            copies.append(cp)
        for cp in copies:
            cp.wait()
        acc = land_ref[0]
        for s in range(1, 8):
            acc = acc + land_ref[s]
        o_ref[...] = acc

    vmem = pl.BlockSpec(memory_space=pltpu.VMEM)
    return pl.pallas_call(
        body, name="all_reduce_small", in_specs=[vmem], out_specs=vmem,
        out_shape=jax.ShapeDtypeStruct((r, LANES), F32),
        scratch_shapes=[pltpu.VMEM((8, r, LANES), F32), pltpu.SemaphoreType.DMA((7,)), pltpu.SemaphoreType.DMA((7,))],
    )(buf)


def adamw(w, g, m, v, name):
    r, c = w.shape
    tm = r
    for cand in (512, 256, 128, 64, 32, 16, 8):
        if r % cand == 0 and cand * c * 4 * 14 <= 24 * 1024 * 1024:
            tm = cand
            break
    bc1 = 1.0 - ADAM_B1 ** ADAM_STEP
    bc2 = 1.0 - ADAM_B2 ** ADAM_STEP

    def fn(wv, gv, mv, vv):
        m2 = ADAM_B1 * mv + (1.0 - ADAM_B1) * gv
        v2 = ADAM_B2 * vv + (1.0 - ADAM_B2) * (gv * gv)
        delta = -ADAM_LR * ((m2 / bc1) / (jnp.sqrt(v2 / bc2) + ADAM_EPS) + ADAM_WD * wv)
        return delta, m2, v2

    return bmap(name, fn, (r // tm,), [rows(a, tm) for a in (w, g, m, v)], [row_out(r, c, F32, tm)] * 3)


WEIGHTS = ("norm_mix_pre", "norm_mix_post", "norm_ffn_pre", "norm_ffn_post", "w_in", "b_gate", "ret_gn_w", "ssd_conv_w",
           "ssd_conv_b", "ssd_dt_bias", "ssd_a_log", "ssd_d", "ssd_norm_w", "w_branch_ret", "w_branch_sb", "w_branch_ssd",
           "w_out", "ffn_w_gate", "ffn_w_up", "ffn_w_down")
BIG_NAMES = tuple(n for n, _, _ in BIG)
SHARDED = ("w_in",) + BIG_NAMES
SMALL = tuple(n for n in WEIGHTS if n not in SHARDED and n != "ssd_conv_w")
CONV_SHARD = XBC_WIDTH // N_SHARD


def _pack_small(parts):
    flat = jnp.concatenate([p.reshape(-1).astype(F32) for p in parts])
    rows_ = -(-flat.shape[0] // (8 * LANES)) * 8
    return jnp.pad(flat, (0, rows_ * LANES - flat.shape[0])).reshape(rows_, LANES)


def _unpack_small(buf, shapes):
    v, out, off = buf.reshape(-1), [], 0
    for shp in shapes:
        n = int(np.prod(shp))
        out.append(v[off:off + n].reshape(shp))
        off += n
    return out


def kernel(x, positions, norm_mix_pre, norm_mix_post, norm_ffn_pre, norm_ffn_post, w_in, b_gate, ret_gn_w, ssd_conv_w, ssd_conv_b, ssd_dt_bias, ssd_a_log, ssd_d, ssd_norm_w, w_branch_ret, w_branch_sb, w_branch_ssd, w_out, ffn_w_gate, ffn_w_up, ffn_w_down, loss_target, m_norm_mix_pre, m_norm_mix_post, m_norm_ffn_pre, m_norm_ffn_post, m_w_in, m_b_gate, m_ret_gn_w, m_ssd_conv_w, m_ssd_conv_b, m_ssd_dt_bias, m_ssd_a_log, m_ssd_d, m_ssd_norm_w, m_w_branch_ret, m_w_branch_sb, m_w_branch_ssd, m_w_out, m_ffn_w_gate, m_ffn_w_up, m_ffn_w_down, v_norm_mix_pre, v_norm_mix_post, v_norm_ffn_pre, v_norm_ffn_post, v_w_in, v_b_gate, v_ret_gn_w, v_ssd_conv_w, v_ssd_conv_b, v_ssd_dt_bias, v_ssd_a_log, v_ssd_d, v_ssd_norm_w, v_w_branch_ret, v_w_branch_sb, v_w_branch_ssd, v_w_out, v_ffn_w_gate, v_ffn_w_up, v_ffn_w_down):
    given = dict(locals())
    wts = {n: given[n] for n in WEIGHTS}
    mom = {n: given["m_" + n] for n in WEIGHTS}
    var = {n: given["v_" + n] for n in WEIGHTS}
    xi, yi, ci = lax.axis_index("x"), lax.axis_index("y"), lax.axis_index("c")
    shard_id = 2 * xi + yi

    conv_mine = jnp.where(ci == 0, ssd_conv_w, 0.0)
    conv_slots = lax.dynamic_update_slice_in_dim(jnp.zeros((DEPTH, SSD_CONV, XBC_WIDTH), F32), conv_mine,
                                                 shard_id * CONV_SHARD, axis=2)
    conv_buf = _pack_small([conv_slots])
    conv_whole = _unpack_small(all_reduce_small(conv_buf), [(DEPTH, SSD_CONV, XBC_WIDTH)])[0]

    tabs = rope_tables(positions[0]) + ret_decay_tables(x.shape[1])
    layer_w, layer_p = [], []
    for l in range(DEPTH):
        flat, w_in_all = all_gather_shards([pack_shard({n: wts[n][l] for n in BIG_NAMES}, BF16), w_in[l].astype(BF16)])
        full = unpack_gathered(flat)
        layer_w.append(dict(
            w_in=in_cols_pad(w_in_all.transpose(1, 0, 2).reshape(D_MODEL, IN_COLS)),
            w_branch=[full["w_branch_ret"], full["w_branch_sb"], full["w_branch_ssd"]],
            w_out=full["w_out"],
            w_gu=jnp.concatenate([full["ffn_w_gate"], full["ffn_w_up"]], axis=1),
            w_down=full["ffn_w_down"]))
        p = {n: wts[n][l] for n in SMALL}
        p["ssd_conv_w"] = conv_whole[l]
        layer_p.append(p)

    act = x[0]
    saved = []
    for l in range(DEPTH):
        act, sv = layer_forward(act, layer_w[l], layer_p[l], tabs)
        saved.append(sv)
    dact, loss_part = loss_head(act, loss_target[0])

    big_grads = [None] * DEPTH
    small_grads = [None] * DEPTH
    for l in reversed(range(DEPTH)):
        dact, gw, small_grads[l] = layer_backward(dact, saved[l], layer_w[l], layer_p[l], tabs)
        gu = gw["w_gu"]
        whole = dict(w_branch_ret=gw["w_branch"][0], w_branch_sb=gw["w_branch"][1],
                     w_branch_ssd=gw["w_branch"][2], w_out=gw["w_out"], ffn_w_gate=gu[:, :FFN_HIDDEN],
                     ffn_w_up=gu[:, FFN_HIDDEN:], ffn_w_down=gw["w_down"])
        g_in = in_cols_unpad(gw["w_in"]).reshape(D_MODEL, N_SHARD, IN_COLS // N_SHARD).transpose(1, 0, 2)
        red_flat, red_in = reduce_scatter([pack_for_scatter(whole), g_in])
        big_grads[l] = unpack_shard(red_flat)
        big_grads[l]["w_in"] = red_in

    small_names = SMALL + ("ssd_conv_w",)
    small_parts = [jnp.stack([small_grads[l][n] for l in range(DEPTH)]) for n in small_names]
    small_shapes = [p.shape for p in small_parts]
    summed = all_reduce_small(_pack_small([loss_part.reshape(1)] + small_parts))
    loss, *small_sum = _unpack_small(summed, [(1,)] + small_shapes)
    grads = dict(zip(small_names, small_sum))
    grads["ssd_conv_w"] = lax.dynamic_slice_in_dim(grads["ssd_conv_w"], shard_id * CONV_SHARD, CONV_SHARD, axis=2)
    for n in SHARDED:
        grads[n] = jnp.stack([big_grads[l][n] for l in range(DEPTH)])

    delta, new_m, new_v = {}, {}, {}
    for n in WEIGHTS:
        shape = wts[n].shape
        two_d = (-1, shape[-1]) if len(shape) == 3 else (8, -1) if shape[-1] * shape[0] % (8 * LANES) == 0 else shape
        args = [a.reshape(two_d) for a in (wts[n], grads[n], mom[n], var[n])]
        d, m2, v2 = adamw(*args, name="adamw_" + n)
        delta[n], new_m[n], new_v[n] = d.reshape(shape), m2.reshape(shape), v2.reshape(shape)

    return (loss.reshape(()), dact[None], *[grads[n] for n in WEIGHTS], *[delta[n] for n in WEIGHTS],
            *[new_m[n] for n in WEIGHTS], *[new_v[n] for n in WEIGHTS])
```
